```python
import math
import jax, jax.numpy as jnp
from jax import lax
import numpy as np

D_MODEL = 1024
BATCH = 8
SEQ = 2048
DEPTH = 1
DEC_BATCH = 128
DEC_SEQ = 4
PAST_LEN = 16384
PAGE_SIZE = 128

D_MIX = D_MODEL
D_A = D_MIX // 2
D_B = D_MIX - D_A
H_A = 4
DK_A = D_A // H_A
DV_A = D_A // H_A
N_BLK_B = 8
BLK_B = D_B // N_BLK_B
CONV_W = 4
LRU_C = 8.0
D_FF = 2816
CHUNK = 64
EPS = 1e-6
D_IN = 4 * D_A + 2 * D_B

kernel_name = "hymba_hgrn2_rglru_macaron_step"


def rmsnorm(x, g):
    xf = x.astype(jnp.float32)
    y = xf * lax.rsqrt(jnp.mean(xf * xf, axis=-1, keepdims=True) + EPS)
    return (y * g.astype(jnp.float32)).astype(x.dtype)


def swiglu(x, wg, wu, wd):
    return (jax.nn.silu(x @ wg) * (x @ wu)) @ wd


def hgrn2_chunked(q, logf, k, v, s0):
    b, t, h, dk = q.shape
    dv = v.shape[-1]
    c = math.gcd(t, CHUNK)
    n = t // c

    def to_chunks(a):
        return a.astype(jnp.float32).reshape(b, n, c, h, a.shape[-1]).transpose(1, 0, 3, 2, 4)

    qc, gc, kc, vc = to_chunks(q), to_chunks(logf), to_chunks(k), to_chunks(v)
    causal = jnp.tril(jnp.ones((c, c), dtype=bool))[:, :, None]

    def step(s, inp):
        qi, gi, ki, vi = inp
        cum = jnp.cumsum(gi, axis=2)
        diff = cum[:, :, :, None, :] - cum[:, :, None, :, :]
        decay = jnp.exp(jnp.where(causal, diff, -jnp.inf))
        scores = jnp.einsum('bhtk,bhsk,bhtsk->bhts', qi, ki, decay)
        o = (jnp.einsum('bhts,bhsv->bhtv', scores, vi)
             + jnp.einsum('bhtk,bhkv->bhtv', qi * jnp.exp(cum), s))
        last = cum[:, :, -1:, :]
        s_new = (jnp.exp(last[:, :, 0, :])[..., None] * s
                 + jnp.einsum('bhsk,bhsv->bhkv', ki * jnp.exp(last - cum), vi))
        return s_new, o

    s_fin, o = lax.scan(step, s0.astype(jnp.float32), (qc, gc, kc, vc))
    o = o.transpose(1, 0, 3, 2, 4).reshape(b, t, h, dv)
    return o, s_fin


def causal_conv(x, buf, w, bias):
    t = x.shape[1]
    xp = jnp.concatenate([buf.astype(x.dtype), x], axis=1)
    y = bias + sum(xp[:, j:j + t] * w[j] for j in range(CONV_W))
    return y, xp[:, -(CONV_W - 1):]


def block_diag(x, w, bias):
    b, t, _ = x.shape
    xb = x.reshape(b, t, N_BLK_B, BLK_B)
    return (jnp.einsum('btni,nij->btnj', xb, w) + bias).reshape(b, t, D_B)


def rglru(x, h0, w_a, b_a, w_x, b_x, lam, is_start):
    xf = x.astype(jnp.float32)
    r = jax.nn.sigmoid(block_diag(xf, w_a.astype(jnp.float32), b_a.astype(jnp.float32)))
    i = jax.nn.sigmoid(block_diag(xf, w_x.astype(jnp.float32), b_x.astype(jnp.float32)))
    log_a = -LRU_C * r * jax.nn.softplus(-lam.astype(jnp.float32))
    a = jnp.exp(log_a)
    mult = jnp.sqrt(-jnp.expm1(2.0 * log_a))
    if is_start:
        mult = mult.at[:, 0].set(1.0)
    u = mult * i * xf
    if h0 is not None:
        u = u.at[:, 0].add(a[:, 0] * h0.astype(jnp.float32))

    def comb(c1, c2):
        a1, b1 = c1
        a2, b2 = c2
        return a1 * a2, a2 * b1 + b2

    _, h = lax.associative_scan(comb, (a, u), axis=1)
    return h, h[:, -1]


def mixer(xn, lb, w_in, hgrn_norm, conv_w, conv_b, lru_wa, lru_ba, lru_wx, lru_bx, lru_lambda, w_o,
          s_hgrn, h_lru, conv_buf, is_start):
    b, t, _ = xn.shape
    proj = xn @ w_in
    q, fz, iv, g, xb, yb = jnp.split(proj, [D_A, 2 * D_A, 3 * D_A, 4 * D_A, 4 * D_A + D_B], axis=-1)
    f = lb + (1.0 - lb) * jax.nn.sigmoid(fz.astype(jnp.float32))
    logf = jnp.log(f)
    k = 1.0 - f
    qh = jax.nn.silu(q).reshape(b, t, H_A, DK_A)
    o_a, s_new = hgrn2_chunked(qh, logf.reshape(b, t, H_A, DK_A), k.reshape(b, t, H_A, DK_A),
                               iv.reshape(b, t, H_A, DV_A), s_hgrn)
    o_a = rmsnorm(o_a, hgrn_norm.reshape(H_A, DV_A)).reshape(b, t, D_A)
    o_a = (o_a * jax.nn.sigmoid(g.astype(jnp.float32))).astype(xn.dtype)
    xc, buf_new = causal_conv(xb, conv_buf, conv_w, conv_b)
    h, h_last = rglru(xc, h_lru, lru_wa, lru_ba, lru_wx, lru_bx, lru_lambda, is_start)
    o_b = (h * jax.nn.gelu(yb.astype(jnp.float32), approximate=True)).astype(xn.dtype)
    y = jnp.concatenate([o_a, o_b], axis=-1) @ w_o
    return y, s_new, h_last, buf_new


def trunk(x, s_hgrn, h_lru, conv_buf, is_start, ffn1_norm, ffn1_wg, ffn1_wu, ffn1_wd, mix_norm, w_in,
          hgrn_lb, hgrn_norm, conv_w, conv_b, lru_wa, lru_ba, lru_wx, lru_bx, lru_lambda, w_o,
          ffn2_norm, ffn2_wg, ffn2_wu, ffn2_wd, final_norm):
    lb_all = jnp.cumsum(jax.nn.softmax(hgrn_lb.astype(jnp.float32), axis=0), axis=0)
    ns_h, ns_l, ns_c = [], [], []
    for l in range(DEPTH):
        x = x + 0.5 * swiglu(rmsnorm(x, ffn1_norm[l]), ffn1_wg[l], ffn1_wu[l], ffn1_wd[l])
        m, s_new, h_last, buf_new = mixer(
            rmsnorm(x, mix_norm[l]), lb_all[l], w_in[l], hgrn_norm[l], conv_w[l], conv_b[l],
            lru_wa[l], lru_ba[l], lru_wx[l], lru_bx[l], lru_lambda[l], w_o[l],
            s_hgrn[l], None if h_lru is None else h_lru[l], conv_buf[l], is_start)
        x = x + m
        x = x + 0.5 * swiglu(rmsnorm(x, ffn2_norm[l]), ffn2_wg[l], ffn2_wu[l], ffn2_wd[l])
        ns_h.append(s_new)
        ns_l.append(h_last)
        ns_c.append(buf_new)
    return rmsnorm(x, final_norm), jnp.stack(ns_h), jnp.stack(ns_l), jnp.stack(ns_c)


def setup_inputs(seed: int = 0) -> dict:
    key = jax.random.key(seed)
    ks = jax.random.split(key, 32)
    f32 = jnp.float32

    def nrm(k, shape, scale):
        return jax.random.normal(k, shape, f32) * scale

    u = jax.random.uniform(ks[20], (DEPTH, D_B), f32, 0.9, 0.999)
    a_base = u ** (1.0 / LRU_C)
    lru_lambda = jnp.log(a_base) - jnp.log1p(-a_base)
    return {
        "x_prompt": nrm(ks[0], (BATCH, SEQ, D_MODEL), 1.0),
        "x_sample": nrm(ks[1], (DEC_BATCH, DEC_SEQ, D_MODEL), 1.0),
        "state_hgrn": nrm(ks[2], (DEPTH, DEC_BATCH, H_A, DK_A, DV_A), 0.5),
        "state_lru": nrm(ks[3], (DEPTH, DEC_BATCH, D_B), 0.5),
        "state_conv": nrm(ks[4], (DEPTH, DEC_BATCH, CONV_W - 1, D_B), 1.0),
        "ffn1_norm": 1.0 + nrm(ks[5], (DEPTH, D_MODEL), 0.02),
        "ffn1_wg": nrm(ks[6], (DEPTH, D_MODEL, D_FF), D_MODEL ** -0.5),
        "ffn1_wu": nrm(ks[7], (DEPTH, D_MODEL, D_FF), D_MODEL ** -0.5),
        "ffn1_wd": nrm(ks[8], (DEPTH, D_FF, D_MODEL), D_FF ** -0.5),
        "mix_norm": 1.0 + nrm(ks[9], (DEPTH, D_MODEL), 0.02),
        "w_in": nrm(ks[10], (DEPTH, D_MODEL, D_IN), D_MODEL ** -0.5),
        "hgrn_lb": nrm(ks[11], (DEPTH + 1, D_A), 0.1),
        "hgrn_norm": 1.0 + nrm(ks[12], (DEPTH, D_A), 0.02),
        "conv_w": nrm(ks[13], (DEPTH, CONV_W, D_B), CONV_W ** -0.5),
        "conv_b": nrm(ks[14], (DEPTH, D_B), 0.01),
        "lru_wa": nrm(ks[15], (DEPTH, N_BLK_B, BLK_B, BLK_B), BLK_B ** -0.5),
        "lru_ba": nrm(ks[16], (DEPTH, N_BLK_B, BLK_B), 0.01),
        "lru_wx": nrm(ks[17], (DEPTH, N_BLK_B, BLK_B, BLK_B), BLK_B ** -0.5),
        "lru_bx": nrm(ks[18], (DEPTH, N_BLK_B, BLK_B), 0.01),
        "lru_lambda": lru_lambda,
        "w_o": nrm(ks[19], (DEPTH, D_MIX, D_MODEL), D_MIX ** -0.5),
        "ffn2_norm": 1.0 + nrm(ks[21], (DEPTH, D_MODEL), 0.02),
        "ffn2_wg": nrm(ks[22], (DEPTH, D_MODEL, D_FF), D_MODEL ** -0.5),
        "ffn2_wu": nrm(ks[23], (DEPTH, D_MODEL, D_FF), D_MODEL ** -0.5),
        "ffn2_wd": nrm(ks[24], (DEPTH, D_FF, D_MODEL), D_FF ** -0.5),
        "final_norm": 1.0 + nrm(ks[25], (D_MODEL,), 0.02),
    }


def reference(x_prompt, x_sample, state_hgrn, state_lru, state_conv, ffn1_norm, ffn1_wg, ffn1_wu, ffn1_wd,
              mix_norm, w_in, hgrn_lb, hgrn_norm, conv_w, conv_b, lru_wa, lru_ba, lru_wx, lru_bx, lru_lambda,
              w_o, ffn2_norm, ffn2_wg, ffn2_wu, ffn2_wd, final_norm):
    weights = (ffn1_norm, ffn1_wg, ffn1_wu, ffn1_wd, mix_norm, w_in, hgrn_lb, hgrn_norm, conv_w, conv_b,
               lru_wa, lru_ba, lru_wx, lru_bx, lru_lambda, w_o, ffn2_norm, ffn2_wg, ffn2_wu, ffn2_wd, final_norm)
    s0_p = jnp.zeros((DEPTH, BATCH, H_A, DK_A, DV_A), jnp.float32)
    buf0_p = jnp.zeros((DEPTH, BATCH, CONV_W - 1, D_B), x_prompt.dtype)
    y_prompt, hgrn_p, lru_p, conv_p = trunk(x_prompt, s0_p, None, buf0_p, True, *weights)
    y_sample, hgrn_s, lru_s, conv_s = trunk(x_sample, state_hgrn, state_lru, state_conv, False, *weights)
    return (y_prompt, y_sample, hgrn_p, lru_p, conv_p, hgrn_s, lru_s, conv_s)
```

```python
import functools
import math

import jax
import jax.numpy as jnp
from jax import lax
from jax.experimental import pallas as pl
from jax.experimental.pallas import tpu as pltpu

F32 = jnp.float32
BF16 = jnp.bfloat16

D_MODEL = 1024
D_FF = 2816
D_A = 512
D_B = 512
N_HEADS = 4
D_HEAD = 128
D_IN = 4 * D_A + 2 * D_B
CONV_W = 4
LRU_C = 8.0
EPS = 1e-6

PAIR_BLOCK = 16
FFN_TILE = 512
FFN_CHUNK = 256
MIX_CHUNK = 128
SAMPLE_GROUP = 16
VMEM_LIMIT_BYTES = 56 * 1024 * 1024

_NT = (((1,), (1,)), ((), ()))


def _dot(a, b):
    return jnp.dot(a, b, preferred_element_type=F32)


def _dot_nt(a, b):
    return lax.dot_general(a, b, _NT, preferred_element_type=F32)


def _rms(x, w):
    ms = jnp.mean(x * x, axis=-1, keepdims=True)
    return x * lax.rsqrt(ms + EPS) * w


def _sigmoid(x):
    return 1.0 / (1.0 + jnp.exp(-x))


def _log1p(x):
    u = 1.0 + x
    return jnp.where(u == 1.0, x, jnp.log(u) * x / (u - 1.0))


def _expm1(x):
    u = jnp.exp(x)
    return jnp.where(u == 1.0, x, (u - 1.0) * x / jnp.log(u))


def _softplus(x):
    return jnp.maximum(x, 0.0) + _log1p(jnp.exp(-jnp.abs(x)))


def _gelu_tanh(x):
    return 0.5 * x * (1.0 + jnp.tanh(math.sqrt(2.0 / math.pi) * (x + 0.044715 * (x * x * x))))


def _split3_bf16(x):
    x1 = x.astype(BF16)
    r1 = x - x1.astype(F32)
    x2 = r1.astype(BF16)
    r2 = r1 - x2.astype(F32)
    return x1, x2, r2.astype(BF16)


def _ffn_kernel(x_ref, nw_ref, wg_ref, wu_ref, wd_ref, fn_ref, o_ref, *, apply_final_norm):
    x = x_ref[...]
    xn = _rms(x, nw_ref[...]).astype(BF16)
    acc = jnp.zeros(x.shape, F32)
    for c in range(D_FF // FFN_CHUNK):
        sl = slice(c * FFN_CHUNK, (c + 1) * FFN_CHUNK)
        g = _dot(xn, wg_ref[:, sl])
        u = _dot(xn, wu_ref[:, sl])
        h = (g * _sigmoid(g) * u).astype(BF16)
        acc = acc + _dot(h, wd_ref[sl, :])
    y = x + 0.5 * acc
    if apply_final_norm:
        y = _rms(y, fn_ref[...])
    o_ref[...] = y


def _resident(shape):
    zeros = (0,) * len(shape)
    return pl.BlockSpec(shape, lambda *_: zeros, pipeline_mode=pl.Buffered(1))


def _ffn(x, norm_w, wg, wu, wd, final_w, *, apply_final_norm, name):
    n = x.shape[0]
    tm = min(FFN_TILE, n)
    assert n % tm == 0 and D_FF % FFN_CHUNK == 0
    row = pl.BlockSpec((tm, D_MODEL), lambda i: (i, 0))
    return pl.pallas_call(
        functools.partial(_ffn_kernel, apply_final_norm=apply_final_norm),
        out_shape=jax.ShapeDtypeStruct((n, D_MODEL), F32),
        grid=(n // tm,),
        in_specs=[row, _resident((1, D_MODEL)), _resident((D_MODEL, D_FF)), _resident((D_MODEL, D_FF)),
                  _resident((D_FF, D_MODEL)), _resident((1, D_MODEL))],
        out_specs=row,
        compiler_params=pltpu.CompilerParams(dimension_semantics=("arbitrary",),
                                             vmem_limit_bytes=VMEM_LIMIT_BYTES),
        name=name,
    )(x, norm_w, wg, wu, wd, final_w)


def _hgrn_gates(proj_ref, lb_ref):
    l0 = lb_ref[0:1, :]
    l1 = lb_ref[1:2, :]
    m = jnp.maximum(l0, l1)
    e0 = jnp.exp(l0 - m)
    e1 = jnp.exp(l1 - m)
    lb = e0 / (e0 + e1)
    q = proj_ref[:, 0:D_A]
    fz = proj_ref[:, D_A:2 * D_A]
    f = lb + (1.0 - lb) * _sigmoid(fz)
    return q * _sigmoid(q), 1.0 - f, jnp.log(f)


def _pairwise(qs, k, cum, v, kpad_ref, cpad_ref, vpad_ref, block, n_rows):
    pad = kpad_ref.shape[0] - n_rows
    kpad_ref[pl.ds(pad, n_rows), :] = k
    cpad_ref[pl.ds(pad, n_rows), :] = cum
    vpad_ref[pl.ds(pad, n_rows), :] = v
    rmod = lax.broadcasted_iota(jnp.int32, (n_rows, 1), 0) & (block - 1)
    out = [jnp.zeros((n_rows, D_HEAD), F32) for _ in range(N_HEADS)]
    for d in range(block):
        if d == 0:
            p = qs * k
            vs = v
        else:
            ks = kpad_ref[pl.ds(pad - d, n_rows), :]
            cs = cpad_ref[pl.ds(pad - d, n_rows), :]
            vs = vpad_ref[pl.ds(pad - d, n_rows), :]
            p = qs * ks * jnp.exp(cum - cs)
        valid = rmod >= d
        for h in range(N_HEADS):
            hs = slice(h * D_HEAD, (h + 1) * D_HEAD)
            s = jnp.sum(p[:, hs], axis=-1, keepdims=True)
            s = jnp.where(valid, s, 0.0)
            out[h] = out[h] + s * vs[:, hs]
    return out


def _head_norm_gate(o_heads, hn_ref, gate):
    outs = []
    for h in range(N_HEADS):
        hs = slice(h * D_HEAD, (h + 1) * D_HEAD)
        o = o_heads[h]
        ms = jnp.mean(o * o, axis=-1, keepdims=True)
        outs.append(o * lax.rsqrt(ms + EPS) * hn_ref[:, hs] * _sigmoid(gate[:, hs]))
    return outs


def _lru_coeffs(xc, wa_ref, ba_ref, wx_ref, bx_ref, lam_ref):
    xcb = xc.astype(BF16)
    r = _sigmoid(_dot(xcb, wa_ref[...]) + ba_ref[...])
    i = _sigmoid(_dot(xcb, wx_ref[...]) + bx_ref[...])
    log_a = -LRU_C * r * _softplus(-lam_ref[...])
    a = jnp.exp(log_a)
    mult = jnp.sqrt(-_expm1(2.0 * log_a))
    return a, mult, i


def _scan_rows(a, u, rowpos, span):
    s = 1
    while s < span:
        keep = rowpos >= s
        a_sh = jnp.where(keep, pltpu.roll(a, s, 0), 1.0)
        u_sh = jnp.where(keep, pltpu.roll(u, s, 0), 0.0)
        u = a * u_sh + u
        a = a * a_sh
        s *= 2
    return a, u


def _mixer_prompt_kernel(x_ref, mn_ref, win_ref, lb_ref, hn_ref, cw_ref, cb_ref, wa_ref, ba_ref, wx_ref,
                         bx_ref, lam_ref, wo_ref,
                         y_ref, shg_ref, slru_ref, scv_ref,
                         proj_ref, st_ref, h_ref, xb_ref, kpad_ref, cpad_ref, vpad_ref, *, chunk):
    C = chunk
    t = pl.program_id(1)
    last = pl.num_programs(1) - 1

    @pl.when(t == 0)
    def _():
        st_ref[...] = jnp.zeros(st_ref.shape, F32)
        h_ref[...] = jnp.zeros(h_ref.shape, F32)
        xb_ref[0:8, :] = jnp.zeros((8, D_B), F32)
        zpad = jnp.zeros((PAIR_BLOCK, D_A), F32)
        kpad_ref[0:PAIR_BLOCK, :] = zpad
        cpad_ref[0:PAIR_BLOCK, :] = zpad
        vpad_ref[0:PAIR_BLOCK, :] = zpad

    x = x_ref[0]
    xn = _rms(x, mn_ref[...]).astype(BF16)
    proj_ref[...] = _dot(xn, win_ref[...])

    qs, k, g = _hgrn_gates(proj_ref, lb_ref)
    v = proj_ref[:, 2 * D_A:3 * D_A]
    row = lax.broadcasted_iota(jnp.int32, (C, C), 0)
    col = lax.broadcasted_iota(jnp.int32, (C, C), 1)
    tri = jnp.where(col <= row, 1.0, 0.0).astype(BF16)
    g1, g2, g3 = _split3_bf16(g)
    cum = _dot(tri, g1) + _dot(tri, g2) + _dot(tri, g3)

    o_pw = _pairwise(qs, k, cum, v, kpad_ref, cpad_ref, vpad_ref, PAIR_BLOCK, C)

    rowi = lax.broadcasted_iota(jnp.int32, (C, 1), 0)
    scores = [jnp.zeros((C, C), F32) for _ in range(N_HEADS)]
    L = 2 * PAIR_BLOCK
    while L <= C:
        mid = jnp.concatenate(
            [jnp.broadcast_to(cum[j * L + L // 2 - 1:j * L + L // 2, :], (L, D_A)) for j in range(C // L)], axis=0)
        e = jnp.exp(-jnp.abs(cum - mid))
        second = (rowi & (L - 1)) >= (L // 2)
        qt = jnp.where(second, qs * e, 0.0).astype(BF16)
        kt = jnp.where(second, 0.0, k * e).astype(BF16)
        shift = L.bit_length() - 1
        same = (row >> shift) == (col >> shift)
        for h in range(N_HEADS):
            hs = slice(h * D_HEAD, (h + 1) * D_HEAD)
            s = _dot_nt(qt[:, hs], kt[:, hs])
            scores[h] = scores[h] + (s if L == C else jnp.where(same, s, 0.0))
        L *= 2

    cl = cum[C - 1:C, :]
    qbar = (qs * jnp.exp(cum)).astype(BF16)
    khat = (k * jnp.exp(cl - cum)).astype(BF16)
    dec = jnp.exp(cl)
    vb = v.astype(BF16)
    o_heads = []
    for h in range(N_HEADS):
        hs = slice(h * D_HEAD, (h + 1) * D_HEAD)
        st = st_ref[h]
        o_heads.append(_dot(scores[h].astype(BF16), vb[:, hs]) + _dot_nt(qbar[:, hs], st.astype(BF16)) + o_pw[h])
        st_ref[h] = st * dec[:, hs] + _dot(v[:, hs].T.astype(BF16), khat[:, hs])
    oa = _head_norm_gate(o_heads, hn_ref, proj_ref[:, 3 * D_A:4 * D_A])

    xb = proj_ref[:, 4 * D_A:4 * D_A + D_B]
    yb = proj_ref[:, 4 * D_A + D_B:D_IN]
    xb_ref[pl.ds(8, C), :] = xb
    xc = (cb_ref[...] + cw_ref[3:4, :] * xb + cw_ref[2:3, :] * xb_ref[pl.ds(7, C), :]
          + cw_ref[1:2, :] * xb_ref[pl.ds(6, C), :] + cw_ref[0:1, :] * xb_ref[pl.ds(5, C), :])
    xb_ref[0:8, :] = xb[C - 8:C, :]
    a, mult, gate_i = _lru_coeffs(xc, wa_ref, ba_ref, wx_ref, bx_ref, lam_ref)
    mult = jnp.where((rowi == 0) & (t == 0), 1.0, mult)
    a_run, h_run = _scan_rows(a, mult * gate_i * xc, rowi, C)
    hseq = h_run + a_run * h_ref[...]
    h_ref[...] = hseq[C - 1:C, :]
    ob = hseq * _gelu_tanh(yb)

    o = jnp.concatenate(oa + [ob], axis=-1).astype(BF16)
    y_ref[0] = x + _dot(o, wo_ref[...])

    @pl.when(t == last)
    def _():
        for h in range(N_HEADS):
            shg_ref[0, h] = st_ref[h].T
        slru_ref[0] = hseq[C - 1:C, :]
        scv_ref[0] = xb[C - (CONV_W - 1):C, :]


def _mixer_prompt(x, mn, win, lb, hn, cw, cb, wa, ba, wx, bx, lam, wo):
    b, t, _ = x.shape
    C = MIX_CHUNK
    assert t % C == 0 and C % (2 * PAIR_BLOCK) == 0 and C & (C - 1) == 0
    seq = pl.BlockSpec((1, C, D_MODEL), lambda i, j: (i, j, 0))
    vec = lambda n: _resident((1, n))
    out_shape = (jax.ShapeDtypeStruct((b, t, D_MODEL), F32),
                 jax.ShapeDtypeStruct((b, N_HEADS, D_HEAD, D_HEAD), F32),
                 jax.ShapeDtypeStruct((b, 1, D_B), F32),
                 jax.ShapeDtypeStruct((b, CONV_W - 1, D_B), F32))
    out_specs = (seq,
                 pl.BlockSpec((1, N_HEADS, D_HEAD, D_HEAD), lambda i, j: (i, 0, 0, 0)),
                 pl.BlockSpec((1, 1, D_B), lambda i, j: (i, 0, 0)),
                 pl.BlockSpec((1, CONV_W - 1, D_B), lambda i, j: (i, 0, 0)))
    scratch = [pltpu.VMEM((C, D_IN), F32),
               pltpu.VMEM((N_HEADS, D_HEAD, D_HEAD), F32),
               pltpu.VMEM((1, D_B), F32),
               pltpu.VMEM((C + 8, D_B), F32),
               pltpu.VMEM((C + PAIR_BLOCK, D_A), F32),
               pltpu.VMEM((C + PAIR_BLOCK, D_A), F32),
               pltpu.VMEM((C + PAIR_BLOCK, D_A), F32)]
    return pl.pallas_call(
        functools.partial(_mixer_prompt_kernel, chunk=C),
        out_shape=out_shape,
        grid=(b, t // C),
        in_specs=[seq, vec(D_MODEL), _resident((D_MODEL, D_IN)), _resident((2, D_A)), vec(D_A),
                  _resident((CONV_W, D_B)), vec(D_B), _resident((D_B, D_B)), vec(D_B), _resident((D_B, D_B)),
                  vec(D_B), vec(D_B), _resident((D_A + D_B, D_MODEL))],
        out_specs=out_specs,
        scratch_shapes=scratch,
        compiler_params=pltpu.CompilerParams(dimension_semantics=("arbitrary", "arbitrary"),
                                             vmem_limit_bytes=VMEM_LIMIT_BYTES),
        name="mixer_prompt",
    )(x, mn, win, lb, hn, cw, cb, wa, ba, wx, bx, lam, wo)


def _mixer_sample_kernel(x_ref, mn_ref, win_ref, lb_ref, hn_ref, cw_ref, cb_ref, wa_ref, ba_ref, wx_ref,
                         bx_ref, lam_ref, wo_ref, s0_ref, h0_ref, hist1_ref, hist2_ref, hist3_ref,
                         y_ref, s1_ref, hseq_ref, xb_out_ref,
                         proj_ref, xb_ref, kpad_ref, cpad_ref, vpad_ref, qbar_ref, ost_ref, *, steps):
    T = steps
    R = x_ref.shape[0]
    G = R // T
    pad = kpad_ref.shape[0] - R

    zpad = jnp.zeros((pad, D_A), F32)
    kpad_ref[0:pad, :] = zpad
    cpad_ref[0:pad, :] = zpad
    vpad_ref[0:pad, :] = zpad
    xb_ref[0:8, :] = jnp.zeros((8, D_B), F32)

    x = x_ref[...]
    xn = _rms(x, mn_ref[...]).astype(BF16)
    proj_ref[...] = _dot(xn, win_ref[...])

    qs, k, g = _hgrn_gates(proj_ref, lb_ref)
    v = proj_ref[:, 2 * D_A:3 * D_A]
    row = lax.broadcasted_iota(jnp.int32, (R, R), 0)
    col = lax.broadcasted_iota(jnp.int32, (R, R), 1)
    tshift = T.bit_length() - 1
    tri = jnp.where((col <= row) & ((row >> tshift) == (col >> tshift)), 1.0, 0.0).astype(BF16)
    g1, g2, g3 = _split3_bf16(g)
    cum = _dot(tri, g1) + _dot(tri, g2) + _dot(tri, g3)
    o_pw = _pairwise(qs, k, cum, v, kpad_ref, cpad_ref, vpad_ref, T, R)

    ecum = jnp.exp(cum)
    qbar_ref[...] = qs * ecum
    upper = jnp.where((col > row) & ((row >> tshift) == (col >> tshift)), 1.0, 0.0).astype(BF16)
    tail = _dot(upper, g1) + _dot(upper, g2) + _dot(upper, g3)
    khat = k * jnp.exp(tail)
    row8 = lax.broadcasted_iota(jnp.int32, (8, 1), 0)
    for h in range(N_HEADS):
        hs = slice(h * D_HEAD, (h + 1) * D_HEAD)
        khat_t = khat[:, hs].T
        dec_t = ecum[:, hs].T
        for p in range(R // 8):
            q8 = qbar_ref[pl.ds(8 * p, 8), hs].astype(BF16)
            o8 = jnp.zeros((8, D_HEAD), F32)
            for i in range(8 // T):
                b = p * (8 // T) + i
                s0 = s0_ref[b, h]
                ob = _dot(q8, s0.astype(BF16))
                o8 = jnp.where((row8 >= i * T) & (row8 < (i + 1) * T), ob, o8)
                acc = s0 * dec_t[:, b * T + T - 1:b * T + T]
                for tt in range(T):
                    j = b * T + tt
                    acc = acc + khat_t[:, j:j + 1] * v[j:j + 1, hs]
                s1_ref[b, h] = acc
            ost_ref[pl.ds(8 * p, 8), hs] = o8
    o_heads = [o_pw[h] + ost_ref[:, h * D_HEAD:(h + 1) * D_HEAD] for h in range(N_HEADS)]
    oa = _head_norm_gate(o_heads, hn_ref, proj_ref[:, 3 * D_A:4 * D_A])

    rowi = lax.broadcasted_iota(jnp.int32, (R, 1), 0)
    tpos = rowi & (T - 1)
    xb = proj_ref[:, 4 * D_A:4 * D_A + D_B]
    yb = proj_ref[:, 4 * D_A + D_B:D_IN]
    xb_ref[pl.ds(8, R), :] = xb
    xb_out_ref[...] = xb
    hist = (hist1_ref, hist2_ref, hist3_ref)
    xc = cb_ref[...] + cw_ref[CONV_W - 1:CONV_W, :] * xb
    for d in range(1, CONV_W):
        prev = jnp.where(tpos >= d, xb_ref[pl.ds(8 - d, R), :], hist[d - 1][...])
        xc = xc + cw_ref[CONV_W - 1 - d:CONV_W - d, :] * prev
    a, mult, gate_i = _lru_coeffs(xc, wa_ref, ba_ref, wx_ref, bx_ref, lam_ref)
    a_run, h_run = _scan_rows(a, mult * gate_i * xc, tpos, T)
    hseq = h_run + a_run * h0_ref[...]
    hseq_ref[...] = hseq
    ob = hseq * _gelu_tanh(yb)

    o = jnp.concatenate(oa + [ob], axis=-1).astype(BF16)
    y_ref[...] = x + _dot(o, wo_ref[...])


def _mixer_sample(x, mn, win, lb, hn, cw, cb, wa, ba, wx, bx, lam, wo, s0, h0, buf):
    nb, T, _ = x.shape
    assert T == CONV_W and 8 % T == 0, "sample kernel assumes DEC_SEQ == CONV_W == 4"
    G = SAMPLE_GROUP
    R = G * T
    assert nb % G == 0 and R % 8 == 0
    xr = x.reshape(nb * T, D_MODEL)
    h0x = jnp.repeat(h0, T, axis=0)
    hists = [jnp.concatenate([buf[:, CONV_W - 1 - d:, :], jnp.zeros((nb, T - d, D_B), F32)], axis=1)
             .reshape(nb * T, D_B) for d in range(1, CONV_W)]
    rows = lambda n: pl.BlockSpec((R, n), lambda i: (i, 0))
    state = pl.BlockSpec((G, N_HEADS, D_HEAD, D_HEAD), lambda i: (i, 0, 0, 0))
    vec = lambda n: _resident((1, n))
    out_shape = (jax.ShapeDtypeStruct((nb * T, D_MODEL), F32),
                 jax.ShapeDtypeStruct((nb, N_HEADS, D_HEAD, D_HEAD), F32),
                 jax.ShapeDtypeStruct((nb * T, D_B), F32),
                 jax.ShapeDtypeStruct((nb * T, D_B), F32))
    scratch = [pltpu.VMEM((R, D_IN), F32),
               pltpu.VMEM((R + 8, D_B), F32),
               pltpu.VMEM((R + 8, D_A), F32),
               pltpu.VMEM((R + 8, D_A), F32),
               pltpu.VMEM((R + 8, D_A), F32),
               pltpu.VMEM((R, D_A), F32),
               pltpu.VMEM((R, D_A), F32)]
    y, s1, hseq, xb = pl.pallas_call(
        functools.partial(_mixer_sample_kernel, steps=T),
        out_shape=out_shape,
        grid=(nb // G,),
        in_specs=[rows(D_MODEL), vec(D_MODEL), _resident((D_MODEL, D_IN)), _resident((2, D_A)), vec(D_A),
                  _resident((CONV_W, D_B)), vec(D_B), _resident((D_B, D_B)), vec(D_B), _resident((D_B, D_B)),
                  vec(D_B), vec(D_B), _resident((D_A + D_B, D_MODEL)),
                  state, rows(D_B), rows(D_B), rows(D_B), rows(D_B)],
        out_specs=(rows(D_MODEL), state, rows(D_B), rows(D_B)),
        scratch_shapes=scratch,
        compiler_params=pltpu.CompilerParams(dimension_semantics=("arbitrary",),
                                             vmem_limit_bytes=VMEM_LIMIT_BYTES),
        name="mixer_sample",
    )(xr, mn, win, lb, hn, cw, cb, wa, ba, wx, bx, lam, wo, s0, h0x, *hists)
    hseq = hseq.reshape(nb, T, D_B)
    xb = xb.reshape(nb, T, D_B)
    return y, s1, hseq[:, T - 1], xb[:, T - (CONV_W - 1):]


def _block_diag(w):
    n, bi, bj = w.shape
    eye = jnp.eye(n, dtype=w.dtype)
    return (w[:, :, None, :] * eye[:, None, :, None]).reshape(n * bi, n * bj)


def kernel(x_prompt, x_sample, state_hgrn, state_lru, state_conv, ffn1_norm, ffn1_wg, ffn1_wu, ffn1_wd,
           mix_norm, w_in, hgrn_lb, hgrn_norm, conv_w, conv_b, lru_wa, lru_ba, lru_wx, lru_bx, lru_lambda,
           w_o, ffn2_norm, ffn2_wg, ffn2_wu, ffn2_wd, final_norm):
    assert ffn1_norm.shape[0] == 1 and hgrn_lb.shape[0] == 2, "single-layer model"
    bp, tp, _ = x_prompt.shape
    bs, ts, _ = x_sample.shape

    ffn1 = (ffn1_norm, ffn1_wg[0].astype(BF16), ffn1_wu[0].astype(BF16), ffn1_wd[0].astype(BF16))
    ffn2 = (ffn2_norm, ffn2_wg[0].astype(BF16), ffn2_wu[0].astype(BF16), ffn2_wd[0].astype(BF16))
    fn = final_norm.reshape(1, D_MODEL)
    mix = (mix_norm, w_in[0].astype(BF16), hgrn_lb, hgrn_norm, conv_w[0], conv_b,
           _block_diag(lru_wa[0]).astype(BF16), lru_ba[0].reshape(1, D_B),
           _block_diag(lru_wx[0]).astype(BF16), lru_bx[0].reshape(1, D_B), lru_lambda, w_o[0].astype(BF16))

    xp = _ffn(x_prompt.reshape(bp * tp, D_MODEL), *ffn1, fn, apply_final_norm=False, name="ffn1_prompt")
    xp, hg_p, lru_p, cv_p = _mixer_prompt(xp.reshape(bp, tp, D_MODEL), *mix)
    yp = _ffn(xp.reshape(bp * tp, D_MODEL), *ffn2, fn, apply_final_norm=True, name="ffn2_prompt")

    xs = _ffn(x_sample.reshape(bs * ts, D_MODEL), *ffn1, fn, apply_final_norm=False, name="ffn1_sample")
    xs, hg_s, lru_s, cv_s = _mixer_sample(xs.reshape(bs, ts, D_MODEL), *mix, state_hgrn[0], state_lru[0],
                                          state_conv[0])
    ys = _ffn(xs, *ffn2, fn, apply_final_norm=True, name="ffn2_sample")

    return (yp.reshape(bp, tp, D_MODEL), ys.reshape(bs, ts, D_MODEL),
            hg_p[None], lru_p.reshape(1, bp, D_B), cv_p[None],
            hg_s[None], lru_s[None], cv_s[None])
```

```python
import functools
import math

import jax
import jax.numpy as jnp
from jax import lax
from jax.experimental import pallas as pl
from jax.experimental.pallas import tpu as pltpu

F32 = jnp.float32
BF16 = jnp.bfloat16

D_MODEL = 1024
D_FF = 2816
D_A = 512
D_B = 512
N_HEADS = 4
D_HEAD = 128
D_IN = 4 * D_A + 2 * D_B
CONV_W = 4
LRU_C = 8.0
EPS = 1e-6

SUBLANES = 8
NEG_LARGE = -1e30
FFN_TILE = 512
FFN_CHUNK = 256
MIX_CHUNK = 128
SAMPLE_GROUP = 16
VMEM_LIMIT_BYTES = 56 * 1024 * 1024

_NT = (((1,), (1,)), ((), ()))


def _dot(a, b):
    return jnp.dot(a, b, preferred_element_type=F32)


def _dot_nt(a, b):
    return lax.dot_general(a, b, _NT, preferred_element_type=F32)


def _rms(x, w):
    ms = jnp.mean(x * x, axis=-1, keepdims=True)
    return x * lax.rsqrt(ms + EPS) * w


def _sigmoid(x):
    return 1.0 / (1.0 + jnp.exp(-x))


def _log1p(x):
    u = 1.0 + x
    return jnp.where(u == 1.0, x, jnp.log(u) * x / (u - 1.0))


def _expm1(x):
    u = jnp.exp(x)
    return jnp.where(u == 1.0, x, (u - 1.0) * x / jnp.log(u))


def _softplus(x):
    return jnp.maximum(x, 0.0) + _log1p(jnp.exp(-jnp.abs(x)))


def _gelu_tanh(x):
    return 0.5 * x * (1.0 + jnp.tanh(math.sqrt(2.0 / math.pi) * (x + 0.044715 * (x * x * x))))


def _split3_bf16(x):
    x1 = x.astype(BF16)
    r1 = x - x1.astype(F32)
    x2 = r1.astype(BF16)
    r2 = r1 - x2.astype(F32)
    return x1, x2, r2.astype(BF16)


def _ffn_kernel(x_ref, nw_ref, wg_ref, wu_ref, wd_ref, fn_ref, o_ref, *, apply_final_norm):
    x = x_ref[...]
    xn = _rms(x, nw_ref[...]).astype(BF16)
    acc = jnp.zeros(x.shape, F32)
    for c in range(D_FF // FFN_CHUNK):
        sl = slice(c * FFN_CHUNK, (c + 1) * FFN_CHUNK)
        g = _dot(xn, wg_ref[:, sl])
        u = _dot(xn, wu_ref[:, sl])
        h = (g * _sigmoid(g) * u).astype(BF16)
        acc = acc + _dot(h, wd_ref[sl, :])
    y = x + 0.5 * acc
    if apply_final_norm:
        y = _rms(y, fn_ref[...])
    o_ref[...] = y


def _resident(shape):
    zeros = (0,) * len(shape)
    return pl.BlockSpec(shape, lambda *_: zeros, pipeline_mode=pl.Buffered(1))


def _ffn(x, norm_w, wg, wu, wd, final_w, *, apply_final_norm, name):
    n = x.shape[0]
    tm = min(FFN_TILE, n)
    assert n % tm == 0 and D_FF % FFN_CHUNK == 0
    row = pl.BlockSpec((tm, D_MODEL), lambda i: (i, 0))
    return pl.pallas_call(
        functools.partial(_ffn_kernel, apply_final_norm=apply_final_norm),
        out_shape=jax.ShapeDtypeStruct((n, D_MODEL), F32),
        grid=(n // tm,),
        in_specs=[row, _resident((1, D_MODEL)), _resident((D_MODEL, D_FF)), _resident((D_MODEL, D_FF)),
                  _resident((D_FF, D_MODEL)), _resident((1, D_MODEL))],
        out_specs=row,
        compiler_params=pltpu.CompilerParams(dimension_semantics=("arbitrary",),
                                             vmem_limit_bytes=VMEM_LIMIT_BYTES),
        name=name,
    )(x, norm_w, wg, wu, wd, final_w)


def _hgrn_gates(proj_ref, lb_ref):
    l0 = lb_ref[0:1, :]
    l1 = lb_ref[1:2, :]
    m = jnp.maximum(l0, l1)
    e0 = jnp.exp(l0 - m)
    e1 = jnp.exp(l1 - m)
    lb = e0 / (e0 + e1)
    q = proj_ref[:, 0:D_A]
    fz = proj_ref[:, D_A:2 * D_A]
    f = lb + (1.0 - lb) * _sigmoid(fz)
    return q * _sigmoid(q), 1.0 - f, jnp.log(f)


def _pairwise(qs, k, cum, v, kpad_ref, cpad_ref, vpad_ref, block, n_rows):
    pad = kpad_ref.shape[0] - n_rows
    kpad_ref[pl.ds(pad, n_rows), :] = k
    cpad_ref[pl.ds(pad, n_rows), :] = cum
    vpad_ref[pl.ds(pad, n_rows), :] = v
    rmod = lax.broadcasted_iota(jnp.int32, (n_rows, 1), 0) & (block - 1)
    out = [jnp.zeros((n_rows, D_HEAD), F32) for _ in range(N_HEADS)]
    for d in range(block):
        if d == 0:
            p = qs * k
            vs = v
        else:
            ks = kpad_ref[pl.ds(pad - d, n_rows), :]
            cs = cpad_ref[pl.ds(pad - d, n_rows), :]
            vs = vpad_ref[pl.ds(pad - d, n_rows), :]
            p = qs * ks * jnp.exp(cum - cs)
        valid = rmod >= d
        for h in range(N_HEADS):
            hs = slice(h * D_HEAD, (h + 1) * D_HEAD)
            s = jnp.sum(p[:, hs], axis=-1, keepdims=True)
            s = jnp.where(valid, s, 0.0)
            out[h] = out[h] + s * vs[:, hs]
    return out


def _head_norm_gate(o_heads, hn_ref, gate):
    outs = []
    for h in range(N_HEADS):
        hs = slice(h * D_HEAD, (h + 1) * D_HEAD)
        o = o_heads[h]
        ms = jnp.mean(o * o, axis=-1, keepdims=True)
        outs.append(o * lax.rsqrt(ms + EPS) * hn_ref[:, hs] * _sigmoid(gate[:, hs]))
    return outs


def _lru_coeffs(xc, wa_ref, ba_ref, wx_ref, bx_ref, lam_ref):
    xcb = xc.astype(BF16)
    r = _sigmoid(_dot(xcb, wa_ref[...]) + ba_ref[...])
    i = _sigmoid(_dot(xcb, wx_ref[...]) + bx_ref[...])
    log_a = -LRU_C * r * _softplus(-lam_ref[...])
    a = jnp.exp(log_a)
    mult = jnp.sqrt(-_expm1(2.0 * log_a))
    return a, mult, i


def _scan_rows(a, u, rowpos, span):
    s = 1
    while s < span:
        keep = rowpos >= s
        a_sh = jnp.where(keep, pltpu.roll(a, s, 0), 1.0)
        u_sh = jnp.where(keep, pltpu.roll(u, s, 0), 0.0)
        u = a * u_sh + u
        a = a * a_sh
        s *= 2
    return a, u


def _row_bcast(ref, h, r, n):
    return jnp.broadcast_to(ref[h, pl.ds(r, 1), :], (n, D_HEAD))


def _pairwise_tile(qs_ref, k_ref, cum_ref, v_ref, out_ref, h, n_rows):
    pos = lax.broadcasted_iota(jnp.int32, (SUBLANES, 1), 0)
    for j in range(n_rows // SUBLANES):
        rows = pl.ds(SUBLANES * j, SUBLANES)
        q_t = qs_ref[h, rows, :]
        c_t = cum_ref[h, rows, :]
        acc = jnp.zeros((SUBLANES, D_HEAD), F32)
        for s in range(SUBLANES):
            src = SUBLANES * j + s
            d = c_t - _row_bcast(cum_ref, h, src, SUBLANES)
            if s > 0:
                d = jnp.where(pos >= s, d, NEG_LARGE)
            p = q_t * _row_bcast(k_ref, h, src, SUBLANES) * jnp.exp(d)
            acc = acc + jnp.sum(p, axis=-1, keepdims=True) * _row_bcast(v_ref, h, src, SUBLANES)
        out_ref[h, rows, :] = acc


def _level_operands(qs_ref, k_ref, cum_ref, h, n_rows, L):
    half = L // 2
    zeros = jnp.zeros((half, D_HEAD), F32)
    qparts, kparts = [], []
    for b in range(n_rows // L):
        r0 = b * L
        first = pl.ds(r0, half)
        second = pl.ds(r0 + half, half)
        mid = _row_bcast(cum_ref, h, r0 + half - 1, half)
        kparts += [k_ref[h, first, :] * jnp.exp(mid - cum_ref[h, first, :]), zeros]
        qparts += [zeros, qs_ref[h, second, :] * jnp.exp(cum_ref[h, second, :] - mid)]
    return jnp.concatenate(qparts, axis=0).astype(BF16), jnp.concatenate(kparts, axis=0).astype(BF16)


def _scan_tiles(a, u, carry):
    n = a.shape[0]
    a3 = a.reshape(n // SUBLANES, SUBLANES, D_B)
    u3 = u.reshape(n // SUBLANES, SUBLANES, D_B)
    pos = lax.broadcasted_iota(jnp.int32, (1, SUBLANES, 1), 1)
    s = 1
    while s < SUBLANES:
        keep = pos >= s
        a_sh = jnp.where(keep, pltpu.roll(a3, s, 1), 1.0)
        u_sh = jnp.where(keep, pltpu.roll(u3, s, 1), 0.0)
        u3 = a3 * u_sh + u3
        a3 = a3 * a_sh
        s *= 2
    tiles = []
    for j in range(n // SUBLANES):
        hj = u3[j] + a3[j] * carry
        tiles.append(hj)
        carry = hj[SUBLANES - 1:SUBLANES, :]
    return jnp.concatenate(tiles, axis=0)


def _mixer_prompt_kernel(x_ref, mn_ref, win_ref, lb_ref, hn_ref, cw_ref, cb_ref, wa_ref, ba_ref, wx_ref,
                         bx_ref, lam_ref, wo_ref,
                         y_ref, shg_ref, slru_ref, scv_ref,
                         proj_ref, st_ref, h_ref, xb_ref, qs_ref, k_ref, cum_ref, v_ref, opw_ref, obuf_ref,
                         *, chunk):
    C = chunk
    t = pl.program_id(1)
    last = pl.num_programs(1) - 1

    @pl.when(t == 0)
    def _():
        st_ref[...] = jnp.zeros(st_ref.shape, F32)
        h_ref[...] = jnp.zeros(h_ref.shape, F32)
        xb_ref[0:8, :] = jnp.zeros((8, D_B), F32)

    x = x_ref[0]
    xn = _rms(x, mn_ref[...]).astype(BF16)
    proj_ref[...] = _dot(xn, win_ref[...])

    qs, k, g = _hgrn_gates(proj_ref, lb_ref)
    row = lax.broadcasted_iota(jnp.int32, (C, C), 0)
    col = lax.broadcasted_iota(jnp.int32, (C, C), 1)
    tri = jnp.where(col <= row, 1.0, 0.0).astype(BF16)
    g1, g2, g3 = _split3_bf16(g)
    cum = _dot(tri, g1) + _dot(tri, g2) + _dot(tri, g3)
    for h in range(N_HEADS):
        hs = slice(h * D_HEAD, (h + 1) * D_HEAD)
        qs_ref[h] = qs[:, hs]
        k_ref[h] = k[:, hs]
        cum_ref[h] = cum[:, hs]
        v_ref[h] = proj_ref[:, 2 * D_A + h * D_HEAD:2 * D_A + (h + 1) * D_HEAD]

    levels = []
    L = 2 * SUBLANES
    while L <= C:
        shift = L.bit_length() - 1
        levels.append((L, None if L == C else (row >> shift) == (col >> shift)))
        L *= 2

    gate = proj_ref[:, 3 * D_A:4 * D_A]
    for h in range(N_HEADS):
        hs = slice(h * D_HEAD, (h + 1) * D_HEAD)
        _pairwise_tile(qs_ref, k_ref, cum_ref, v_ref, opw_ref, h, C)
        scores = jnp.zeros((C, C), F32)
        for L, same in levels:
            qt, kt = _level_operands(qs_ref, k_ref, cum_ref, h, C, L)
            s = _dot_nt(qt, kt)
            scores = scores + (s if same is None else jnp.where(same, s, 0.0))
        cum_h = cum_ref[h]
        cl = _row_bcast(cum_ref, h, C - 1, C)
        qbar = (qs_ref[h] * jnp.exp(cum_h)).astype(BF16)
        khat = (k_ref[h] * jnp.exp(cl - cum_h)).astype(BF16)
        dec = jnp.exp(cum_ref[h, pl.ds(C - 1, 1), :])
        v_h = v_ref[h]
        st = st_ref[h]
        o = _dot(scores.astype(BF16), v_h.astype(BF16)) + _dot_nt(qbar, st.astype(BF16)) + opw_ref[h]
        st_ref[h] = st * dec + _dot(v_h.T.astype(BF16), khat)
        ms = jnp.mean(o * o, axis=-1, keepdims=True)
        obuf_ref[:, hs] = (o * lax.rsqrt(ms + EPS) * hn_ref[:, hs] * _sigmoid(gate[:, hs])).astype(BF16)

    rowi = lax.broadcasted_iota(jnp.int32, (C, 1), 0)
    xb = proj_ref[:, 4 * D_A:4 * D_A + D_B]
    yb = proj_ref[:, 4 * D_A + D_B:D_IN]
    xb_ref[pl.ds(8, C), :] = xb
    xc = (cb_ref[...] + cw_ref[3:4, :] * xb + cw_ref[2:3, :] * xb_ref[pl.ds(7, C), :]
          + cw_ref[1:2, :] * xb_ref[pl.ds(6, C), :] + cw_ref[0:1, :] * xb_ref[pl.ds(5, C), :])
    xb_ref[0:8, :] = xb[C - 8:C, :]
    a, mult, gate_i = _lru_coeffs(xc, wa_ref, ba_ref, wx_ref, bx_ref, lam_ref)
    mult = jnp.where((rowi == 0) & (t == 0), 1.0, mult)
    hseq = _scan_tiles(a, mult * gate_i * xc, h_ref[...])
    h_ref[...] = hseq[C - 1:C, :]
    obuf_ref[:, D_A:D_A + D_B] = (hseq * _gelu_tanh(yb)).astype(BF16)

    y_ref[0] = x + _dot(obuf_ref[...], wo_ref[...])

    @pl.when(t == last)
    def _():
        for h in range(N_HEADS):
            shg_ref[0, h] = st_ref[h].T
        slru_ref[0] = hseq[C - 1:C, :]
        scv_ref[0] = xb[C - (CONV_W - 1):C, :]


def _mixer_prompt(x, mn, win, lb, hn, cw, cb, wa, ba, wx, bx, lam, wo):
    b, t, _ = x.shape
    C = MIX_CHUNK
    assert t % C == 0 and C % (2 * SUBLANES) == 0 and C & (C - 1) == 0
    seq = pl.BlockSpec((1, C, D_MODEL), lambda i, j: (i, j, 0))
    vec = lambda n: _resident((1, n))
    out_shape = (jax.ShapeDtypeStruct((b, t, D_MODEL), F32),
                 jax.ShapeDtypeStruct((b, N_HEADS, D_HEAD, D_HEAD), F32),
                 jax.ShapeDtypeStruct((b, 1, D_B), F32),
                 jax.ShapeDtypeStruct((b, CONV_W - 1, D_B), F32))
    out_specs = (seq,
                 pl.BlockSpec((1, N_HEADS, D_HEAD, D_HEAD), lambda i, j: (i, 0, 0, 0)),
                 pl.BlockSpec((1, 1, D_B), lambda i, j: (i, 0, 0)),
                 pl.BlockSpec((1, CONV_W - 1, D_B), lambda i, j: (i, 0, 0)))
    per_head = pltpu.VMEM((N_HEADS, C, D_HEAD), F32)
    scratch = [pltpu.VMEM((C, D_IN), F32),
               pltpu.VMEM((N_HEADS, D_HEAD, D_HEAD), F32),
               pltpu.VMEM((1, D_B), F32),
               pltpu.VMEM((C + 8, D_B), F32),
               per_head, per_head, per_head, per_head,
               per_head,
               pltpu.VMEM((C, D_A + D_B), BF16)]
    return pl.pallas_call(
        functools.partial(_mixer_prompt_kernel, chunk=C),
        out_shape=out_shape,
        grid=(b, t // C),
        in_specs=[seq, vec(D_MODEL), _resident((D_MODEL, D_IN)), _resident((2, D_A)), vec(D_A),
                  _resident((CONV_W, D_B)), vec(D_B), _resident((D_B, D_B)), vec(D_B), _resident((D_B, D_B)),
                  vec(D_B), vec(D_B), _resident((D_A + D_B, D_MODEL))],
        out_specs=out_specs,
        scratch_shapes=scratch,
        compiler_params=pltpu.CompilerParams(dimension_semantics=("arbitrary", "arbitrary"),
                                             vmem_limit_bytes=VMEM_LIMIT_BYTES),
        name="mixer_prompt",
    )(x, mn, win, lb, hn, cw, cb, wa, ba, wx, bx, lam, wo)


def _mixer_sample_kernel(x_ref, mn_ref, win_ref, lb_ref, hn_ref, cw_ref, cb_ref, wa_ref, ba_ref, wx_ref,
                         bx_ref, lam_ref, wo_ref, s0_ref, h0_ref, hist1_ref, hist2_ref, hist3_ref,
                         y_ref, s1_ref, hseq_ref, xb_out_ref,
                         proj_ref, xb_ref, kpad_ref, cpad_ref, vpad_ref, qbar_ref, ost_ref, *, steps):
    T = steps
    R = x_ref.shape[0]
    pad = kpad_ref.shape[0] - R

    zpad = jnp.zeros((pad, D_A), F32)
    kpad_ref[0:pad, :] = zpad
    cpad_ref[0:pad, :] = zpad
    vpad_ref[0:pad, :] = zpad
    xb_ref[0:8, :] = jnp.zeros((8, D_B), F32)

    x = x_ref[...]
    xn = _rms(x, mn_ref[...]).astype(BF16)
    proj_ref[...] = _dot(xn, win_ref[...])

    qs, k, g = _hgrn_gates(proj_ref, lb_ref)
    v = proj_ref[:, 2 * D_A:3 * D_A]
    row = lax.broadcasted_iota(jnp.int32, (R, R), 0)
    col = lax.broadcasted_iota(jnp.int32, (R, R), 1)
    tshift = T.bit_length() - 1
    tri = jnp.where((col <= row) & ((row >> tshift) == (col >> tshift)), 1.0, 0.0).astype(BF16)
    g1, g2, g3 = _split3_bf16(g)
    cum = _dot(tri, g1) + _dot(tri, g2) + _dot(tri, g3)
    o_pw = _pairwise(qs, k, cum, v, kpad_ref, cpad_ref, vpad_ref, T, R)

    ecum = jnp.exp(cum)
    qbar_ref[...] = qs * ecum
    upper = jnp.where((col > row) & ((row >> tshift) == (col >> tshift)), 1.0, 0.0).astype(BF16)
    tail = _dot(upper, g1) + _dot(upper, g2) + _dot(upper, g3)
    khat = k * jnp.exp(tail)
    row8 = lax.broadcasted_iota(jnp.int32, (8, 1), 0)
    for h in range(N_HEADS):
        hs = slice(h * D_HEAD, (h + 1) * D_HEAD)
        khat_t = khat[:, hs].T
        dec_t = ecum[:, hs].T
        for p in range(R // 8):
            q8 = qbar_ref[pl.ds(8 * p, 8), hs].astype(BF16)
            o8 = jnp.zeros((8, D_HEAD), F32)
            for i in range(8 // T):
                b = p * (8 // T) + i
                s0 = s0_ref[b, h]
                ob = _dot(q8, s0.astype(BF16))
                o8 = jnp.where((row8 >= i * T) & (row8 < (i + 1) * T), ob, o8)
                acc = s0 * dec_t[:, b * T + T - 1:b * T + T]
                for tt in range(T):
                    j = b * T + tt
                    acc = acc + khat_t[:, j:j + 1] * v[j:j + 1, hs]
                s1_ref[b, h] = acc
            ost_ref[pl.ds(8 * p, 8), hs] = o8
    o_heads = [o_pw[h] + ost_ref[:, h * D_HEAD:(h + 1) * D_HEAD] for h in range(N_HEADS)]
    oa = _head_norm_gate(o_heads, hn_ref, proj_ref[:, 3 * D_A:4 * D_A])

    rowi = lax.broadcasted_iota(jnp.int32, (R, 1), 0)
    tpos = rowi & (T - 1)
    xb = proj_ref[:, 4 * D_A:4 * D_A + D_B]
    yb = proj_ref[:, 4 * D_A + D_B:D_IN]
    xb_ref[pl.ds(8, R), :] = xb
    xb_out_ref[...] = xb
    hist = (hist1_ref, hist2_ref, hist3_ref)
    xc = cb_ref[...] + cw_ref[CONV_W - 1:CONV_W, :] * xb
    for d in range(1, CONV_W):
        prev = jnp.where(tpos >= d, xb_ref[pl.ds(8 - d, R), :], hist[d - 1][...])
        xc = xc + cw_ref[CONV_W - 1 - d:CONV_W - d, :] * prev
    a, mult, gate_i = _lru_coeffs(xc, wa_ref, ba_ref, wx_ref, bx_ref, lam_ref)
    a_run, h_run = _scan_rows(a, mult * gate_i * xc, tpos, T)
    hseq = h_run + a_run * h0_ref[...]
    hseq_ref[...] = hseq
    ob = hseq * _gelu_tanh(yb)

    o = jnp.concatenate(oa + [ob], axis=-1).astype(BF16)
    y_ref[...] = x + _dot(o, wo_ref[...])


def _mixer_sample(x, mn, win, lb, hn, cw, cb, wa, ba, wx, bx, lam, wo, s0, h0, buf):
    nb, T, _ = x.shape
    assert T == CONV_W and 8 % T == 0, "sample kernel assumes DEC_SEQ == CONV_W == 4"
    G = SAMPLE_GROUP
    R = G * T
    assert nb % G == 0 and R % 8 == 0
    xr = x.reshape(nb * T, D_MODEL)
    h0x = jnp.repeat(h0, T, axis=0)
    hists = [jnp.concatenate([buf[:, CONV_W - 1 - d:, :], jnp.zeros((nb, T - d, D_B), F32)], axis=1)
             .reshape(nb * T, D_B) for d in range(1, CONV_W)]
    rows = lambda n: pl.BlockSpec((R, n), lambda i: (i, 0))
    state = pl.BlockSpec((G, N_HEADS, D_HEAD, D_HEAD), lambda i: (i, 0, 0, 0))
    vec = lambda n: _resident((1, n))
    out_shape = (jax.ShapeDtypeStruct((nb * T, D_MODEL), F32),
                 jax.ShapeDtypeStruct((nb, N_HEADS, D_HEAD, D_HEAD), F32),
                 jax.ShapeDtypeStruct((nb * T, D_B), F32),
                 jax.ShapeDtypeStruct((nb * T, D_B), F32))
    scratch = [pltpu.VMEM((R, D_IN), F32),
               pltpu.VMEM((R + 8, D_B), F32),
               pltpu.VMEM((R + 8, D_A), F32),
               pltpu.VMEM((R + 8, D_A), F32),
               pltpu.VMEM((R + 8, D_A), F32),
               pltpu.VMEM((R, D_A), F32),
               pltpu.VMEM((R, D_A), F32)]
    y, s1, hseq, xb = pl.pallas_call(
        functools.partial(_mixer_sample_kernel, steps=T),
        out_shape=out_shape,
        grid=(nb // G,),
        in_specs=[rows(D_MODEL), vec(D_MODEL), _resident((D_MODEL, D_IN)), _resident((2, D_A)), vec(D_A),
                  _resident((CONV_W, D_B)), vec(D_B), _resident((D_B, D_B)), vec(D_B), _resident((D_B, D_B)),
                  vec(D_B), vec(D_B), _resident((D_A + D_B, D_MODEL)),
                  state, rows(D_B), rows(D_B), rows(D_B), rows(D_B)],
        out_specs=(rows(D_MODEL), state, rows(D_B), rows(D_B)),
        scratch_shapes=scratch,
        compiler_params=pltpu.CompilerParams(dimension_semantics=("arbitrary",),
                                             vmem_limit_bytes=VMEM_LIMIT_BYTES),
        name="mixer_sample",
    )(xr, mn, win, lb, hn, cw, cb, wa, ba, wx, bx, lam, wo, s0, h0x, *hists)
    hseq = hseq.reshape(nb, T, D_B)
    xb = xb.reshape(nb, T, D_B)
    return y, s1, hseq[:, T - 1], xb[:, T - (CONV_W - 1):]


def _block_diag(w):
    n, bi, bj = w.shape
    eye = jnp.eye(n, dtype=w.dtype)
    return (w[:, :, None, :] * eye[:, None, :, None]).reshape(n * bi, n * bj)


def kernel(x_prompt, x_sample, state_hgrn, state_lru, state_conv, ffn1_norm, ffn1_wg, ffn1_wu, ffn1_wd,
           mix_norm, w_in, hgrn_lb, hgrn_norm, conv_w, conv_b, lru_wa, lru_ba, lru_wx, lru_bx, lru_lambda,
           w_o, ffn2_norm, ffn2_wg, ffn2_wu, ffn2_wd, final_norm):
    assert ffn1_norm.shape[0] == 1 and hgrn_lb.shape[0] == 2, "single-layer model"
    bp, tp, _ = x_prompt.shape
    bs, ts, _ = x_sample.shape

    ffn1 = (ffn1_norm, ffn1_wg[0].astype(BF16), ffn1_wu[0].astype(BF16), ffn1_wd[0].astype(BF16))
    ffn2 = (ffn2_norm, ffn2_wg[0].astype(BF16), ffn2_wu[0].astype(BF16), ffn2_wd[0].astype(BF16))
    fn = final_norm.reshape(1, D_MODEL)
    mix = (mix_norm, w_in[0].astype(BF16), hgrn_lb, hgrn_norm, conv_w[0], conv_b,
           _block_diag(lru_wa[0]).astype(BF16), lru_ba[0].reshape(1, D_B),
           _block_diag(lru_wx[0]).astype(BF16), lru_bx[0].reshape(1, D_B), lru_lambda, w_o[0].astype(BF16))

    xp = _ffn(x_prompt.reshape(bp * tp, D_MODEL), *ffn1, fn, apply_final_norm=False, name="ffn1_prompt")
    xp, hg_p, lru_p, cv_p = _mixer_prompt(xp.reshape(bp, tp, D_MODEL), *mix)
    yp = _ffn(xp.reshape(bp * tp, D_MODEL), *ffn2, fn, apply_final_norm=True, name="ffn2_prompt")

    xs = _ffn(x_sample.reshape(bs * ts, D_MODEL), *ffn1, fn, apply_final_norm=False, name="ffn1_sample")
    xs, hg_s, lru_s, cv_s = _mixer_sample(xs.reshape(bs, ts, D_MODEL), *mix, state_hgrn[0], state_lru[0],
                                          state_conv[0])
    ys = _ffn(xs, *ffn2, fn, apply_final_norm=True, name="ffn2_sample")

    return (yp.reshape(bp, tp, D_MODEL), ys.reshape(bs, ts, D_MODEL),
            hg_p[None], lru_p.reshape(1, bp, D_B), cv_p[None],
            hg_s[None], lru_s[None], cv_s[None])
```

```python
import functools
import math

import jax
import jax.numpy as jnp
from jax import lax
from jax.experimental import pallas as pl
from jax.experimental.pallas import tpu as pltpu

F32 = jnp.float32
BF16 = jnp.bfloat16

D_MODEL = 1024
D_FF = 2816
D_A = 512
D_B = 512
N_HEADS = 4
D_HEAD = 128
D_IN = 4 * D_A + 2 * D_B
CONV_W = 4
LRU_C = 8.0
EPS = 1e-6

SUBLANES = 8
NEG_LARGE = -1e30
FFN_TILE = 512
FFN_CHUNK = 256
MIX_CHUNK = 256
LAYER_FFN_BOUNDS = (0, 768, 1536, 2304, 2816)
SAMPLE_GROUP = 16
VMEM_LIMIT_BYTES = 56 * 1024 * 1024
LAYER_VMEM_LIMIT_BYTES = 62 * 1024 * 1024

_NT = (((1,), (1,)), ((), ()))


def _dot(a, b):
    return jnp.dot(a, b, preferred_element_type=F32)


def _dot_nt(a, b):
    return lax.dot_general(a, b, _NT, preferred_element_type=F32)


def _rms(x, w):
    ms = jnp.mean(x * x, axis=-1, keepdims=True)
    return x * lax.rsqrt(ms + EPS) * w


def _sigmoid(x):
    return 1.0 / (1.0 + jnp.exp(-x))


def _log1p(x):
    u = 1.0 + x
    return jnp.where(u == 1.0, x, jnp.log(u) * x / (u - 1.0))


def _expm1(x):
    u = jnp.exp(x)
    return jnp.where(u == 1.0, x, (u - 1.0) * x / jnp.log(u))


def _softplus(x):
    return jnp.maximum(x, 0.0) + _log1p(jnp.exp(-jnp.abs(x)))


def _gelu_tanh(x):
    return 0.5 * x * (1.0 + jnp.tanh(math.sqrt(2.0 / math.pi) * (x + 0.044715 * (x * x * x))))


def _split3_bf16(x):
    x1 = x.astype(BF16)
    r1 = x - x1.astype(F32)
    x2 = r1.astype(BF16)
    r2 = r1 - x2.astype(F32)
    return x1, x2, r2.astype(BF16)


def _ffn_kernel(x_ref, nw_ref, wg_ref, wu_ref, wd_ref, fn_ref, o_ref, *, apply_final_norm):
    x = x_ref[...]
    xn = _rms(x, nw_ref[...]).astype(BF16)
    acc = jnp.zeros(x.shape, F32)
    for c in range(D_FF // FFN_CHUNK):
        sl = slice(c * FFN_CHUNK, (c + 1) * FFN_CHUNK)
        g = _dot(xn, wg_ref[:, sl])
        u = _dot(xn, wu_ref[:, sl])
        h = (g * _sigmoid(g) * u).astype(BF16)
        acc = acc + _dot(h, wd_ref[sl, :])
    y = x + 0.5 * acc
    if apply_final_norm:
        y = _rms(y, fn_ref[...])
    o_ref[...] = y


def _resident(shape):
    zeros = (0,) * len(shape)
    return pl.BlockSpec(shape, lambda *_: zeros, pipeline_mode=pl.Buffered(1))


def _ffn(x, norm_w, wg, wu, wd, final_w, *, apply_final_norm, name):
    n = x.shape[0]
    tm = min(FFN_TILE, n)
    assert n % tm == 0 and D_FF % FFN_CHUNK == 0
    row = pl.BlockSpec((tm, D_MODEL), lambda i: (i, 0))
    return pl.pallas_call(
        functools.partial(_ffn_kernel, apply_final_norm=apply_final_norm),
        out_shape=jax.ShapeDtypeStruct((n, D_MODEL), F32),
        grid=(n // tm,),
        in_specs=[row, _resident((1, D_MODEL)), _resident((D_MODEL, D_FF)), _resident((D_MODEL, D_FF)),
                  _resident((D_FF, D_MODEL)), _resident((1, D_MODEL))],
        out_specs=row,
        compiler_params=pltpu.CompilerParams(dimension_semantics=("arbitrary",),
                                             vmem_limit_bytes=VMEM_LIMIT_BYTES),
        name=name,
    )(x, norm_w, wg, wu, wd, final_w)


def _hgrn_gates(proj_ref, lb_ref):
    l0 = lb_ref[0:1, :]
    l1 = lb_ref[1:2, :]
    m = jnp.maximum(l0, l1)
    e0 = jnp.exp(l0 - m)
    e1 = jnp.exp(l1 - m)
    lb = e0 / (e0 + e1)
    q = proj_ref[:, 0:D_A]
    fz = proj_ref[:, D_A:2 * D_A]
    f = lb + (1.0 - lb) * _sigmoid(fz)
    return q * _sigmoid(q), 1.0 - f, jnp.log(f)


def _pairwise(qs, k, cum, v, kpad_ref, cpad_ref, vpad_ref, block, n_rows):
    pad = kpad_ref.shape[0] - n_rows
    kpad_ref[pl.ds(pad, n_rows), :] = k
    cpad_ref[pl.ds(pad, n_rows), :] = cum
    vpad_ref[pl.ds(pad, n_rows), :] = v
    rmod = lax.broadcasted_iota(jnp.int32, (n_rows, 1), 0) & (block - 1)
    out = [jnp.zeros((n_rows, D_HEAD), F32) for _ in range(N_HEADS)]
    for d in range(block):
        if d == 0:
            p = qs * k
            vs = v
        else:
            ks = kpad_ref[pl.ds(pad - d, n_rows), :]
            cs = cpad_ref[pl.ds(pad - d, n_rows), :]
            vs = vpad_ref[pl.ds(pad - d, n_rows), :]
            p = qs * ks * jnp.exp(cum - cs)
        valid = rmod >= d
        for h in range(N_HEADS):
            hs = slice(h * D_HEAD, (h + 1) * D_HEAD)
            s = jnp.sum(p[:, hs], axis=-1, keepdims=True)
            s = jnp.where(valid, s, 0.0)
            out[h] = out[h] + s * vs[:, hs]
    return out


def _head_norm_gate(o_heads, hn_ref, gate):
    outs = []
    for h in range(N_HEADS):
        hs = slice(h * D_HEAD, (h + 1) * D_HEAD)
        o = o_heads[h]
        ms = jnp.mean(o * o, axis=-1, keepdims=True)
        outs.append(o * lax.rsqrt(ms + EPS) * hn_ref[:, hs] * _sigmoid(gate[:, hs]))
    return outs


def _lru_coeffs(xc, wa_ref, ba_ref, wx_ref, bx_ref, lam_ref):
    xcb = xc.astype(BF16)
    r = _sigmoid(_dot(xcb, wa_ref[...]) + ba_ref[...])
    i = _sigmoid(_dot(xcb, wx_ref[...]) + bx_ref[...])
    log_a = -LRU_C * r * _softplus(-lam_ref[...])
    a = jnp.exp(log_a)
    mult = jnp.sqrt(-_expm1(2.0 * log_a))
    return a, mult, i


def _scan_rows(a, u, rowpos, span):
    s = 1
    while s < span:
        keep = rowpos >= s
        a_sh = jnp.where(keep, pltpu.roll(a, s, 0), 1.0)
        u_sh = jnp.where(keep, pltpu.roll(u, s, 0), 0.0)
        u = a * u_sh + u
        a = a * a_sh
        s *= 2
    return a, u


def _row_bcast(ref, h, r, n):
    return jnp.broadcast_to(ref[h, pl.ds(r, 1), :], (n, D_HEAD))


def _pairwise_tile(qs_ref, k_ref, cum_ref, v_ref, out_ref, h, n_rows):
    pos = lax.broadcasted_iota(jnp.int32, (SUBLANES, 1), 0)
    for j in range(n_rows // SUBLANES):
        rows = pl.ds(SUBLANES * j, SUBLANES)
        q_t = qs_ref[h, rows, :]
        c_t = cum_ref[h, rows, :]
        acc = jnp.zeros((SUBLANES, D_HEAD), F32)
        for s in range(SUBLANES):
            src = SUBLANES * j + s
            d = c_t - _row_bcast(cum_ref, h, src, SUBLANES)
            if s > 0:
                d = jnp.where(pos >= s, d, NEG_LARGE)
            p = q_t * _row_bcast(k_ref, h, src, SUBLANES) * jnp.exp(d)
            acc = acc + jnp.sum(p, axis=-1, keepdims=True) * _row_bcast(v_ref, h, src, SUBLANES)
        out_ref[h, rows, :] = acc


def _level_operands(qs_ref, k_ref, cum_ref, h, n_rows, L):
    half = L // 2
    zeros = jnp.zeros((half, D_HEAD), F32)
    qparts, kparts = [], []
    for b in range(n_rows // L):
        r0 = b * L
        first = pl.ds(r0, half)
        second = pl.ds(r0 + half, half)
        mid = _row_bcast(cum_ref, h, r0 + half - 1, half)
        kparts += [k_ref[h, first, :] * jnp.exp(mid - cum_ref[h, first, :]), zeros]
        qparts += [zeros, qs_ref[h, second, :] * jnp.exp(cum_ref[h, second, :] - mid)]
    return jnp.concatenate(qparts, axis=0).astype(BF16), jnp.concatenate(kparts, axis=0).astype(BF16)


def _scan_tiles(a, u, carry):
    n = a.shape[0]
    a3 = a.reshape(n // SUBLANES, SUBLANES, D_B)
    u3 = u.reshape(n // SUBLANES, SUBLANES, D_B)
    pos = lax.broadcasted_iota(jnp.int32, (1, SUBLANES, 1), 1)
    s = 1
    while s < SUBLANES:
        keep = pos >= s
        a_sh = jnp.where(keep, pltpu.roll(a3, s, 1), 1.0)
        u_sh = jnp.where(keep, pltpu.roll(u3, s, 1), 0.0)
        u3 = a3 * u_sh + u3
        a3 = a3 * a_sh
        s *= 2
    tiles = []
    for j in range(n // SUBLANES):
        hj = u3[j] + a3[j] * carry
        tiles.append(hj)
        carry = hj[SUBLANES - 1:SUBLANES, :]
    return jnp.concatenate(tiles, axis=0)


class _SwigluChunks:
    def __init__(self, x, nw_ref, wg_ref, wu_ref, wd_ref):
        self.x = x
        self.xn = _rms(x, nw_ref[...]).astype(BF16)
        self.wg_ref, self.wu_ref, self.wd_ref = wg_ref, wu_ref, wd_ref
        self.hidden = {}
        self.acc = None

    @staticmethod
    def _cols(c):
        return slice(LAYER_FFN_BOUNDS[c], LAYER_FFN_BOUNDS[c + 1])

    def up(self, c):
        g = _dot(self.xn, self.wg_ref[:, self._cols(c)])
        u = _dot(self.xn, self.wu_ref[:, self._cols(c)])
        self.hidden[c] = (g * _sigmoid(g) * u).astype(BF16)

    def down(self, c):
        d = _dot(self.hidden.pop(c), self.wd_ref[self._cols(c), :])
        self.acc = d if self.acc is None else self.acc + d

    def result(self):
        assert not self.hidden
        return self.x + 0.5 * self.acc


def _layer_prompt_kernel(x_ref, n1_ref, wg1_ref, wu1_ref, wd1_ref,
                         mn_ref, win_ref, lb_ref, hn_ref, cw_ref, cb_ref, wa_ref, ba_ref, wx_ref,
                         bx_ref, lam_ref, wo_ref,
                         n2_ref, wg2_ref, wu2_ref, wd2_ref, fn_ref,
                         y_ref, shg_ref, slru_ref, scv_ref,
                         x1_ref, x2_ref, proj_ref, st_ref, h_ref, xb_ref, qs_ref, k_ref, cum_ref, v_ref,
                         opw_ref, obuf_ref, *, chunk, chunks_per_seq, n_chunks):
    C = chunk
    s = pl.program_id(0)

    @pl.when(s == 0)
    def _():
        x1_ref[...] = jnp.zeros(x1_ref.shape, F32)
        x2_ref[...] = jnp.zeros(x2_ref.shape, F32)
        st_ref[...] = jnp.zeros(st_ref.shape, F32)
        h_ref[...] = jnp.zeros(h_ref.shape, F32)
        xb_ref[0:8, :] = jnp.zeros((8, D_B), F32)

    ffn2 = _SwigluChunks(x2_ref[...], n2_ref, wg2_ref, wu2_ref, wd2_ref)
    ffn1 = _SwigluChunks(x_ref[...], n1_ref, wg1_ref, wu1_ref, wd1_ref)

    pos_in_seq = lax.rem(s + (chunks_per_seq - 1), chunks_per_seq)
    first = pos_in_seq == 0
    keep = jnp.where(first, 0.0, 1.0)
    x = x1_ref[...]
    xn = _rms(x, mn_ref[...]).astype(BF16)
    proj_ref[...] = _dot(xn, win_ref[...])
    ffn2.up(0)

    qs, k, g = _hgrn_gates(proj_ref, lb_ref)
    row = lax.broadcasted_iota(jnp.int32, (C, C), 0)
    col = lax.broadcasted_iota(jnp.int32, (C, C), 1)
    tri = jnp.where(col <= row, 1.0, 0.0).astype(BF16)
    g1, g2, g3 = _split3_bf16(g)
    cum = _dot(tri, g1) + _dot(tri, g2) + _dot(tri, g3)
    for h in range(N_HEADS):
        hs = slice(h * D_HEAD, (h + 1) * D_HEAD)
        qs_ref[h] = qs[:, hs]
        k_ref[h] = k[:, hs]
        cum_ref[h] = cum[:, hs]
        v_ref[h] = proj_ref[:, 2 * D_A + h * D_HEAD:2 * D_A + (h + 1) * D_HEAD]

    levels = []
    L = 2 * SUBLANES
    while L <= C:
        shift = L.bit_length() - 1
        levels.append((L, None if L == C else (row >> shift) == (col >> shift)))
        L *= 2

    gate = proj_ref[:, 3 * D_A:4 * D_A]
    for h in range(N_HEADS):
        hs = slice(h * D_HEAD, (h + 1) * D_HEAD)
        if h + 1 < len(LAYER_FFN_BOUNDS) - 1:
            ffn2.up(h + 1)
        else:
            ffn1.up(0)
        ffn2.down(h)
        _pairwise_tile(qs_ref, k_ref, cum_ref, v_ref, opw_ref, h, C)
        scores = jnp.zeros((C, C), F32)
        for L, same in levels:
            qt, kt = _level_operands(qs_ref, k_ref, cum_ref, h, C, L)
            sc = _dot_nt(qt, kt)
            scores = scores + (sc if same is None else jnp.where(same, sc, 0.0))
        cum_h = cum_ref[h]
        cl = _row_bcast(cum_ref, h, C - 1, C)
        qbar = (qs_ref[h] * jnp.exp(cum_h)).astype(BF16)
        khat = (k_ref[h] * jnp.exp(cl - cum_h)).astype(BF16)
        dec = jnp.exp(cum_ref[h, pl.ds(C - 1, 1), :])
        v_h = v_ref[h]
        st = st_ref[h] * keep
        o = _dot(scores.astype(BF16), v_h.astype(BF16)) + _dot_nt(qbar, st.astype(BF16)) + opw_ref[h]
        st_ref[h] = st * dec + _dot(v_h.T.astype(BF16), khat)
        ms = jnp.mean(o * o, axis=-1, keepdims=True)
        obuf_ref[:, hs] = (o * lax.rsqrt(ms + EPS) * hn_ref[:, hs] * _sigmoid(gate[:, hs])).astype(BF16)

    y_ref[...] = _rms(ffn2.result(), fn_ref[...])
    ffn1.up(1)
    ffn1.down(0)

    rowi = lax.broadcasted_iota(jnp.int32, (C, 1), 0)
    xb = proj_ref[:, 4 * D_A:4 * D_A + D_B]
    yb = proj_ref[:, 4 * D_A + D_B:D_IN]
    xb_ref[0:8, :] = xb_ref[0:8, :] * keep
    xb_ref[pl.ds(8, C), :] = xb
    xc = (cb_ref[...] + cw_ref[3:4, :] * xb + cw_ref[2:3, :] * xb_ref[pl.ds(7, C), :]
          + cw_ref[1:2, :] * xb_ref[pl.ds(6, C), :] + cw_ref[0:1, :] * xb_ref[pl.ds(5, C), :])
    xb_ref[0:8, :] = xb[C - 8:C, :]
    a, mult, gate_i = _lru_coeffs(xc, wa_ref, ba_ref, wx_ref, bx_ref, lam_ref)
    ffn1.up(2)
    ffn1.down(1)
    mult = jnp.where((rowi == 0) & first, 1.0, mult)
    hseq = _scan_tiles(a, mult * gate_i * xc, h_ref[...] * keep)
    h_ref[...] = hseq[C - 1:C, :]
    obuf_ref[:, D_A:D_A + D_B] = (hseq * _gelu_tanh(yb)).astype(BF16)

    x2_ref[...] = x + _dot(obuf_ref[...], wo_ref[...])
    ffn1.up(3)
    ffn1.down(2)
    ffn1.down(3)
    x1_ref[...] = ffn1.result()

    @pl.when((pos_in_seq == chunks_per_seq - 1) & (s >= 1) & (s <= n_chunks))
    def _():
        for h in range(N_HEADS):
            shg_ref[0, h] = st_ref[h].T
        slru_ref[0] = hseq[C - 1:C, :]
        scv_ref[0] = xb[C - (CONV_W - 1):C, :]


def _layer_prompt(x, ffn1, mix, ffn2, fn):
    b, t, _ = x.shape
    C = MIX_CHUNK
    assert t % C == 0 and C % (2 * SUBLANES) == 0 and C & (C - 1) == 0
    assert LAYER_FFN_BOUNDS[0] == 0 and LAYER_FFN_BOUNDS[-1] == D_FF and len(LAYER_FFN_BOUNDS) - 1 == N_HEADS
    nt = t // C
    n_chunks = b * nt
    rows_in = pl.BlockSpec((C, D_MODEL), lambda s: (jnp.minimum(s, n_chunks - 1), 0))
    rows_out = pl.BlockSpec((C, D_MODEL), lambda s: (jnp.clip(s - 2, 0, n_chunks - 1), 0))
    seq_of = lambda s: jnp.clip((s - 1) // nt, 0, b - 1)
    vec = lambda n: _resident((1, n))
    ffn_specs = [vec(D_MODEL), _resident((D_MODEL, D_FF)), _resident((D_MODEL, D_FF)), _resident((D_FF, D_MODEL))]
    mix_specs = [vec(D_MODEL), _resident((D_MODEL, D_IN)), _resident((2, D_A)), vec(D_A),
                 _resident((CONV_W, D_B)), vec(D_B), _resident((D_B, D_B)), vec(D_B), _resident((D_B, D_B)),
                 vec(D_B), vec(D_B), _resident((D_A + D_B, D_MODEL))]
    out_shape = (jax.ShapeDtypeStruct((b * t, D_MODEL), F32),
                 jax.ShapeDtypeStruct((b, N_HEADS, D_HEAD, D_HEAD), F32),
                 jax.ShapeDtypeStruct((b, 1, D_B), F32),
                 jax.ShapeDtypeStruct((b, CONV_W - 1, D_B), F32))
    out_specs = (rows_out,
                 pl.BlockSpec((1, N_HEADS, D_HEAD, D_HEAD), lambda s: (seq_of(s), 0, 0, 0)),
                 pl.BlockSpec((1, 1, D_B), lambda s: (seq_of(s), 0, 0)),
                 pl.BlockSpec((1, CONV_W - 1, D_B), lambda s: (seq_of(s), 0, 0)))
    per_head = pltpu.VMEM((N_HEADS, C, D_HEAD), F32)
    scratch = [pltpu.VMEM((C, D_MODEL), F32),
               pltpu.VMEM((C, D_MODEL), F32),
               pltpu.VMEM((C, D_IN), F32),
               pltpu.VMEM((N_HEADS, D_HEAD, D_HEAD), F32),
               pltpu.VMEM((1, D_B), F32),
               pltpu.VMEM((C + 8, D_B), F32),
               per_head, per_head, per_head, per_head,
               per_head,
               pltpu.VMEM((C, D_A + D_B), BF16)]
    return pl.pallas_call(
        functools.partial(_layer_prompt_kernel, chunk=C, chunks_per_seq=nt, n_chunks=n_chunks),
        out_shape=out_shape,
        grid=(n_chunks + 2,),
        in_specs=[rows_in] + ffn_specs + mix_specs + ffn_specs + [vec(D_MODEL)],
        out_specs=out_specs,
        scratch_shapes=scratch,
        compiler_params=pltpu.CompilerParams(dimension_semantics=("arbitrary",),
                                             vmem_limit_bytes=LAYER_VMEM_LIMIT_BYTES),
        name="layer_prompt",
    )(x.reshape(b * t, D_MODEL), *ffn1, *mix, *ffn2, fn)


def _mixer_sample_kernel(x_ref, mn_ref, win_ref, lb_ref, hn_ref, cw_ref, cb_ref, wa_ref, ba_ref, wx_ref,
                         bx_ref, lam_ref, wo_ref, s0_ref, h0_ref, hist1_ref, hist2_ref, hist3_ref,
                         y_ref, s1_ref, hseq_ref, xb_out_ref,
                         proj_ref, xb_ref, kpad_ref, cpad_ref, vpad_ref, qbar_ref, ost_ref, *, steps):
    T = steps
    R = x_ref.shape[0]
    pad = kpad_ref.shape[0] - R

    zpad = jnp.zeros((pad, D_A), F32)
    kpad_ref[0:pad, :] = zpad
    cpad_ref[0:pad, :] = zpad
    vpad_ref[0:pad, :] = zpad
    xb_ref[0:8, :] = jnp.zeros((8, D_B), F32)

    x = x_ref[...]
    xn = _rms(x, mn_ref[...]).astype(BF16)
    proj_ref[...] = _dot(xn, win_ref[...])

    qs, k, g = _hgrn_gates(proj_ref, lb_ref)
    v = proj_ref[:, 2 * D_A:3 * D_A]
    row = lax.broadcasted_iota(jnp.int32, (R, R), 0)
    col = lax.broadcasted_iota(jnp.int32, (R, R), 1)
    tshift = T.bit_length() - 1
    tri = jnp.where((col <= row) & ((row >> tshift) == (col >> tshift)), 1.0, 0.0).astype(BF16)
    g1, g2, g3 = _split3_bf16(g)
    cum = _dot(tri, g1) + _dot(tri, g2) + _dot(tri, g3)
    o_pw = _pairwise(qs, k, cum, v, kpad_ref, cpad_ref, vpad_ref, T, R)

    ecum = jnp.exp(cum)
    qbar_ref[...] = qs * ecum
    upper = jnp.where((col > row) & ((row >> tshift) == (col >> tshift)), 1.0, 0.0).astype(BF16)
    tail = _dot(upper, g1) + _dot(upper, g2) + _dot(upper, g3)
    khat = k * jnp.exp(tail)
    row8 = lax.broadcasted_iota(jnp.int32, (8, 1), 0)
    for h in range(N_HEADS):
        hs = slice(h * D_HEAD, (h + 1) * D_HEAD)
        khat_t = khat[:, hs].T
        dec_t = ecum[:, hs].T
        for p in range(R // 8):
            q8 = qbar_ref[pl.ds(8 * p, 8), hs].astype(BF16)
            o8 = jnp.zeros((8, D_HEAD), F32)
            for i in range(8 // T):
                b = p * (8 // T) + i
                s0 = s0_ref[b, h]
                ob = _dot(q8, s0.astype(BF16))
                o8 = jnp.where((row8 >= i * T) & (row8 < (i + 1) * T), ob, o8)
                acc = s0 * dec_t[:, b * T + T - 1:b * T + T]
                for tt in range(T):
                    j = b * T + tt
                    acc = acc + khat_t[:, j:j + 1] * v[j:j + 1, hs]
                s1_ref[b, h] = acc
            ost_ref[pl.ds(8 * p, 8), hs] = o8
    o_heads = [o_pw[h] + ost_ref[:, h * D_HEAD:(h + 1) * D_HEAD] for h in range(N_HEADS)]
    oa = _head_norm_gate(o_heads, hn_ref, proj_ref[:, 3 * D_A:4 * D_A])

    rowi = lax.broadcasted_iota(jnp.int32, (R, 1), 0)
    tpos = rowi & (T - 1)
    xb = proj_ref[:, 4 * D_A:4 * D_A + D_B]
    yb = proj_ref[:, 4 * D_A + D_B:D_IN]
    xb_ref[pl.ds(8, R), :] = xb
    xb_out_ref[...] = xb
    hist = (hist1_ref, hist2_ref, hist3_ref)
    xc = cb_ref[...] + cw_ref[CONV_W - 1:CONV_W, :] * xb
    for d in range(1, CONV_W):
        prev = jnp.where(tpos >= d, xb_ref[pl.ds(8 - d, R), :], hist[d - 1][...])
        xc = xc + cw_ref[CONV_W - 1 - d:CONV_W - d, :] * prev
    a, mult, gate_i = _lru_coeffs(xc, wa_ref, ba_ref, wx_ref, bx_ref, lam_ref)
    a_run, h_run = _scan_rows(a, mult * gate_i * xc, tpos, T)
    hseq = h_run + a_run * h0_ref[...]
    hseq_ref[...] = hseq
    ob = hseq * _gelu_tanh(yb)

    o = jnp.concatenate(oa + [ob], axis=-1).astype(BF16)
    y_ref[...] = x + _dot(o, wo_ref[...])


def _mixer_sample(x, mn, win, lb, hn, cw, cb, wa, ba, wx, bx, lam, wo, s0, h0, buf):
    nb, T, _ = x.shape
    assert T == CONV_W and 8 % T == 0, "sample kernel assumes DEC_SEQ == CONV_W == 4"
    G = SAMPLE_GROUP
    R = G * T
    assert nb % G == 0 and R % 8 == 0
    xr = x.reshape(nb * T, D_MODEL)
    h0x = jnp.repeat(h0, T, axis=0)
    hists = [jnp.concatenate([buf[:, CONV_W - 1 - d:, :], jnp.zeros((nb, T - d, D_B), F32)], axis=1)
             .reshape(nb * T, D_B) for d in range(1, CONV_W)]
    rows = lambda n: pl.BlockSpec((R, n), lambda i: (i, 0))
    state = pl.BlockSpec((G, N_HEADS, D_HEAD, D_HEAD), lambda i: (i, 0, 0, 0))
    vec = lambda n: _resident((1, n))
    out_shape = (jax.ShapeDtypeStruct((nb * T, D_MODEL), F32),
                 jax.ShapeDtypeStruct((nb, N_HEADS, D_HEAD, D_HEAD), F32),
                 jax.ShapeDtypeStruct((nb * T, D_B), F32),
                 jax.ShapeDtypeStruct((nb * T, D_B), F32))
    scratch = [pltpu.VMEM((R, D_IN), F32),
               pltpu.VMEM((R + 8, D_B), F32),
               pltpu.VMEM((R + 8, D_A), F32),
               pltpu.VMEM((R + 8, D_A), F32),
               pltpu.VMEM((R + 8, D_A), F32),
               pltpu.VMEM((R, D_A), F32),
               pltpu.VMEM((R, D_A), F32)]
    y, s1, hseq, xb = pl.pallas_call(
        functools.partial(_mixer_sample_kernel, steps=T),
        out_shape=out_shape,
        grid=(nb // G,),
        in_specs=[rows(D_MODEL), vec(D_MODEL), _resident((D_MODEL, D_IN)), _resident((2, D_A)), vec(D_A),
                  _resident((CONV_W, D_B)), vec(D_B), _resident((D_B, D_B)), vec(D_B), _resident((D_B, D_B)),
                  vec(D_B), vec(D_B), _resident((D_A + D_B, D_MODEL)),
                  state, rows(D_B), rows(D_B), rows(D_B), rows(D_B)],
        out_specs=(rows(D_MODEL), state, rows(D_B), rows(D_B)),
        scratch_shapes=scratch,
        compiler_params=pltpu.CompilerParams(dimension_semantics=("arbitrary",),
                                             vmem_limit_bytes=VMEM_LIMIT_BYTES),
        name="mixer_sample",
    )(xr, mn, win, lb, hn, cw, cb, wa, ba, wx, bx, lam, wo, s0, h0x, *hists)
    hseq = hseq.reshape(nb, T, D_B)
    xb = xb.reshape(nb, T, D_B)
    return y, s1, hseq[:, T - 1], xb[:, T - (CONV_W - 1):]


def _block_diag(w):
    n, bi, bj = w.shape
    eye = jnp.eye(n, dtype=w.dtype)
    return (w[:, :, None, :] * eye[:, None, :, None]).reshape(n * bi, n * bj)


def kernel(x_prompt, x_sample, state_hgrn, state_lru, state_conv, ffn1_norm, ffn1_wg, ffn1_wu, ffn1_wd,
           mix_norm, w_in, hgrn_lb, hgrn_norm, conv_w, conv_b, lru_wa, lru_ba, lru_wx, lru_bx, lru_lambda,
           w_o, ffn2_norm, ffn2_wg, ffn2_wu, ffn2_wd, final_norm):
    assert ffn1_norm.shape[0] == 1 and hgrn_lb.shape[0] == 2, "single-layer model"
    bp, tp, _ = x_prompt.shape
    bs, ts, _ = x_sample.shape

    ffn1 = (ffn1_norm, ffn1_wg[0].astype(BF16), ffn1_wu[0].astype(BF16), ffn1_wd[0].astype(BF16))
    ffn2 = (ffn2_norm, ffn2_wg[0].astype(BF16), ffn2_wu[0].astype(BF16), ffn2_wd[0].astype(BF16))
    fn = final_norm.reshape(1, D_MODEL)
    mix = (mix_norm, w_in[0].astype(BF16), hgrn_lb, hgrn_norm, conv_w[0], conv_b,
           _block_diag(lru_wa[0]).astype(BF16), lru_ba[0].reshape(1, D_B),
           _block_diag(lru_wx[0]).astype(BF16), lru_bx[0].reshape(1, D_B), lru_lambda, w_o[0].astype(BF16))

    yp, hg_p, lru_p, cv_p = _layer_prompt(x_prompt, ffn1, mix, ffn2, fn)

    xs = _ffn(x_sample.reshape(bs * ts, D_MODEL), *ffn1, fn, apply_final_norm=False, name="ffn1_sample")
    xs, hg_s, lru_s, cv_s = _mixer_sample(xs.reshape(bs, ts, D_MODEL), *mix, state_hgrn[0], state_lru[0],
                                          state_conv[0])
    ys = _ffn(xs, *ffn2, fn, apply_final_norm=True, name="ffn2_sample")

    return (yp.reshape(bp, tp, D_MODEL), ys.reshape(bs, ts, D_MODEL),
            hg_p[None], lru_p.reshape(1, bp, D_B), cv_p[None],
            hg_s[None], lru_s[None], cv_s[None])
```

```python
import functools
import math

import jax
import jax.numpy as jnp
from jax import lax
from jax.experimental import pallas as pl
from jax.experimental.pallas import tpu as pltpu

F32 = jnp.float32
BF16 = jnp.bfloat16

D_MODEL = 1024
D_FF = 2816
D_A = 512
D_B = 512
N_HEADS = 4
D_HEAD = 128
D_IN = 4 * D_A + 2 * D_B
CONV_W = 4
LRU_C = 8.0
EPS = 1e-6

SUBLANES = 8
NEG_LARGE = -1e30
FFN_TILE = 512
FFN_CHUNK = 256
MIX_CHUNK = 256
LAYER_FFN_BOUNDS = (0, 768, 1536, 2304, 2816)
LAYER_FILL_STEPS = 2
SAMPLE_GROUP = 8
VMEM_LIMIT_BYTES = 56 * 1024 * 1024
LAYER_VMEM_LIMIT_BYTES = 63 * 1024 * 1024

_NT = (((1,), (1,)), ((), ()))


def _dot(a, b):
    return jnp.dot(a, b, preferred_element_type=F32)


def _dot_nt(a, b):
    return lax.dot_general(a, b, _NT, preferred_element_type=F32)


def _rms(x, w):
    ms = jnp.mean(x * x, axis=-1, keepdims=True)
    return x * lax.rsqrt(ms + EPS) * w


def _sigmoid(x):
    return 1.0 / (1.0 + jnp.exp(-x))


def _log1p(x):
    u = 1.0 + x
    return jnp.where(u == 1.0, x, jnp.log(u) * x / (u - 1.0))


def _expm1(x):
    u = jnp.exp(x)
    return jnp.where(u == 1.0, x, (u - 1.0) * x / jnp.log(u))


def _softplus(x):
    return jnp.maximum(x, 0.0) + _log1p(jnp.exp(-jnp.abs(x)))


def _gelu_tanh(x):
    return 0.5 * x * (1.0 + jnp.tanh(math.sqrt(2.0 / math.pi) * (x + 0.044715 * (x * x * x))))


def _split3_bf16(x):
    x1 = x.astype(BF16)
    r1 = x - x1.astype(F32)
    x2 = r1.astype(BF16)
    r2 = r1 - x2.astype(F32)
    return x1, x2, r2.astype(BF16)


def _ffn_kernel(x_ref, nw_ref, wg_ref, wu_ref, wd_ref, o_ref):
    x = x_ref[...]
    xn = _rms(x, nw_ref[...]).astype(BF16)
    acc = jnp.zeros(x.shape, F32)
    for c in range(D_FF // FFN_CHUNK):
        sl = slice(c * FFN_CHUNK, (c + 1) * FFN_CHUNK)
        g = _dot(xn, wg_ref[:, sl])
        u = _dot(xn, wu_ref[:, sl])
        h = (g * _sigmoid(g) * u).astype(BF16)
        acc = acc + _dot(h, wd_ref[sl, :])
    o_ref[...] = x + 0.5 * acc


def _resident(shape):
    zeros = (0,) * len(shape)
    return pl.BlockSpec(shape, lambda *_: zeros, pipeline_mode=pl.Buffered(1))


def _ffn(x, norm_w, wg, wu, wd, *, name):
    n = x.shape[0]
    tm = min(FFN_TILE, n)
    assert n % tm == 0 and D_FF % FFN_CHUNK == 0
    row = pl.BlockSpec((tm, D_MODEL), lambda i: (i, 0))
    return pl.pallas_call(
        _ffn_kernel,
        out_shape=jax.ShapeDtypeStruct((n, D_MODEL), F32),
        grid=(n // tm,),
        in_specs=[row, _resident((1, D_MODEL)), _resident((D_MODEL, D_FF)), _resident((D_MODEL, D_FF)),
                  _resident((D_FF, D_MODEL))],
        out_specs=row,
        compiler_params=pltpu.CompilerParams(dimension_semantics=("arbitrary",),
                                             vmem_limit_bytes=VMEM_LIMIT_BYTES),
        name=name,
    )(x, norm_w, wg, wu, wd)


def _hgrn_gates(proj_ref, lb_ref):
    l0 = lb_ref[0:1, :]
    l1 = lb_ref[1:2, :]
    m = jnp.maximum(l0, l1)
    e0 = jnp.exp(l0 - m)
    e1 = jnp.exp(l1 - m)
    lb = e0 / (e0 + e1)
    q = proj_ref[:, 0:D_A]
    fz = proj_ref[:, D_A:2 * D_A]
    f = lb + (1.0 - lb) * _sigmoid(fz)
    return q * _sigmoid(q), 1.0 - f, jnp.log(f)


def _pairwise(qs, k, cum, v, kpad_ref, cpad_ref, vpad_ref, out_ref, block, n_rows):
    pad = kpad_ref.shape[0] - n_rows
    kpad_ref[pl.ds(pad, n_rows), :] = k
    cpad_ref[pl.ds(pad, n_rows), :] = cum
    vpad_ref[pl.ds(pad, n_rows), :] = v
    rmod = lax.broadcasted_iota(jnp.int32, (n_rows, 1), 0) & (block - 1)
    out = [jnp.zeros((n_rows, D_HEAD), F32) for _ in range(N_HEADS)]
    for d in range(block):
        if d == 0:
            p = qs * k
            vs = v
        else:
            ks = kpad_ref[pl.ds(pad - d, n_rows), :]
            cs = cpad_ref[pl.ds(pad - d, n_rows), :]
            vs = vpad_ref[pl.ds(pad - d, n_rows), :]
            p = qs * ks * jnp.exp(cum - cs)
        valid = rmod >= d
        for h in range(N_HEADS):
            hs = slice(h * D_HEAD, (h + 1) * D_HEAD)
            s = jnp.sum(p[:, hs], axis=-1, keepdims=True)
            s = jnp.where(valid, s, 0.0)
            out[h] = out[h] + s * vs[:, hs]
    for h in range(N_HEADS):
        out_ref[:, h * D_HEAD:(h + 1) * D_HEAD] = out[h]


def _lru_coeffs(xc, wa_ref, ba_ref, wx_ref, bx_ref, lam_ref):
    xcb = xc.astype(BF16)
    r = _sigmoid(_dot(xcb, wa_ref[...]) + ba_ref[...])
    i = _sigmoid(_dot(xcb, wx_ref[...]) + bx_ref[...])
    log_a = -LRU_C * r * _softplus(-lam_ref[...])
    a = jnp.exp(log_a)
    mult = jnp.sqrt(-_expm1(2.0 * log_a))
    return a, mult, i


def _scan_rows(a, u, rowpos, span):
    s = 1
    while s < span:
        keep = rowpos >= s
        a_sh = jnp.where(keep, pltpu.roll(a, s, 0), 1.0)
        u_sh = jnp.where(keep, pltpu.roll(u, s, 0), 0.0)
        u = a * u_sh + u
        a = a * a_sh
        s *= 2
    return a, u


def _row_bcast(ref, h, r, n):
    return jnp.broadcast_to(ref[h, pl.ds(r, 1), :], (n, D_HEAD))


def _pairwise_tile(qs_ref, k_ref, cum_ref, v_ref, out_ref, h, n_rows):
    pos = lax.broadcasted_iota(jnp.int32, (SUBLANES, 1), 0)
    for j in range(n_rows // SUBLANES):
        rows = pl.ds(SUBLANES * j, SUBLANES)
        q_t = qs_ref[h, rows, :]
        c_t = cum_ref[h, rows, :]
        acc = jnp.zeros((SUBLANES, D_HEAD), F32)
        for s in range(SUBLANES):
            src = SUBLANES * j + s
            d = c_t - _row_bcast(cum_ref, h, src, SUBLANES)
            if s > 0:
                d = jnp.where(pos >= s, d, NEG_LARGE)
            p = q_t * _row_bcast(k_ref, h, src, SUBLANES) * jnp.exp(d)
            acc = acc + jnp.sum(p, axis=-1, keepdims=True) * _row_bcast(v_ref, h, src, SUBLANES)
        out_ref[h, rows, :] = acc


def _level_operands(qs_ref, k_ref, cum_ref, h, n_rows, L):
    half = L // 2
    zeros = jnp.zeros((half, D_HEAD), F32)
    qparts, kparts = [], []
    for b in range(n_rows // L):
        r0 = b * L
        first = pl.ds(r0, half)
        second = pl.ds(r0 + half, half)
        mid = _row_bcast(cum_ref, h, r0 + half - 1, half)
        kparts += [k_ref[h, first, :] * jnp.exp(mid - cum_ref[h, first, :]), zeros]
        qparts += [zeros, qs_ref[h, second, :] * jnp.exp(cum_ref[h, second, :] - mid)]
    return jnp.concatenate(qparts, axis=0).astype(BF16), jnp.concatenate(kparts, axis=0).astype(BF16)


def _scan_tiles(a, u, carry):
    n = a.shape[0]
    a3 = a.reshape(n // SUBLANES, SUBLANES, D_B)
    u3 = u.reshape(n // SUBLANES, SUBLANES, D_B)
    pos = lax.broadcasted_iota(jnp.int32, (1, SUBLANES, 1), 1)
    s = 1
    while s < SUBLANES:
        keep = pos >= s
        a_sh = jnp.where(keep, pltpu.roll(a3, s, 1), 1.0)
        u_sh = jnp.where(keep, pltpu.roll(u3, s, 1), 0.0)
        u3 = a3 * u_sh + u3
        a3 = a3 * a_sh
        s *= 2
    tiles = []
    for j in range(n // SUBLANES):
        hj = u3[j] + a3[j] * carry
        tiles.append(hj)
        carry = hj[SUBLANES - 1:SUBLANES, :]
    return jnp.concatenate(tiles, axis=0)


class _SwigluChunks:
    def __init__(self, x, nw_ref, wg_ref, wu_ref, wd_ref):
        self.x = x
        self.xn = _rms(x, nw_ref[...]).astype(BF16)
        self.wg_ref, self.wu_ref, self.wd_ref = wg_ref, wu_ref, wd_ref
        self.hidden = {}
        self.acc = None

    @staticmethod
    def _cols(c):
        return slice(LAYER_FFN_BOUNDS[c], LAYER_FFN_BOUNDS[c + 1])

    def up(self, c):
        g = _dot(self.xn, self.wg_ref[:, self._cols(c)])
        u = _dot(self.xn, self.wu_ref[:, self._cols(c)])
        self.hidden[c] = (g * _sigmoid(g) * u).astype(BF16)

    def down(self, c):
        d = _dot(self.hidden.pop(c), self.wd_ref[self._cols(c), :])
        self.acc = d if self.acc is None else self.acc + d

    def result(self):
        assert not self.hidden
        return self.x + 0.5 * self.acc


def _layer_prompt_kernel(x_ref, xs_ref, n1_ref, wg1_ref, wu1_ref, wd1_ref,
                         mn_ref, win_ref, lb_ref, hn_ref, cw_ref, cb_ref, wa_ref, ba_ref, wx_ref,
                         bx_ref, lam_ref, wo_ref,
                         n2_ref, wg2_ref, wu2_ref, wd2_ref, fn_ref,
                         y_ref, ys_ref, shg_ref, slru_ref, scv_ref,
                         x1_ref, x2_ref, proj_ref, st_ref, h_ref, xb_ref, qs_ref, k_ref, cum_ref, v_ref,
                         opw_ref, obuf_ref, *, chunk, chunks_per_seq, n_chunks):
    C = chunk
    s = pl.program_id(0)

    @pl.when(s == 0)
    def _():
        x1_ref[...] = jnp.zeros(x1_ref.shape, F32)
        st_ref[...] = jnp.zeros(st_ref.shape, F32)
        h_ref[...] = jnp.zeros(h_ref.shape, F32)
        xb_ref[0:8, :] = jnp.zeros((8, D_B), F32)

    @pl.when(s < LAYER_FILL_STEPS)
    def _():
        x2_ref[...] = xs_ref[...]

    ffn2 = _SwigluChunks(x2_ref[...], n2_ref, wg2_ref, wu2_ref, wd2_ref)
    ffn1 = _SwigluChunks(x_ref[...], n1_ref, wg1_ref, wu1_ref, wd1_ref)

    pos_in_seq = lax.rem(s + (chunks_per_seq - 1), chunks_per_seq)
    first = pos_in_seq == 0
    keep = jnp.where(first, 0.0, 1.0)
    x = x1_ref[...]
    xn = _rms(x, mn_ref[...]).astype(BF16)
    proj_ref[...] = _dot(xn, win_ref[...])
    ffn2.up(0)

    qs, k, g = _hgrn_gates(proj_ref, lb_ref)
    row = lax.broadcasted_iota(jnp.int32, (C, C), 0)
    col = lax.broadcasted_iota(jnp.int32, (C, C), 1)
    tri = jnp.where(col <= row, 1.0, 0.0).astype(BF16)
    g1, g2, g3 = _split3_bf16(g)
    cum = _dot(tri, g1) + _dot(tri, g2) + _dot(tri, g3)
    for h in range(N_HEADS):
        hs = slice(h * D_HEAD, (h + 1) * D_HEAD)
        qs_ref[h] = qs[:, hs]
        k_ref[h] = k[:, hs]
        cum_ref[h] = cum[:, hs]
        v_ref[h] = proj_ref[:, 2 * D_A + h * D_HEAD:2 * D_A + (h + 1) * D_HEAD]

    levels = []
    L = 2 * SUBLANES
    while L <= C:
        shift = L.bit_length() - 1
        levels.append((L, None if L == C else (row >> shift) == (col >> shift)))
        L *= 2

    gate = proj_ref[:, 3 * D_A:4 * D_A]
    for h in range(N_HEADS):
        hs = slice(h * D_HEAD, (h + 1) * D_HEAD)
        if h + 1 < len(LAYER_FFN_BOUNDS) - 1:
            ffn2.up(h + 1)
        else:
            ffn1.up(0)
        ffn2.down(h)
        _pairwise_tile(qs_ref, k_ref, cum_ref, v_ref, opw_ref, h, C)
        scores = jnp.zeros((C, C), F32)
        for L, same in levels:
            qt, kt = _level_operands(qs_ref, k_ref, cum_ref, h, C, L)
            sc = _dot_nt(qt, kt)
            scores = scores + (sc if same is None else jnp.where(same, sc, 0.0))
        cum_h = cum_ref[h]
        cl = _row_bcast(cum_ref, h, C - 1, C)
        qbar = (qs_ref[h] * jnp.exp(cum_h)).astype(BF16)
        khat = (k_ref[h] * jnp.exp(cl - cum_h)).astype(BF16)
        dec = jnp.exp(cum_ref[h, pl.ds(C - 1, 1), :])
        v_h = v_ref[h]
        st = st_ref[h] * keep
        o = _dot(scores.astype(BF16), v_h.astype(BF16)) + _dot_nt(qbar, st.astype(BF16)) + opw_ref[h]
        st_ref[h] = st * dec + _dot(v_h.T.astype(BF16), khat)
        ms = jnp.mean(o * o, axis=-1, keepdims=True)
        obuf_ref[:, hs] = (o * lax.rsqrt(ms + EPS) * hn_ref[:, hs] * _sigmoid(gate[:, hs])).astype(BF16)

    y_ref[...] = _rms(ffn2.result(), fn_ref[...])
    ffn1.up(1)
    ffn1.down(0)

    rowi = lax.broadcasted_iota(jnp.int32, (C, 1), 0)
    xb = proj_ref[:, 4 * D_A:4 * D_A + D_B]
    yb = proj_ref[:, 4 * D_A + D_B:D_IN]
    xb_ref[0:8, :] = xb_ref[0:8, :] * keep
    xb_ref[pl.ds(8, C), :] = xb
    xc = (cb_ref[...] + cw_ref[3:4, :] * xb + cw_ref[2:3, :] * xb_ref[pl.ds(7, C), :]
          + cw_ref[1:2, :] * xb_ref[pl.ds(6, C), :] + cw_ref[0:1, :] * xb_ref[pl.ds(5, C), :])
    xb_ref[0:8, :] = xb[C - 8:C, :]
    a, mult, gate_i = _lru_coeffs(xc, wa_ref, ba_ref, wx_ref, bx_ref, lam_ref)
    ffn1.up(2)
    ffn1.down(1)
    mult = jnp.where((rowi == 0) & first, 1.0, mult)
    hseq = _scan_tiles(a, mult * gate_i * xc, h_ref[...] * keep)
    h_ref[...] = hseq[C - 1:C, :]
    obuf_ref[:, D_A:D_A + D_B] = (hseq * _gelu_tanh(yb)).astype(BF16)

    x2_ref[...] = x + _dot(obuf_ref[...], wo_ref[...])
    ffn1.up(3)
    ffn1.down(2)
    ffn1.down(3)
    x1_ref[...] = ffn1.result()

    @pl.when(s < LAYER_FILL_STEPS)
    def _():
        ys_ref[...] = y_ref[...]

    @pl.when((pos_in_seq == chunks_per_seq - 1) & (s >= 1) & (s <= n_chunks))
    def _():
        for h in range(N_HEADS):
            shg_ref[0, h] = st_ref[h].T
        slru_ref[0] = hseq[C - 1:C, :]
        scv_ref[0] = xb[C - (CONV_W - 1):C, :]


def _layer_prompt(x, xs, ffn1, mix, ffn2, fn):
    b, t, _ = x.shape
    C = MIX_CHUNK
    assert t % C == 0 and C % (2 * SUBLANES) == 0 and C & (C - 1) == 0
    assert LAYER_FFN_BOUNDS[0] == 0 and LAYER_FFN_BOUNDS[-1] == D_FF and len(LAYER_FFN_BOUNDS) - 1 == N_HEADS
    assert xs.shape == (LAYER_FILL_STEPS * C, D_MODEL), "sample rows must exactly fill the pipeline fill steps"
    nt = t // C
    n_chunks = b * nt
    rows_in = pl.BlockSpec((C, D_MODEL), lambda s: (jnp.minimum(s, n_chunks - 1), 0))
    rows_out = pl.BlockSpec((C, D_MODEL), lambda s: (jnp.clip(s - 2, 0, n_chunks - 1), 0))
    fill_index = lambda s: (jnp.minimum(s, LAYER_FILL_STEPS - 1), 0)
    rows_fill = pl.BlockSpec((C, D_MODEL), fill_index)
    rows_fill_in = pl.BlockSpec((C, D_MODEL), fill_index, pipeline_mode=pl.Buffered(1))
    seq_of = lambda s: jnp.clip((s - 1) // nt, 0, b - 1)
    vec = lambda n: _resident((1, n))
    ffn_specs = [vec(D_MODEL), _resident((D_MODEL, D_FF)), _resident((D_MODEL, D_FF)), _resident((D_FF, D_MODEL))]
    mix_specs = [vec(D_MODEL), _resident((D_MODEL, D_IN)), _resident((2, D_A)), vec(D_A),
                 _resident((CONV_W, D_B)), vec(D_B), _resident((D_B, D_B)), vec(D_B), _resident((D_B, D_B)),
                 vec(D_B), vec(D_B), _resident((D_A + D_B, D_MODEL))]
    out_shape = (jax.ShapeDtypeStruct((b * t, D_MODEL), F32),
                 jax.ShapeDtypeStruct(xs.shape, F32),
                 jax.ShapeDtypeStruct((b, N_HEADS, D_HEAD, D_HEAD), F32),
                 jax.ShapeDtypeStruct((b, 1, D_B), F32),
                 jax.ShapeDtypeStruct((b, CONV_W - 1, D_B), F32))
    out_specs = (rows_out, rows_fill,
                 pl.BlockSpec((1, N_HEADS, D_HEAD, D_HEAD), lambda s: (seq_of(s), 0, 0, 0)),
                 pl.BlockSpec((1, 1, D_B), lambda s: (seq_of(s), 0, 0)),
                 pl.BlockSpec((1, CONV_W - 1, D_B), lambda s: (seq_of(s), 0, 0)))
    per_head = pltpu.VMEM((N_HEADS, C, D_HEAD), F32)
    scratch = [pltpu.VMEM((C, D_MODEL), F32),
               pltpu.VMEM((C, D_MODEL), F32),
               pltpu.VMEM((C, D_IN), F32),
               pltpu.VMEM((N_HEADS, D_HEAD, D_HEAD), F32),
               pltpu.VMEM((1, D_B), F32),
               pltpu.VMEM((C + 8, D_B), F32),
               per_head, per_head, per_head, per_head,
               per_head,
               pltpu.VMEM((C, D_A + D_B), BF16)]
    return pl.pallas_call(
        functools.partial(_layer_prompt_kernel, chunk=C, chunks_per_seq=nt, n_chunks=n_chunks),
        out_shape=out_shape,
        grid=(n_chunks + LAYER_FILL_STEPS,),
        in_specs=[rows_in, rows_fill_in] + ffn_specs + mix_specs + ffn_specs + [vec(D_MODEL)],
        out_specs=out_specs,
        scratch_shapes=scratch,
        compiler_params=pltpu.CompilerParams(dimension_semantics=("arbitrary",),
                                             vmem_limit_bytes=LAYER_VMEM_LIMIT_BYTES),
        name="layer_prompt",
    )(x.reshape(b * t, D_MODEL), xs, *ffn1, *mix, *ffn2, fn)


def _mixer_sample_kernel(x_ref, mn_ref, win_ref, lb_ref, hn_ref, cw_ref, cb_ref, wa_ref, ba_ref, wx_ref,
                         bx_ref, lam_ref, wo_ref, s0_ref, h0_ref, hist1_ref, hist2_ref, hist3_ref,
                         y_ref, s1_ref, hseq_ref, xb_out_ref,
                         proj_ref, xb_ref, kpad_ref, cpad_ref, vpad_ref, qbar_ref, khat_ref, ecum_ref,
                         opw_ref, ost_ref, ob_ref, *, steps):
    T = steps
    N = x_ref.shape[0]
    G = s0_ref.shape[0]
    R = G * T
    i = pl.program_id(0)
    tshift = T.bit_length() - 1

    @pl.when(i == 0)
    def _():
        pad = kpad_ref.shape[0] - N
        zpad = jnp.zeros((pad, D_A), F32)
        kpad_ref[0:pad, :] = zpad
        cpad_ref[0:pad, :] = zpad
        vpad_ref[0:pad, :] = zpad
        xb_ref[0:8, :] = jnp.zeros((8, D_B), F32)

        xn = _rms(x_ref[...], mn_ref[...]).astype(BF16)
        proj_ref[...] = _dot(xn, win_ref[...])

        qs, k, g = _hgrn_gates(proj_ref, lb_ref)
        v = proj_ref[:, 2 * D_A:3 * D_A]
        row = lax.broadcasted_iota(jnp.int32, (N, N), 0)
        col = lax.broadcasted_iota(jnp.int32, (N, N), 1)
        same_seq = (row >> tshift) == (col >> tshift)
        tri = jnp.where((col <= row) & same_seq, 1.0, 0.0).astype(BF16)
        upper = jnp.where((col > row) & same_seq, 1.0, 0.0).astype(BF16)
        g1, g2, g3 = _split3_bf16(g)
        cum = _dot(tri, g1) + _dot(tri, g2) + _dot(tri, g3)
        tail = _dot(upper, g1) + _dot(upper, g2) + _dot(upper, g3)
        _pairwise(qs, k, cum, v, kpad_ref, cpad_ref, vpad_ref, opw_ref, T, N)
        ecum = jnp.exp(cum)
        ecum_ref[...] = ecum
        qbar_ref[...] = qs * ecum
        khat_ref[...] = k * jnp.exp(tail)

        tpos = lax.broadcasted_iota(jnp.int32, (N, 1), 0) & (T - 1)
        xb = proj_ref[:, 4 * D_A:4 * D_A + D_B]
        yb = proj_ref[:, 4 * D_A + D_B:D_IN]
        xb_ref[pl.ds(8, N), :] = xb
        xb_out_ref[...] = xb
        hist = (hist1_ref, hist2_ref, hist3_ref)
        xc = cb_ref[...] + cw_ref[CONV_W - 1:CONV_W, :] * xb
        for d in range(1, CONV_W):
            prev = jnp.where(tpos >= d, xb_ref[pl.ds(8 - d, N), :], hist[d - 1][...])
            xc = xc + cw_ref[CONV_W - 1 - d:CONV_W - d, :] * prev
        a, mult, gate_i = _lru_coeffs(xc, wa_ref, ba_ref, wx_ref, bx_ref, lam_ref)
        a_run, h_run = _scan_rows(a, mult * gate_i * xc, tpos, T)
        hseq = h_run + a_run * h0_ref[...]
        hseq_ref[...] = hseq
        ob_ref[...] = hseq * _gelu_tanh(yb)

    r0 = pl.multiple_of(i * R, R)
    rows = pl.ds(r0, R)
    row8 = lax.broadcasted_iota(jnp.int32, (8, 1), 0)
    for h in range(N_HEADS):
        hs = slice(h * D_HEAD, (h + 1) * D_HEAD)
        khat_t = khat_ref[rows, hs].T
        dec_t = ecum_ref[rows, hs].T
        v_g = proj_ref[rows, 2 * D_A + h * D_HEAD:2 * D_A + (h + 1) * D_HEAD]
        for p in range(R // 8):
            tile = pl.ds(pl.multiple_of(r0 + 8 * p, 8), 8)
            q8 = qbar_ref[tile, hs].astype(BF16)
            o8 = jnp.zeros((8, D_HEAD), F32)
            for j in range(8 // T):
                b = p * (8 // T) + j
                s0 = s0_ref[b, h]
                ob = _dot(q8, s0.astype(BF16))
                o8 = jnp.where((row8 >= j * T) & (row8 < (j + 1) * T), ob, o8)
                acc = s0 * dec_t[:, b * T + T - 1:b * T + T]
                for tt in range(T):
                    r = b * T + tt
                    acc = acc + khat_t[:, r:r + 1] * v_g[r:r + 1, :]
                s1_ref[b, h] = acc
            ost_ref[tile, hs] = o8

    @pl.when(i == pl.num_programs(0) - 1)
    def _():
        gate = proj_ref[:, 3 * D_A:4 * D_A]
        outs = []
        for h in range(N_HEADS):
            hs = slice(h * D_HEAD, (h + 1) * D_HEAD)
            o = opw_ref[:, hs] + ost_ref[:, hs]
            ms = jnp.mean(o * o, axis=-1, keepdims=True)
            outs.append(o * lax.rsqrt(ms + EPS) * hn_ref[:, hs] * _sigmoid(gate[:, hs]))
        o = jnp.concatenate(outs + [ob_ref[...]], axis=-1).astype(BF16)
        y_ref[...] = x_ref[...] + _dot(o, wo_ref[...])


def _mixer_sample(x, mn, win, lb, hn, cw, cb, wa, ba, wx, bx, lam, wo, s0, h0, buf):
    nb, T, _ = x.shape
    assert T == CONV_W and 8 % T == 0, "sample kernel assumes DEC_SEQ == CONV_W == 4"
    G = SAMPLE_GROUP
    N = nb * T
    assert nb % G == 0 and (G * T) % 8 == 0
    xr = x.reshape(N, D_MODEL)
    h0x = jnp.repeat(h0, T, axis=0)
    hists = [jnp.concatenate([buf[:, CONV_W - 1 - d:, :], jnp.zeros((nb, T - d, D_B), F32)], axis=1)
             .reshape(N, D_B) for d in range(1, CONV_W)]
    whole = lambda n: _resident((N, n))
    whole_out = lambda n: pl.BlockSpec((N, n), lambda i: (0, 0))
    state = pl.BlockSpec((G, N_HEADS, D_HEAD, D_HEAD), lambda i: (i, 0, 0, 0))
    vec = lambda n: _resident((1, n))
    out_shape = (jax.ShapeDtypeStruct((N, D_MODEL), F32),
                 jax.ShapeDtypeStruct((nb, N_HEADS, D_HEAD, D_HEAD), F32),
                 jax.ShapeDtypeStruct((N, D_B), F32),
                 jax.ShapeDtypeStruct((N, D_B), F32))
    half = lambda: pltpu.VMEM((N, D_A), F32)
    scratch = [pltpu.VMEM((N, D_IN), F32),
               pltpu.VMEM((N + 8, D_B), F32),
               pltpu.VMEM((N + 8, D_A), F32), pltpu.VMEM((N + 8, D_A), F32), pltpu.VMEM((N + 8, D_A), F32),
               half(), half(), half(),
               half(), half(),
               pltpu.VMEM((N, D_B), F32)]
    y, s1, hseq, xb = pl.pallas_call(
        functools.partial(_mixer_sample_kernel, steps=T),
        out_shape=out_shape,
        grid=(nb // G,),
        in_specs=[whole(D_MODEL), vec(D_MODEL), _resident((D_MODEL, D_IN)), _resident((2, D_A)), vec(D_A),
                  _resident((CONV_W, D_B)), vec(D_B), _resident((D_B, D_B)), vec(D_B), _resident((D_B, D_B)),
                  vec(D_B), vec(D_B), _resident((D_A + D_B, D_MODEL)),
                  state, whole(D_B), whole(D_B), whole(D_B), whole(D_B)],
        out_specs=(whole_out(D_MODEL), state, whole_out(D_B), whole_out(D_B)),
        scratch_shapes=scratch,
        compiler_params=pltpu.CompilerParams(dimension_semantics=("arbitrary",),
                                             vmem_limit_bytes=LAYER_VMEM_LIMIT_BYTES),
        name="mixer_sample",
    )(xr, mn, win, lb, hn, cw, cb, wa, ba, wx, bx, lam, wo, s0, h0x, *hists)
    hseq = hseq.reshape(nb, T, D_B)
    xb = xb.reshape(nb, T, D_B)
    return y, s1, hseq[:, T - 1], xb[:, T - (CONV_W - 1):]


def _block_diag(w):
    n, bi, bj = w.shape
    eye = jnp.eye(n, dtype=w.dtype)
    return (w[:, :, None, :] * eye[:, None, :, None]).reshape(n * bi, n * bj)


def kernel(x_prompt, x_sample, state_hgrn, state_lru, state_conv, ffn1_norm, ffn1_wg, ffn1_wu, ffn1_wd,
           mix_norm, w_in, hgrn_lb, hgrn_norm, conv_w, conv_b, lru_wa, lru_ba, lru_wx, lru_bx, lru_lambda,
           w_o, ffn2_norm, ffn2_wg, ffn2_wu, ffn2_wd, final_norm):
    assert ffn1_norm.shape[0] == 1 and hgrn_lb.shape[0] == 2, "single-layer model"
    bp, tp, _ = x_prompt.shape
    bs, ts, _ = x_sample.shape

    ffn1 = (ffn1_norm, ffn1_wg[0].astype(BF16), ffn1_wu[0].astype(BF16), ffn1_wd[0].astype(BF16))
    ffn2 = (ffn2_norm, ffn2_wg[0].astype(BF16), ffn2_wu[0].astype(BF16), ffn2_wd[0].astype(BF16))
    fn = final_norm.reshape(1, D_MODEL)
    mix = (mix_norm, w_in[0].astype(BF16), hgrn_lb, hgrn_norm, conv_w[0], conv_b,
           _block_diag(lru_wa[0]).astype(BF16), lru_ba[0].reshape(1, D_B),
           _block_diag(lru_wx[0]).astype(BF16), lru_bx[0].reshape(1, D_B), lru_lambda, w_o[0].astype(BF16))

    xs = _ffn(x_sample.reshape(bs * ts, D_MODEL), *ffn1, name="ffn1_sample")
    xs, hg_s, lru_s, cv_s = _mixer_sample(xs.reshape(bs, ts, D_MODEL), *mix, state_hgrn[0], state_lru[0],
                                          state_conv[0])

    yp, ys, hg_p, lru_p, cv_p = _layer_prompt(x_prompt, xs, ffn1, mix, ffn2, fn)

    return (yp.reshape(bp, tp, D_MODEL), ys.reshape(bs, ts, D_MODEL),
            hg_p[None], lru_p.reshape(1, bp, D_B), cv_p[None],
            hg_s[None], lru_s[None], cv_s[None])
```

```python
import functools
import math

import jax
import jax.numpy as jnp
from jax import lax
from jax.experimental import pallas as pl
from jax.experimental.pallas import tpu as pltpu

F32 = jnp.float32
BF16 = jnp.bfloat16

D_MODEL = 1024
D_FF = 2816
D_A = 512
D_B = 512
N_HEADS = 4
D_HEAD = 128
D_IN = 4 * D_A + 2 * D_B
CONV_W = 4
LRU_C = 8.0
EPS = 1e-6

SUBLANES = 8
NEG_LARGE = -1e30
FFN_TILE = 512
FFN_CHUNK = 256
MIX_CHUNK = 256
LAYER_FFN_BOUNDS = tuple(range(0, D_FF + 1, 256))
LAYER_FILL_SLOTS = (4, 6, 6, 6, 6, 4, 6, 6)
LAYER_FILL_STEPS = 2
SAMPLE_GROUP = 8
VMEM_LIMIT_BYTES = 56 * 1024 * 1024
LAYER_VMEM_LIMIT_BYTES = 63 * 1024 * 1024

_NT = (((1,), (1,)), ((), ()))


def _dot(a, b):
    return jnp.dot(a, b, preferred_element_type=F32)


def _dot_nt(a, b):
    return lax.dot_general(a, b, _NT, preferred_element_type=F32)


def _rms(x, w):
    ms = jnp.mean(x * x, axis=-1, keepdims=True)
    return x * lax.rsqrt(ms + EPS) * w


def _sigmoid(x):
    return 1.0 / (1.0 + jnp.exp(-x))


def _log1p(x):
    u = 1.0 + x
    return jnp.where(u == 1.0, x, jnp.log(u) * x / (u - 1.0))


def _expm1(x):
    u = jnp.exp(x)
    return jnp.where(u == 1.0, x, (u - 1.0) * x / jnp.log(u))


def _softplus(x):
    return jnp.maximum(x, 0.0) + _log1p(jnp.exp(-jnp.abs(x)))


def _gelu_tanh(x):
    return 0.5 * x * (1.0 + jnp.tanh(math.sqrt(2.0 / math.pi) * (x + 0.044715 * (x * x * x))))


def _split3_bf16(x):
    x1 = x.astype(BF16)
    r1 = x - x1.astype(F32)
    x2 = r1.astype(BF16)
    r2 = r1 - x2.astype(F32)
    return x1, x2, r2.astype(BF16)


def _ffn_kernel(x_ref, nw_ref, wg_ref, wu_ref, wd_ref, o_ref):
    x = x_ref[...]
    xn = _rms(x, nw_ref[...]).astype(BF16)
    acc = jnp.zeros(x.shape, F32)
    for c in range(D_FF // FFN_CHUNK):
        sl = slice(c * FFN_CHUNK, (c + 1) * FFN_CHUNK)
        g = _dot(xn, wg_ref[:, sl])
        u = _dot(xn, wu_ref[:, sl])
        h = (g * _sigmoid(g) * u).astype(BF16)
        acc = acc + _dot(h, wd_ref[sl, :])
    o_ref[...] = x + 0.5 * acc


def _resident(shape):
    zeros = (0,) * len(shape)
    return pl.BlockSpec(shape, lambda *_: zeros, pipeline_mode=pl.Buffered(1))


def _ffn(x, norm_w, wg, wu, wd, *, name):
    n = x.shape[0]
    tm = min(FFN_TILE, n)
    assert n % tm == 0 and D_FF % FFN_CHUNK == 0
    row = pl.BlockSpec((tm, D_MODEL), lambda i: (i, 0))
    return pl.pallas_call(
        _ffn_kernel,
        out_shape=jax.ShapeDtypeStruct((n, D_MODEL), F32),
        grid=(n // tm,),
        in_specs=[row, _resident((1, D_MODEL)), _resident((D_MODEL, D_FF)), _resident((D_MODEL, D_FF)),
                  _resident((D_FF, D_MODEL))],
        out_specs=row,
        compiler_params=pltpu.CompilerParams(dimension_semantics=("arbitrary",),
                                             vmem_limit_bytes=VMEM_LIMIT_BYTES),
        name=name,
    )(x, norm_w, wg, wu, wd)


def _hgrn_gates(proj_ref, lb_ref):
    l0 = lb_ref[0:1, :]
    l1 = lb_ref[1:2, :]
    m = jnp.maximum(l0, l1)
    e0 = jnp.exp(l0 - m)
    e1 = jnp.exp(l1 - m)
    lb = e0 / (e0 + e1)
    q = proj_ref[:, 0:D_A]
    fz = proj_ref[:, D_A:2 * D_A]
    f = lb + (1.0 - lb) * _sigmoid(fz)
    return q * _sigmoid(q), 1.0 - f, jnp.log(f)


def _pairwise(qs, k, cum, v, kpad_ref, cpad_ref, vpad_ref, out_ref, block, n_rows):
    pad = kpad_ref.shape[0] - n_rows
    kpad_ref[pl.ds(pad, n_rows), :] = k
    cpad_ref[pl.ds(pad, n_rows), :] = cum
    vpad_ref[pl.ds(pad, n_rows), :] = v
    rmod = lax.broadcasted_iota(jnp.int32, (n_rows, 1), 0) & (block - 1)
    out = [jnp.zeros((n_rows, D_HEAD), F32) for _ in range(N_HEADS)]
    for d in range(block):
        if d == 0:
            p = qs * k
            vs = v
        else:
            ks = kpad_ref[pl.ds(pad - d, n_rows), :]
            cs = cpad_ref[pl.ds(pad - d, n_rows), :]
            vs = vpad_ref[pl.ds(pad - d, n_rows), :]
            p = qs * ks * jnp.exp(cum - cs)
        valid = rmod >= d
        for h in range(N_HEADS):
            hs = slice(h * D_HEAD, (h + 1) * D_HEAD)
            s = jnp.sum(p[:, hs], axis=-1, keepdims=True)
            s = jnp.where(valid, s, 0.0)
            out[h] = out[h] + s * vs[:, hs]
    for h in range(N_HEADS):
        out_ref[:, h * D_HEAD:(h + 1) * D_HEAD] = out[h]


def _lru_coeffs(xc, wa_ref, ba_ref, wx_ref, bx_ref, lam_ref):
    xcb = xc.astype(BF16)
    r = _sigmoid(_dot(xcb, wa_ref[...]) + ba_ref[...])
    i = _sigmoid(_dot(xcb, wx_ref[...]) + bx_ref[...])
    log_a = -LRU_C * r * _softplus(-lam_ref[...])
    a = jnp.exp(log_a)
    mult = jnp.sqrt(-_expm1(2.0 * log_a))
    return a, mult, i


def _scan_rows(a, u, rowpos, span):
    s = 1
    while s < span:
        keep = rowpos >= s
        a_sh = jnp.where(keep, pltpu.roll(a, s, 0), 1.0)
        u_sh = jnp.where(keep, pltpu.roll(u, s, 0), 0.0)
        u = a * u_sh + u
        a = a * a_sh
        s *= 2
    return a, u


def _row_bcast(ref, h, r, n):
    return jnp.broadcast_to(ref[h, pl.ds(r, 1), :], (n, D_HEAD))


def _pairwise_tile(qs_ref, k_ref, cum_ref, v_ref, out_ref, h, n_rows):
    pos = lax.broadcasted_iota(jnp.int32, (SUBLANES, 1), 0)
    for j in range(n_rows // SUBLANES):
        rows = pl.ds(SUBLANES * j, SUBLANES)
        q_t = qs_ref[h, rows, :]
        c_t = cum_ref[h, rows, :]
        acc = jnp.zeros((SUBLANES, D_HEAD), F32)
        for s in range(SUBLANES):
            src = SUBLANES * j + s
            d = c_t - _row_bcast(cum_ref, h, src, SUBLANES)
            if s > 0:
                d = jnp.where(pos >= s, d, NEG_LARGE)
            p = q_t * _row_bcast(k_ref, h, src, SUBLANES) * jnp.exp(d)
            acc = acc + jnp.sum(p, axis=-1, keepdims=True) * _row_bcast(v_ref, h, src, SUBLANES)
        out_ref[h, rows, :] = acc


def _level_operands(qs_ref, k_ref, cum_ref, h, n_rows, L):
    half = L // 2
    zeros = jnp.zeros((half, D_HEAD), F32)
    qparts, kparts = [], []
    for b in range(n_rows // L):
        r0 = b * L
        first = pl.ds(r0, half)
        second = pl.ds(r0 + half, half)
        mid = _row_bcast(cum_ref, h, r0 + half - 1, half)
        kparts += [k_ref[h, first, :] * jnp.exp(mid - cum_ref[h, first, :]), zeros]
        qparts += [zeros, qs_ref[h, second, :] * jnp.exp(cum_ref[h, second, :] - mid)]
    return jnp.concatenate(qparts, axis=0).astype(BF16), jnp.concatenate(kparts, axis=0).astype(BF16)


def _scan_tiles(a, u, carry):
    n = a.shape[0]
    a3 = a.reshape(n // SUBLANES, SUBLANES, D_B)
    u3 = u.reshape(n // SUBLANES, SUBLANES, D_B)
    pos = lax.broadcasted_iota(jnp.int32, (1, SUBLANES, 1), 1)
    s = 1
    while s < SUBLANES:
        keep = pos >= s
        a_sh = jnp.where(keep, pltpu.roll(a3, s, 1), 1.0)
        u_sh = jnp.where(keep, pltpu.roll(u3, s, 1), 0.0)
        u3 = a3 * u_sh + u3
        a3 = a3 * a_sh
        s *= 2
    tiles = []
    for j in range(n // SUBLANES):
        hj = u3[j] + a3[j] * carry
        tiles.append(hj)
        carry = hj[SUBLANES - 1:SUBLANES, :]
    return jnp.concatenate(tiles, axis=0)


class _SwigluChunks:
    def __init__(self, x, nw_ref, wg_ref, wu_ref, wd_ref):
        self.x = x
        self.xn = _rms(x, nw_ref[...]).astype(BF16)
        self.wg_ref, self.wu_ref, self.wd_ref = wg_ref, wu_ref, wd_ref
        self.hidden = {}
        self.acc = None

    @staticmethod
    def _cols(c):
        return slice(LAYER_FFN_BOUNDS[c], LAYER_FFN_BOUNDS[c + 1])

    def up(self, c):
        g = _dot(self.xn, self.wg_ref[:, self._cols(c)])
        u = _dot(self.xn, self.wu_ref[:, self._cols(c)])
        self.hidden[c] = (g * _sigmoid(g) * u).astype(BF16)

    def down(self, c):
        d = _dot(self.hidden.pop(c), self.wd_ref[self._cols(c), :])
        self.acc = d if self.acc is None else self.acc + d

    def result(self):
        assert not self.hidden
        return self.x + 0.5 * self.acc

    def pieces(self):
        n = len(LAYER_FFN_BOUNDS) - 1
        out = [functools.partial(self.up, 0)]
        for c in range(1, n):
            out += [functools.partial(self.up, c), functools.partial(self.down, c - 1)]
        return out + [functools.partial(self.down, n - 1)]


def _layer_prompt_kernel(x_ref, xs_ref, n1_ref, wg1_ref, wu1_ref, wd1_ref,
                         mn_ref, win_ref, lb_ref, hn_ref, cw_ref, cb_ref, wa_ref, ba_ref, wx_ref,
                         bx_ref, lam_ref, wo_ref,
                         n2_ref, wg2_ref, wu2_ref, wd2_ref, fn_ref,
                         y_ref, ys_ref, shg_ref, slru_ref, scv_ref,
                         x1_ref, x2_ref, proj_ref, st_ref, h_ref, xb_ref, qs_ref, k_ref, cum_ref, v_ref,
                         opw_ref, obuf_ref, *, chunk, chunks_per_seq, n_chunks):
    C = chunk
    s = pl.program_id(0)

    @pl.when(s == 0)
    def _():
        x1_ref[...] = jnp.zeros(x1_ref.shape, F32)
        st_ref[...] = jnp.zeros(st_ref.shape, F32)
        h_ref[...] = jnp.zeros(h_ref.shape, F32)
        xb_ref[0:8, :] = jnp.zeros((8, D_B), F32)

    @pl.when(s < LAYER_FILL_STEPS)
    def _():
        x2_ref[...] = xs_ref[...]

    ffn2 = _SwigluChunks(x2_ref[...], n2_ref, wg2_ref, wu2_ref, wd2_ref)
    ffn1 = _SwigluChunks(x_ref[...], n1_ref, wg1_ref, wu1_ref, wd1_ref)

    pos_in_seq = lax.rem(s + (chunks_per_seq - 1), chunks_per_seq)
    first = pos_in_seq == 0
    keep = jnp.where(first, 0.0, 1.0)
    x = x1_ref[...]
    xn = _rms(x, mn_ref[...]).astype(BF16)
    proj_ref[...] = _dot(xn, win_ref[...])
    fill = ffn2.pieces() + ffn1.pieces()
    slots = iter(LAYER_FILL_SLOTS)

    def run_fill():
        for _ in range(next(slots)):
            fill.pop(0)()

    run_fill()

    qs, k, g = _hgrn_gates(proj_ref, lb_ref)
    row = lax.broadcasted_iota(jnp.int32, (C, C), 0)
    col = lax.broadcasted_iota(jnp.int32, (C, C), 1)
    tri = jnp.where(col <= row, 1.0, 0.0).astype(BF16)
    g1, g2, g3 = _split3_bf16(g)
    cum = _dot(tri, g1) + _dot(tri, g2) + _dot(tri, g3)
    for h in range(N_HEADS):
        hs = slice(h * D_HEAD, (h + 1) * D_HEAD)
        qs_ref[h] = qs[:, hs]
        k_ref[h] = k[:, hs]
        cum_ref[h] = cum[:, hs]
        v_ref[h] = proj_ref[:, 2 * D_A + h * D_HEAD:2 * D_A + (h + 1) * D_HEAD]

    levels = []
    L = 2 * SUBLANES
    while L <= C:
        shift = L.bit_length() - 1
        levels.append((L, None if L == C else (row >> shift) == (col >> shift)))
        L *= 2

    gate = proj_ref[:, 3 * D_A:4 * D_A]
    for h in range(N_HEADS):
        hs = slice(h * D_HEAD, (h + 1) * D_HEAD)
        run_fill()
        _pairwise_tile(qs_ref, k_ref, cum_ref, v_ref, opw_ref, h, C)
        scores = jnp.zeros((C, C), F32)
        for L, same in levels:
            qt, kt = _level_operands(qs_ref, k_ref, cum_ref, h, C, L)
            sc = _dot_nt(qt, kt)
            scores = scores + (sc if same is None else jnp.where(same, sc, 0.0))
        cum_h = cum_ref[h]
        cl = _row_bcast(cum_ref, h, C - 1, C)
        qbar = (qs_ref[h] * jnp.exp(cum_h)).astype(BF16)
        khat = (k_ref[h] * jnp.exp(cl - cum_h)).astype(BF16)
        dec = jnp.exp(cum_ref[h, pl.ds(C - 1, 1), :])
        v_h = v_ref[h]
        st = st_ref[h] * keep
        o = _dot(scores.astype(BF16), v_h.astype(BF16)) + _dot_nt(qbar, st.astype(BF16)) + opw_ref[h]
        st_ref[h] = st * dec + _dot(v_h.T.astype(BF16), khat)
        ms = jnp.mean(o * o, axis=-1, keepdims=True)
        obuf_ref[:, hs] = (o * lax.rsqrt(ms + EPS) * hn_ref[:, hs] * _sigmoid(gate[:, hs])).astype(BF16)

    run_fill()
    y_ref[...] = _rms(ffn2.result(), fn_ref[...])

    rowi = lax.broadcasted_iota(jnp.int32, (C, 1), 0)
    xb = proj_ref[:, 4 * D_A:4 * D_A + D_B]
    yb = proj_ref[:, 4 * D_A + D_B:D_IN]
    xb_ref[0:8, :] = xb_ref[0:8, :] * keep
    xb_ref[pl.ds(8, C), :] = xb
    xc = (cb_ref[...] + cw_ref[3:4, :] * xb + cw_ref[2:3, :] * xb_ref[pl.ds(7, C), :]
          + cw_ref[1:2, :] * xb_ref[pl.ds(6, C), :] + cw_ref[0:1, :] * xb_ref[pl.ds(5, C), :])
    xb_ref[0:8, :] = xb[C - 8:C, :]
    a, mult, gate_i = _lru_coeffs(xc, wa_ref, ba_ref, wx_ref, bx_ref, lam_ref)
    run_fill()
    mult = jnp.where((rowi == 0) & first, 1.0, mult)
    hseq = _scan_tiles(a, mult * gate_i * xc, h_ref[...] * keep)
    h_ref[...] = hseq[C - 1:C, :]
    obuf_ref[:, D_A:D_A + D_B] = (hseq * _gelu_tanh(yb)).astype(BF16)

    x2_ref[...] = x + _dot(obuf_ref[...], wo_ref[...])
    run_fill()
    assert not fill and next(slots, None) is None
    x1_ref[...] = ffn1.result()

    @pl.when(s < LAYER_FILL_STEPS)
    def _():
        ys_ref[...] = y_ref[...]

    @pl.when((pos_in_seq == chunks_per_seq - 1) & (s >= 1) & (s <= n_chunks))
    def _():
        for h in range(N_HEADS):
            shg_ref[0, h] = st_ref[h].T
        slru_ref[0] = hseq[C - 1:C, :]
        scv_ref[0] = xb[C - (CONV_W - 1):C, :]


def _layer_prompt(x, xs, ffn1, mix, ffn2, fn):
    b, t, _ = x.shape
    C = MIX_CHUNK
    assert t % C == 0 and C % (2 * SUBLANES) == 0 and C & (C - 1) == 0
    assert LAYER_FFN_BOUNDS[0] == 0 and LAYER_FFN_BOUNDS[-1] == D_FF
    assert sum(LAYER_FILL_SLOTS) == 4 * (len(LAYER_FFN_BOUNDS) - 1) and len(LAYER_FILL_SLOTS) == N_HEADS + 4
    assert xs.shape == (LAYER_FILL_STEPS * C, D_MODEL), "sample rows must exactly fill the pipeline fill steps"
    nt = t // C
    n_chunks = b * nt
    rows_in = pl.BlockSpec((C, D_MODEL), lambda s: (jnp.minimum(s, n_chunks - 1), 0))
    rows_out = pl.BlockSpec((C, D_MODEL), lambda s: (jnp.clip(s - 2, 0, n_chunks - 1), 0))
    fill_index = lambda s: (jnp.minimum(s, LAYER_FILL_STEPS - 1), 0)
    rows_fill = pl.BlockSpec((C, D_MODEL), fill_index)
    rows_fill_in = pl.BlockSpec((C, D_MODEL), fill_index, pipeline_mode=pl.Buffered(1))
    seq_of = lambda s: jnp.clip((s - 1) // nt, 0, b - 1)
    vec = lambda n: _resident((1, n))
    ffn_specs = [vec(D_MODEL), _resident((D_MODEL, D_FF)), _resident((D_MODEL, D_FF)), _resident((D_FF, D_MODEL))]
    mix_specs = [vec(D_MODEL), _resident((D_MODEL, D_IN)), _resident((2, D_A)), vec(D_A),
                 _resident((CONV_W, D_B)), vec(D_B), _resident((D_B, D_B)), vec(D_B), _resident((D_B, D_B)),
                 vec(D_B), vec(D_B), _resident((D_A + D_B, D_MODEL))]
    out_shape = (jax.ShapeDtypeStruct((b * t, D_MODEL), F32),
                 jax.ShapeDtypeStruct(xs.shape, F32),
                 jax.ShapeDtypeStruct((b, N_HEADS, D_HEAD, D_HEAD), F32),
                 jax.ShapeDtypeStruct((b, 1, D_B), F32),
                 jax.ShapeDtypeStruct((b, CONV_W - 1, D_B), F32))
    out_specs = (rows_out, rows_fill,
                 pl.BlockSpec((1, N_HEADS, D_HEAD, D_HEAD), lambda s: (seq_of(s), 0, 0, 0)),
                 pl.BlockSpec((1, 1, D_B), lambda s: (seq_of(s), 0, 0)),
                 pl.BlockSpec((1, CONV_W - 1, D_B), lambda s: (seq_of(s), 0, 0)))
    per_head = pltpu.VMEM((N_HEADS, C, D_HEAD), F32)
    scratch = [pltpu.VMEM((C, D_MODEL), F32),
               pltpu.VMEM((C, D_MODEL), F32),
               pltpu.VMEM((C, D_IN), F32),
               pltpu.VMEM((N_HEADS, D_HEAD, D_HEAD), F32),
               pltpu.VMEM((1, D_B), F32),
               pltpu.VMEM((C + 8, D_B), F32),
               per_head, per_head, per_head, per_head,
               per_head,
               pltpu.VMEM((C, D_A + D_B), BF16)]
    return pl.pallas_call(
        functools.partial(_layer_prompt_kernel, chunk=C, chunks_per_seq=nt, n_chunks=n_chunks),
        out_shape=out_shape,
        grid=(n_chunks + LAYER_FILL_STEPS,),
        in_specs=[rows_in, rows_fill_in] + ffn_specs + mix_specs + ffn_specs + [vec(D_MODEL)],
        out_specs=out_specs,
        scratch_shapes=scratch,
        compiler_params=pltpu.CompilerParams(dimension_semantics=("arbitrary",),
                                             vmem_limit_bytes=LAYER_VMEM_LIMIT_BYTES),
        name="layer_prompt",
    )(x.reshape(b * t, D_MODEL), xs, *ffn1, *mix, *ffn2, fn)


def _mixer_sample_kernel(x_ref, mn_ref, win_ref, lb_ref, hn_ref, cw_ref, cb_ref, wa_ref, ba_ref, wx_ref,
                         bx_ref, lam_ref, wo_ref, s0_ref, h0_ref, hist1_ref, hist2_ref, hist3_ref,
                         y_ref, s1_ref, hseq_ref, xb_out_ref,
                         proj_ref, xb_ref, kpad_ref, cpad_ref, vpad_ref, qbar_ref, khat_ref, ecum_ref,
                         opw_ref, ost_ref, ob_ref, *, steps):
    T = steps
    N = x_ref.shape[0]
    G = s0_ref.shape[0]
    R = G * T
    i = pl.program_id(0)
    tshift = T.bit_length() - 1

    @pl.when(i == 0)
    def _():
        pad = kpad_ref.shape[0] - N
        zpad = jnp.zeros((pad, D_A), F32)
        kpad_ref[0:pad, :] = zpad
        cpad_ref[0:pad, :] = zpad
        vpad_ref[0:pad, :] = zpad
        xb_ref[0:8, :] = jnp.zeros((8, D_B), F32)

        xn = _rms(x_ref[...], mn_ref[...]).astype(BF16)
        proj_ref[...] = _dot(xn, win_ref[...])

        qs, k, g = _hgrn_gates(proj_ref, lb_ref)
        v = proj_ref[:, 2 * D_A:3 * D_A]
        tpos = lax.broadcasted_iota(jnp.int32, (N, 1), 0) & (T - 1)
        cum = g
        tail = jnp.zeros_like(g)
        for d in range(1, T):
            cum = cum + jnp.where(tpos >= d, pltpu.roll(g, d, 0), 0.0)
            tail = tail + jnp.where(tpos < T - d, pltpu.roll(g, N - d, 0), 0.0)
        _pairwise(qs, k, cum, v, kpad_ref, cpad_ref, vpad_ref, opw_ref, T, N)
        ecum = jnp.exp(cum)
        ecum_ref[...] = ecum
        qbar_ref[...] = qs * ecum
        khat_ref[...] = k * jnp.exp(tail)

        xb = proj_ref[:, 4 * D_A:4 * D_A + D_B]
        yb = proj_ref[:, 4 * D_A + D_B:D_IN]
        xb_ref[pl.ds(8, N), :] = xb
        xb_out_ref[...] = xb
        hist = (hist1_ref, hist2_ref, hist3_ref)
        xc = cb_ref[...] + cw_ref[CONV_W - 1:CONV_W, :] * xb
        for d in range(1, CONV_W):
            prev = jnp.where(tpos >= d, xb_ref[pl.ds(8 - d, N), :], hist[d - 1][...])
            xc = xc + cw_ref[CONV_W - 1 - d:CONV_W - d, :] * prev
        a, mult, gate_i = _lru_coeffs(xc, wa_ref, ba_ref, wx_ref, bx_ref, lam_ref)
        a_run, h_run = _scan_rows(a, mult * gate_i * xc, tpos, T)
        hseq = h_run + a_run * h0_ref[...]
        hseq_ref[...] = hseq
        ob_ref[...] = hseq * _gelu_tanh(yb)

    r0 = pl.multiple_of(i * R, R)
    rows = pl.ds(r0, R)
    row8 = lax.broadcasted_iota(jnp.int32, (8, 1), 0)
    seq_of_row = lax.broadcasted_iota(jnp.int32, (R, 1), 0) >> tshift
    for h in range(N_HEADS):
        hs = slice(h * D_HEAD, (h + 1) * D_HEAD)
        khat_t = khat_ref[rows, hs].T.astype(BF16)
        dec_t = ecum_ref[rows, hs].T
        v_g = proj_ref[rows, 2 * D_A + h * D_HEAD:2 * D_A + (h + 1) * D_HEAD]
        for p in range(R // 8):
            tile = pl.ds(pl.multiple_of(r0 + 8 * p, 8), 8)
            q8 = qbar_ref[tile, hs].astype(BF16)
            o8 = jnp.zeros((8, D_HEAD), F32)
            for j in range(8 // T):
                b = p * (8 // T) + j
                s0 = s0_ref[b, h]
                ob = _dot(q8, s0.astype(BF16))
                o8 = jnp.where((row8 >= j * T) & (row8 < (j + 1) * T), ob, o8)
                v_b = jnp.where(seq_of_row == b, v_g, 0.0).astype(BF16)
                s1_ref[b, h] = s0 * dec_t[:, b * T + T - 1:b * T + T] + _dot(khat_t, v_b)
            ost_ref[tile, hs] = o8

    @pl.when(i == pl.num_programs(0) - 1)
    def _():
        gate = proj_ref[:, 3 * D_A:4 * D_A]
        outs = []
        for h in range(N_HEADS):
            hs = slice(h * D_HEAD, (h + 1) * D_HEAD)
            o = opw_ref[:, hs] + ost_ref[:, hs]
            ms = jnp.mean(o * o, axis=-1, keepdims=True)
            outs.append(o * lax.rsqrt(ms + EPS) * hn_ref[:, hs] * _sigmoid(gate[:, hs]))
        o = jnp.concatenate(outs + [ob_ref[...]], axis=-1).astype(BF16)
        y_ref[...] = x_ref[...] + _dot(o, wo_ref[...])


def _mixer_sample(x, mn, win, lb, hn, cw, cb, wa, ba, wx, bx, lam, wo, s0, h0, buf):
    nb, T, _ = x.shape
    assert T == CONV_W and 8 % T == 0, "sample kernel assumes DEC_SEQ == CONV_W == 4"
    G = SAMPLE_GROUP
    N = nb * T
    assert nb % G == 0 and (G * T) % 8 == 0
    xr = x.reshape(N, D_MODEL)
    h0x = jnp.repeat(h0, T, axis=0)
    hists = [jnp.concatenate([buf[:, CONV_W - 1 - d:, :], jnp.zeros((nb, T - d, D_B), F32)], axis=1)
             .reshape(N, D_B) for d in range(1, CONV_W)]
    whole = lambda n: _resident((N, n))
    whole_out = lambda n: pl.BlockSpec((N, n), lambda i: (0, 0))
    state = pl.BlockSpec((G, N_HEADS, D_HEAD, D_HEAD), lambda i: (i, 0, 0, 0))
    vec = lambda n: _resident((1, n))
    out_shape = (jax.ShapeDtypeStruct((N, D_MODEL), F32),
                 jax.ShapeDtypeStruct((nb, N_HEADS, D_HEAD, D_HEAD), F32),
                 jax.ShapeDtypeStruct((N, D_B), F32),
                 jax.ShapeDtypeStruct((N, D_B), F32))
    half = lambda: pltpu.VMEM((N, D_A), F32)
    scratch = [pltpu.VMEM((N, D_IN), F32),
               pltpu.VMEM((N + 8, D_B), F32),
               pltpu.VMEM((N + 8, D_A), F32), pltpu.VMEM((N + 8, D_A), F32), pltpu.VMEM((N + 8, D_A), F32),
               half(), half(), half(),
               half(), half(),
               pltpu.VMEM((N, D_B), F32)]
    y, s1, hseq, xb = pl.pallas_call(
        functools.partial(_mixer_sample_kernel, steps=T),
        out_shape=out_shape,
        grid=(nb // G,),
        in_specs=[whole(D_MODEL), vec(D_MODEL), _resident((D_MODEL, D_IN)), _resident((2, D_A)), vec(D_A),
                  _resident((CONV_W, D_B)), vec(D_B), _resident((D_B, D_B)), vec(D_B), _resident((D_B, D_B)),
                  vec(D_B), vec(D_B), _resident((D_A + D_B, D_MODEL)),
                  state, whole(D_B), whole(D_B), whole(D_B), whole(D_B)],
        out_specs=(whole_out(D_MODEL), state, whole_out(D_B), whole_out(D_B)),
        scratch_shapes=scratch,
        compiler_params=pltpu.CompilerParams(dimension_semantics=("arbitrary",),
                                             vmem_limit_bytes=LAYER_VMEM_LIMIT_BYTES),
        name="mixer_sample",
    )(xr, mn, win, lb, hn, cw, cb, wa, ba, wx, bx, lam, wo, s0, h0x, *hists)
    hseq = hseq.reshape(nb, T, D_B)
    xb = xb.reshape(nb, T, D_B)
    return y, s1, hseq[:, T - 1], xb[:, T - (CONV_W - 1):]


def _block_diag(w):
    n, bi, bj = w.shape
    eye = jnp.eye(n, dtype=w.dtype)
    return (w[:, :, None, :] * eye[:, None, :, None]).reshape(n * bi, n * bj)


def kernel(x_prompt, x_sample, state_hgrn, state_lru, state_conv, ffn1_norm, ffn1_wg, ffn1_wu, ffn1_wd,
           mix_norm, w_in, hgrn_lb, hgrn_norm, conv_w, conv_b, lru_wa, lru_ba, lru_wx, lru_bx, lru_lambda,
           w_o, ffn2_norm, ffn2_wg, ffn2_wu, ffn2_wd, final_norm):
    assert ffn1_norm.shape[0] == 1 and hgrn_lb.shape[0] == 2, "single-layer model"
    bp, tp, _ = x_prompt.shape
    bs, ts, _ = x_sample.shape

    ffn1 = (ffn1_norm, ffn1_wg[0].astype(BF16), ffn1_wu[0].astype(BF16), ffn1_wd[0].astype(BF16))
    ffn2 = (ffn2_norm, ffn2_wg[0].astype(BF16), ffn2_wu[0].astype(BF16), ffn2_wd[0].astype(BF16))
    fn = final_norm.reshape(1, D_MODEL)
    mix = (mix_norm, w_in[0].astype(BF16), hgrn_lb, hgrn_norm, conv_w[0], conv_b,
           _block_diag(lru_wa[0]).astype(BF16), lru_ba[0].reshape(1, D_B),
           _block_diag(lru_wx[0]).astype(BF16), lru_bx[0].reshape(1, D_B), lru_lambda, w_o[0].astype(BF16))

    xs = _ffn(x_sample.reshape(bs * ts, D_MODEL), *ffn1, name="ffn1_sample")
    xs, hg_s, lru_s, cv_s = _mixer_sample(xs.reshape(bs, ts, D_MODEL), *mix, state_hgrn[0], state_lru[0],
                                          state_conv[0])

    yp, ys, hg_p, lru_p, cv_p = _layer_prompt(x_prompt, xs, ffn1, mix, ffn2, fn)

    return (yp.reshape(bp, tp, D_MODEL), ys.reshape(bs, ts, D_MODEL),
            hg_p[None], lru_p.reshape(1, bp, D_B), cv_p[None],
            hg_s[None], lru_s[None], cv_s[None])
```

```python
import functools
import math

import jax
import jax.numpy as jnp
from jax import lax
from jax.experimental import pallas as pl
from jax.experimental.pallas import tpu as pltpu

F32 = jnp.float32
BF16 = jnp.bfloat16

D_MODEL = 1024
D_FF = 2816
D_A = 512
D_B = 512
N_HEADS = 4
D_HEAD = 128
D_IN = 4 * D_A + 2 * D_B
CONV_W = 4
LRU_C = 8.0
EPS = 1e-6

SUBLANES = 8
NEG_LARGE = -1e30
FFN_TILE = 512
FFN_CHUNK = 256
MIX_CHUNK = 256
LAYER_FFN_BOUNDS = (0, 768, 1536, 2304, 2816)
LAYER_FILL_SLOTS = (1, 2, 2, 2, 2, 2, 2, 3)
LAYER_FILL_STEPS = 2
SAMPLE_GROUP = 8
VMEM_LIMIT_BYTES = 56 * 1024 * 1024
LAYER_VMEM_LIMIT_BYTES = 63 * 1024 * 1024

_NT = (((1,), (1,)), ((), ()))


def _dot(a, b):
    return jnp.dot(a, b, preferred_element_type=F32)


def _dot_nt(a, b):
    return lax.dot_general(a, b, _NT, preferred_element_type=F32)


def _rms(x, w):
    ms = jnp.mean(x * x, axis=-1, keepdims=True)
    return x * lax.rsqrt(ms + EPS) * w


def _sigmoid(x):
    return 1.0 / (1.0 + jnp.exp(-x))


def _log1p(x):
    u = 1.0 + x
    return jnp.where(u == 1.0, x, jnp.log(u) * x / (u - 1.0))


def _expm1(x):
    u = jnp.exp(x)
    return jnp.where(u == 1.0, x, (u - 1.0) * x / jnp.log(u))


def _softplus(x):
    return jnp.maximum(x, 0.0) + _log1p(jnp.exp(-jnp.abs(x)))


def _gelu_tanh(x):
    return 0.5 * x * (1.0 + jnp.tanh(math.sqrt(2.0 / math.pi) * (x + 0.044715 * (x * x * x))))


def _split3_bf16(x):
    x1 = x.astype(BF16)
    r1 = x - x1.astype(F32)
    x2 = r1.astype(BF16)
    r2 = r1 - x2.astype(F32)
    return x1, x2, r2.astype(BF16)


def _ffn_kernel(x_ref, nw_ref, wg_ref, wu_ref, wd_ref, o_ref):
    x = x_ref[...]
    xn = _rms(x, nw_ref[...]).astype(BF16)
    acc = jnp.zeros(x.shape, F32)
    for c in range(D_FF // FFN_CHUNK):
        sl = slice(c * FFN_CHUNK, (c + 1) * FFN_CHUNK)
        g = _dot(xn, wg_ref[:, sl])
        u = _dot(xn, wu_ref[:, sl])
        h = (g * _sigmoid(g) * u).astype(BF16)
        acc = acc + _dot(h, wd_ref[sl, :])
    o_ref[...] = x + 0.5 * acc


def _resident(shape):
    zeros = (0,) * len(shape)
    return pl.BlockSpec(shape, lambda *_: zeros, pipeline_mode=pl.Buffered(1))


def _ffn(x, norm_w, wg, wu, wd, *, name):
    n = x.shape[0]
    tm = min(FFN_TILE, n)
    assert n % tm == 0 and D_FF % FFN_CHUNK == 0
    row = pl.BlockSpec((tm, D_MODEL), lambda i: (i, 0))
    return pl.pallas_call(
        _ffn_kernel,
        out_shape=jax.ShapeDtypeStruct((n, D_MODEL), F32),
        grid=(n // tm,),
        in_specs=[row, _resident((1, D_MODEL)), _resident((D_MODEL, D_FF)), _resident((D_MODEL, D_FF)),
                  _resident((D_FF, D_MODEL))],
        out_specs=row,
        compiler_params=pltpu.CompilerParams(dimension_semantics=("arbitrary",),
                                             vmem_limit_bytes=VMEM_LIMIT_BYTES),
        name=name,
    )(x, norm_w, wg, wu, wd)


def _hgrn_gates(proj_ref, lb_ref):
    l0 = lb_ref[0:1, :]
    l1 = lb_ref[1:2, :]
    m = jnp.maximum(l0, l1)
    e0 = jnp.exp(l0 - m)
    e1 = jnp.exp(l1 - m)
    lb = e0 / (e0 + e1)
    q = proj_ref[:, 0:D_A]
    fz = proj_ref[:, D_A:2 * D_A]
    f = lb + (1.0 - lb) * _sigmoid(fz)
    return q * _sigmoid(q), 1.0 - f, jnp.log(f)


def _pairwise(qs, k, cum, v, kpad_ref, cpad_ref, vpad_ref, out_ref, block, n_rows):
    pad = kpad_ref.shape[0] - n_rows
    kpad_ref[pl.ds(pad, n_rows), :] = k
    cpad_ref[pl.ds(pad, n_rows), :] = cum
    vpad_ref[pl.ds(pad, n_rows), :] = v
    rmod = lax.broadcasted_iota(jnp.int32, (n_rows, 1), 0) & (block - 1)
    out = [jnp.zeros((n_rows, D_HEAD), F32) for _ in range(N_HEADS)]
    for d in range(block):
        if d == 0:
            p = qs * k
            vs = v
        else:
            ks = kpad_ref[pl.ds(pad - d, n_rows), :]
            cs = cpad_ref[pl.ds(pad - d, n_rows), :]
            vs = vpad_ref[pl.ds(pad - d, n_rows), :]
            p = qs * ks * jnp.exp(cum - cs)
        valid = rmod >= d
        for h in range(N_HEADS):
            hs = slice(h * D_HEAD, (h + 1) * D_HEAD)
            s = jnp.sum(p[:, hs], axis=-1, keepdims=True)
            s = jnp.where(valid, s, 0.0)
            out[h] = out[h] + s * vs[:, hs]
    for h in range(N_HEADS):
        out_ref[:, h * D_HEAD:(h + 1) * D_HEAD] = out[h]


def _lru_coeffs(xc, wa_ref, ba_ref, wx_ref, bx_ref, lam_ref):
    xcb = xc.astype(BF16)
    r = _sigmoid(_dot(xcb, wa_ref[...]) + ba_ref[...])
    i = _sigmoid(_dot(xcb, wx_ref[...]) + bx_ref[...])
    log_a = -LRU_C * r * _softplus(-lam_ref[...])
    a = jnp.exp(log_a)
    mult = jnp.sqrt(-_expm1(2.0 * log_a))
    return a, mult, i


def _scan_rows(a, u, rowpos, span):
    s = 1
    while s < span:
        keep = rowpos >= s
        a_sh = jnp.where(keep, pltpu.roll(a, s, 0), 1.0)
        u_sh = jnp.where(keep, pltpu.roll(u, s, 0), 0.0)
        u = a * u_sh + u
        a = a * a_sh
        s *= 2
    return a, u


def _row_bcast(ref, h, r, n):
    return jnp.broadcast_to(ref[h, pl.ds(r, 1), :], (n, D_HEAD))


def _pairwise_tile(qs_ref, k_ref, cum_ref, v_ref, out_ref, h, n_rows):
    pos = lax.broadcasted_iota(jnp.int32, (SUBLANES, 1), 0)
    for j in range(n_rows // SUBLANES):
        rows = pl.ds(SUBLANES * j, SUBLANES)
        q_t = qs_ref[h, rows, :]
        c_t = cum_ref[h, rows, :]
        acc = jnp.zeros((SUBLANES, D_HEAD), F32)
        for s in range(SUBLANES):
            src = SUBLANES * j + s
            d = c_t - _row_bcast(cum_ref, h, src, SUBLANES)
            if s > 0:
                d = jnp.where(pos >= s, d, NEG_LARGE)
            p = q_t * _row_bcast(k_ref, h, src, SUBLANES) * jnp.exp(d)
            acc = acc + jnp.sum(p, axis=-1, keepdims=True) * _row_bcast(v_ref, h, src, SUBLANES)
        out_ref[h, rows, :] = acc


def _level_operands(qs_ref, k_ref, cum_ref, h, n_rows, L):
    half = L // 2
    zeros = jnp.zeros((half, D_HEAD), F32)
    qparts, kparts = [], []
    for b in range(n_rows // L):
        r0 = b * L
        first = pl.ds(r0, half)
        second = pl.ds(r0 + half, half)
        mid = _row_bcast(cum_ref, h, r0 + half - 1, half)
        kparts += [k_ref[h, first, :] * jnp.exp(mid - cum_ref[h, first, :]), zeros]
        qparts += [zeros, qs_ref[h, second, :] * jnp.exp(cum_ref[h, second, :] - mid)]
    return jnp.concatenate(qparts, axis=0).astype(BF16), jnp.concatenate(kparts, axis=0).astype(BF16)


def _scan_tiles(a, u, carry):
    n = a.shape[0]
    a3 = a.reshape(n // SUBLANES, SUBLANES, D_B)
    u3 = u.reshape(n // SUBLANES, SUBLANES, D_B)
    pos = lax.broadcasted_iota(jnp.int32, (1, SUBLANES, 1), 1)
    s = 1
    while s < SUBLANES:
        keep = pos >= s
        a_sh = jnp.where(keep, pltpu.roll(a3, s, 1), 1.0)
        u_sh = jnp.where(keep, pltpu.roll(u3, s, 1), 0.0)
        u3 = a3 * u_sh + u3
        a3 = a3 * a_sh
        s *= 2
    tiles = []
    for j in range(n // SUBLANES):
        hj = u3[j] + a3[j] * carry
        tiles.append(hj)
        carry = hj[SUBLANES - 1:SUBLANES, :]
    return jnp.concatenate(tiles, axis=0)


class _SwigluChunks:
    def __init__(self, x, nw_ref, wg_ref, wu_ref, wd_ref):
        self.x = x
        self.xn = _rms(x, nw_ref[...]).astype(BF16)
        self.wg_ref, self.wu_ref, self.wd_ref = wg_ref, wu_ref, wd_ref
        self.hidden = {}
        self.acc = None

    @staticmethod
    def _cols(c):
        return slice(LAYER_FFN_BOUNDS[c], LAYER_FFN_BOUNDS[c + 1])

    def up(self, c):
        g = _dot(self.xn, self.wg_ref[:, self._cols(c)])
        u = _dot(self.xn, self.wu_ref[:, self._cols(c)])
        self.hidden[c] = (g * _sigmoid(g) * u).astype(BF16)

    def down(self, c):
        d = _dot(self.hidden.pop(c), self.wd_ref[self._cols(c), :])
        self.acc = d if self.acc is None else self.acc + d

    def result(self):
        assert not self.hidden
        return self.x + 0.5 * self.acc

    def pieces(self):
        n = len(LAYER_FFN_BOUNDS) - 1
        out = [functools.partial(self.up, 0)]
        for c in range(1, n):
            out += [functools.partial(self.up, c), functools.partial(self.down, c - 1)]
        return out + [functools.partial(self.down, n - 1)]


def _layer_prompt_kernel(x_ref, xs_ref, n1_ref, wg1_ref, wu1_ref, wd1_ref,
                         mn_ref, win_ref, lb_ref, hn_ref, cw_ref, cb_ref, wa_ref, ba_ref, wx_ref,
                         bx_ref, lam_ref, wo_ref,
                         n2_ref, wg2_ref, wu2_ref, wd2_ref, fn_ref,
                         y_ref, ys_ref, shg_ref, slru_ref, scv_ref,
                         x1_ref, x2_ref, proj_ref, st_ref, h_ref, xb_ref, qs_ref, k_ref, cum_ref, v_ref,
                         opw_ref, obuf_ref, *, chunk, chunks_per_seq, n_chunks):
    C = chunk
    s = pl.program_id(0)

    @pl.when(s == 0)
    def _():
        x1_ref[...] = jnp.zeros(x1_ref.shape, F32)
        st_ref[...] = jnp.zeros(st_ref.shape, F32)
        h_ref[...] = jnp.zeros(h_ref.shape, F32)
        xb_ref[0:8, :] = jnp.zeros((8, D_B), F32)

    @pl.when(s < LAYER_FILL_STEPS)
    def _():
        x2_ref[...] = xs_ref[...]

    ffn2 = _SwigluChunks(x2_ref[...], n2_ref, wg2_ref, wu2_ref, wd2_ref)
    ffn1 = _SwigluChunks(x_ref[...], n1_ref, wg1_ref, wu1_ref, wd1_ref)

    pos_in_seq = lax.rem(s + (chunks_per_seq - 1), chunks_per_seq)
    first = pos_in_seq == 0
    keep = jnp.where(first, 0.0, 1.0)
    x = x1_ref[...]
    xn = _rms(x, mn_ref[...]).astype(BF16)
    proj_ref[...] = _dot(xn, win_ref[...])
    fill = ffn2.pieces() + ffn1.pieces()
    slots = iter(LAYER_FILL_SLOTS)

    def run_fill():
        for _ in range(next(slots)):
            fill.pop(0)()

    run_fill()

    qs, k, g = _hgrn_gates(proj_ref, lb_ref)
    row = lax.broadcasted_iota(jnp.int32, (C, C), 0)
    col = lax.broadcasted_iota(jnp.int32, (C, C), 1)
    tri = jnp.where(col <= row, 1.0, 0.0).astype(BF16)
    g1, g2, g3 = _split3_bf16(g)
    cum = _dot(tri, g1) + _dot(tri, g2) + _dot(tri, g3)
    for h in range(N_HEADS):
        hs = slice(h * D_HEAD, (h + 1) * D_HEAD)
        qs_ref[h] = qs[:, hs]
        k_ref[h] = k[:, hs]
        cum_ref[h] = cum[:, hs]
        v_ref[h] = proj_ref[:, 2 * D_A + h * D_HEAD:2 * D_A + (h + 1) * D_HEAD]

    levels = []
    L = 2 * SUBLANES
    while L <= C:
        shift = L.bit_length() - 1
        levels.append((L, None if L == C else (row >> shift) == (col >> shift)))
        L *= 2

    gate = proj_ref[:, 3 * D_A:4 * D_A]
    for h in range(N_HEADS):
        hs = slice(h * D_HEAD, (h + 1) * D_HEAD)
        run_fill()
        _pairwise_tile(qs_ref, k_ref, cum_ref, v_ref, opw_ref, h, C)
        scores = jnp.zeros((C, C), F32)
        for L, same in levels:
            qt, kt = _level_operands(qs_ref, k_ref, cum_ref, h, C, L)
            sc = _dot_nt(qt, kt)
            scores = scores + (sc if same is None else jnp.where(same, sc, 0.0))
        cum_h = cum_ref[h]
        cl = _row_bcast(cum_ref, h, C - 1, C)
        qbar = (qs_ref[h] * jnp.exp(cum_h)).astype(BF16)
        khat = (k_ref[h] * jnp.exp(cl - cum_h)).astype(BF16)
        dec = jnp.exp(cum_ref[h, pl.ds(C - 1, 1), :])
        v_h = v_ref[h]
        st = st_ref[h] * keep
        o = _dot(scores.astype(BF16), v_h.astype(BF16)) + _dot_nt(qbar, st.astype(BF16)) + opw_ref[h]
        st_ref[h] = st * dec + _dot(v_h.T.astype(BF16), khat)
        ms = jnp.mean(o * o, axis=-1, keepdims=True)
        obuf_ref[:, hs] = (o * lax.rsqrt(ms + EPS) * hn_ref[:, hs] * _sigmoid(gate[:, hs])).astype(BF16)

    run_fill()
    y_ref[...] = _rms(ffn2.result(), fn_ref[...])

    rowi = lax.broadcasted_iota(jnp.int32, (C, 1), 0)
    xb = proj_ref[:, 4 * D_A:4 * D_A + D_B]
    yb = proj_ref[:, 4 * D_A + D_B:D_IN]
    xb_ref[0:8, :] = xb_ref[0:8, :] * keep
    xb_ref[pl.ds(8, C), :] = xb
    xc = (cb_ref[...] + cw_ref[3:4, :] * xb + cw_ref[2:3, :] * xb_ref[pl.ds(7, C), :]
          + cw_ref[1:2, :] * xb_ref[pl.ds(6, C), :] + cw_ref[0:1, :] * xb_ref[pl.ds(5, C), :])
    xb_ref[0:8, :] = xb[C - 8:C, :]
    a, mult, gate_i = _lru_coeffs(xc, wa_ref, ba_ref, wx_ref, bx_ref, lam_ref)
    run_fill()
    mult = jnp.where((rowi == 0) & first, 1.0, mult)
    hseq = _scan_tiles(a, mult * gate_i * xc, h_ref[...] * keep)
    h_ref[...] = hseq[C - 1:C, :]
    obuf_ref[:, D_A:D_A + D_B] = (hseq * _gelu_tanh(yb)).astype(BF16)

    x2_ref[...] = x + _dot(obuf_ref[...], wo_ref[...])
    run_fill()
    assert not fill and next(slots, None) is None
    x1_ref[...] = ffn1.result()

    @pl.when(s < LAYER_FILL_STEPS)
    def _():
        ys_ref[...] = y_ref[...]

    @pl.when((pos_in_seq == chunks_per_seq - 1) & (s >= 1) & (s <= n_chunks))
    def _():
        for h in range(N_HEADS):
            shg_ref[0, h] = st_ref[h].T
        slru_ref[0] = hseq[C - 1:C, :]
        scv_ref[0] = xb[C - (CONV_W - 1):C, :]


def _layer_prompt(x, xs, ffn1, mix, ffn2, fn):
    b, t, _ = x.shape
    C = MIX_CHUNK
    assert t % C == 0 and C % (2 * SUBLANES) == 0 and C & (C - 1) == 0
    assert LAYER_FFN_BOUNDS[0] == 0 and LAYER_FFN_BOUNDS[-1] == D_FF
    assert sum(LAYER_FILL_SLOTS) == 4 * (len(LAYER_FFN_BOUNDS) - 1) and len(LAYER_FILL_SLOTS) == N_HEADS + 4
    assert xs.shape == (LAYER_FILL_STEPS * C, D_MODEL), "sample rows must exactly fill the pipeline fill steps"
    nt = t // C
    n_chunks = b * nt
    rows_in = pl.BlockSpec((C, D_MODEL), lambda s: (jnp.minimum(s, n_chunks - 1), 0))
    rows_out = pl.BlockSpec((C, D_MODEL), lambda s: (jnp.clip(s - 2, 0, n_chunks - 1), 0))
    fill_index = lambda s: (jnp.minimum(s, LAYER_FILL_STEPS - 1), 0)
    rows_fill = pl.BlockSpec((C, D_MODEL), fill_index)
    rows_fill_in = pl.BlockSpec((C, D_MODEL), fill_index, pipeline_mode=pl.Buffered(1))
    seq_of = lambda s: jnp.clip((s - 1) // nt, 0, b - 1)
    vec = lambda n: _resident((1, n))
    ffn_specs = [vec(D_MODEL), _resident((D_MODEL, D_FF)), _resident((D_MODEL, D_FF)), _resident((D_FF, D_MODEL))]
    mix_specs = [vec(D_MODEL), _resident((D_MODEL, D_IN)), _resident((2, D_A)), vec(D_A),
                 _resident((CONV_W, D_B)), vec(D_B), _resident((D_B, D_B)), vec(D_B), _resident((D_B, D_B)),
                 vec(D_B), vec(D_B), _resident((D_A + D_B, D_MODEL))]
    out_shape = (jax.ShapeDtypeStruct((b * t, D_MODEL), F32),
                 jax.ShapeDtypeStruct(xs.shape, F32),
                 jax.ShapeDtypeStruct((b, N_HEADS, D_HEAD, D_HEAD), F32),
                 jax.ShapeDtypeStruct((b, 1, D_B), F32),
                 jax.ShapeDtypeStruct((b, CONV_W - 1, D_B), F32))
    out_specs = (rows_out, rows_fill,
                 pl.BlockSpec((1, N_HEADS, D_HEAD, D_HEAD), lambda s: (seq_of(s), 0, 0, 0)),
                 pl.BlockSpec((1, 1, D_B), lambda s: (seq_of(s), 0, 0)),
                 pl.BlockSpec((1, CONV_W - 1, D_B), lambda s: (seq_of(s), 0, 0)))
    per_head = pltpu.VMEM((N_HEADS, C, D_HEAD), F32)
    scratch = [pltpu.VMEM((C, D_MODEL), F32),
               pltpu.VMEM((C, D_MODEL), F32),
               pltpu.VMEM((C, D_IN), F32),
               pltpu.VMEM((N_HEADS, D_HEAD, D_HEAD), F32),
               pltpu.VMEM((1, D_B), F32),
               pltpu.VMEM((C + 8, D_B), F32),
               per_head, per_head, per_head, per_head,
               per_head,
               pltpu.VMEM((C, D_A + D_B), BF16)]
    return pl.pallas_call(
        functools.partial(_layer_prompt_kernel, chunk=C, chunks_per_seq=nt, n_chunks=n_chunks),
        out_shape=out_shape,
        grid=(n_chunks + LAYER_FILL_STEPS,),
        in_specs=[rows_in, rows_fill_in] + ffn_specs + mix_specs + ffn_specs + [vec(D_MODEL)],
        out_specs=out_specs,
        scratch_shapes=scratch,
        compiler_params=pltpu.CompilerParams(dimension_semantics=("arbitrary",),
                                             vmem_limit_bytes=LAYER_VMEM_LIMIT_BYTES),
        name="layer_prompt",
    )(x.reshape(b * t, D_MODEL), xs, *ffn1, *mix, *ffn2, fn)


def _mixer_sample_kernel(x_ref, mn_ref, win_ref, lb_ref, hn_ref, cw_ref, cb_ref, wa_ref, ba_ref, wx_ref,
                         bx_ref, lam_ref, wo_ref, s0_ref, h0_ref, hist1_ref, hist2_ref, hist3_ref,
                         y_ref, s1_ref, hseq_ref, xb_out_ref,
                         proj_ref, xb_ref, kpad_ref, cpad_ref, vpad_ref, qbar_ref, khat_ref, ecum_ref,
                         opw_ref, ost_ref, ob_ref, *, steps):
    T = steps
    N = x_ref.shape[0]
    G = s0_ref.shape[0]
    R = G * T
    i = pl.program_id(0)
    tshift = T.bit_length() - 1

    @pl.when(i == 0)
    def _():
        pad = kpad_ref.shape[0] - N
        zpad = jnp.zeros((pad, D_A), F32)
        kpad_ref[0:pad, :] = zpad
        cpad_ref[0:pad, :] = zpad
        vpad_ref[0:pad, :] = zpad
        xb_ref[0:8, :] = jnp.zeros((8, D_B), F32)

        xn = _rms(x_ref[...], mn_ref[...]).astype(BF16)
        proj_ref[...] = _dot(xn, win_ref[...])

        qs, k, g = _hgrn_gates(proj_ref, lb_ref)
        v = proj_ref[:, 2 * D_A:3 * D_A]
        tpos = lax.broadcasted_iota(jnp.int32, (N, 1), 0) & (T - 1)
        cum = g
        tail = jnp.zeros_like(g)
        for d in range(1, T):
            cum = cum + jnp.where(tpos >= d, pltpu.roll(g, d, 0), 0.0)
            tail = tail + jnp.where(tpos < T - d, pltpu.roll(g, N - d, 0), 0.0)
        _pairwise(qs, k, cum, v, kpad_ref, cpad_ref, vpad_ref, opw_ref, T, N)
        ecum = jnp.exp(cum)
        ecum_ref[...] = ecum
        qbar_ref[...] = qs * ecum
        khat_ref[...] = k * jnp.exp(tail)

        xb = proj_ref[:, 4 * D_A:4 * D_A + D_B]
        yb = proj_ref[:, 4 * D_A + D_B:D_IN]
        xb_ref[pl.ds(8, N), :] = xb
        xb_out_ref[...] = xb
        hist = (hist1_ref, hist2_ref, hist3_ref)
        xc = cb_ref[...] + cw_ref[CONV_W - 1:CONV_W, :] * xb
        for d in range(1, CONV_W):
            prev = jnp.where(tpos >= d, xb_ref[pl.ds(8 - d, N), :], hist[d - 1][...])
            xc = xc + cw_ref[CONV_W - 1 - d:CONV_W - d, :] * prev
        a, mult, gate_i = _lru_coeffs(xc, wa_ref, ba_ref, wx_ref, bx_ref, lam_ref)
        a_run, h_run = _scan_rows(a, mult * gate_i * xc, tpos, T)
        hseq = h_run + a_run * h0_ref[...]
        hseq_ref[...] = hseq
        ob_ref[...] = hseq * _gelu_tanh(yb)

    r0 = pl.multiple_of(i * R, R)
    rows = pl.ds(r0, R)
    row8 = lax.broadcasted_iota(jnp.int32, (8, 1), 0)
    seq_of_row = lax.broadcasted_iota(jnp.int32, (R, 1), 0) >> tshift
    for h in range(N_HEADS):
        hs = slice(h * D_HEAD, (h + 1) * D_HEAD)
        khat_t = khat_ref[rows, hs].T.astype(BF16)
        dec_t = ecum_ref[rows, hs].T
        v_g = proj_ref[rows, 2 * D_A + h * D_HEAD:2 * D_A + (h + 1) * D_HEAD]
        for p in range(R // 8):
            tile = pl.ds(pl.multiple_of(r0 + 8 * p, 8), 8)
            q8 = qbar_ref[tile, hs].astype(BF16)
            o8 = jnp.zeros((8, D_HEAD), F32)
            for j in range(8 // T):
                b = p * (8 // T) + j
                s0 = s0_ref[b, h]
                ob = _dot(q8, s0.astype(BF16))
                o8 = jnp.where((row8 >= j * T) & (row8 < (j + 1) * T), ob, o8)
                v_b = jnp.where(seq_of_row == b, v_g, 0.0).astype(BF16)
                s1_ref[b, h] = s0 * dec_t[:, b * T + T - 1:b * T + T] + _dot(khat_t, v_b)
            ost_ref[tile, hs] = o8

    @pl.when(i == pl.num_programs(0) - 1)
    def _():
        gate = proj_ref[:, 3 * D_A:4 * D_A]
        outs = []
        for h in range(N_HEADS):
            hs = slice(h * D_HEAD, (h + 1) * D_HEAD)
            o = opw_ref[:, hs] + ost_ref[:, hs]
            ms = jnp.mean(o * o, axis=-1, keepdims=True)
            outs.append(o * lax.rsqrt(ms + EPS) * hn_ref[:, hs] * _sigmoid(gate[:, hs]))
        o = jnp.concatenate(outs + [ob_ref[...]], axis=-1).astype(BF16)
        y_ref[...] = x_ref[...] + _dot(o, wo_ref[...])


def _mixer_sample(x, mn, win, lb, hn, cw, cb, wa, ba, wx, bx, lam, wo, s0, h0, buf):
    nb, T, _ = x.shape
    assert T == CONV_W and 8 % T == 0, "sample kernel assumes DEC_SEQ == CONV_W == 4"
    G = SAMPLE_GROUP
    N = nb * T
    assert nb % G == 0 and (G * T) % 8 == 0
    xr = x.reshape(N, D_MODEL)
    h0x = jnp.repeat(h0, T, axis=0)
    hists = [jnp.concatenate([buf[:, CONV_W - 1 - d:, :], jnp.zeros((nb, T - d, D_B), F32)], axis=1)
             .reshape(N, D_B) for d in range(1, CONV_W)]
    whole = lambda n: _resident((N, n))
    whole_out = lambda n: pl.BlockSpec((N, n), lambda i: (0, 0))
    state = pl.BlockSpec((G, N_HEADS, D_HEAD, D_HEAD), lambda i: (i, 0, 0, 0))
    vec = lambda n: _resident((1, n))
    out_shape = (jax.ShapeDtypeStruct((N, D_MODEL), F32),
                 jax.ShapeDtypeStruct((nb, N_HEADS, D_HEAD, D_HEAD), F32),
                 jax.ShapeDtypeStruct((N, D_B), F32),
                 jax.ShapeDtypeStruct((N, D_B), F32))
    half = lambda: pltpu.VMEM((N, D_A), F32)
    scratch = [pltpu.VMEM((N, D_IN), F32),
               pltpu.VMEM((N + 8, D_B), F32),
               pltpu.VMEM((N + 8, D_A), F32), pltpu.VMEM((N + 8, D_A), F32), pltpu.VMEM((N + 8, D_A), F32),
               half(), half(), half(),
               half(), half(),
               pltpu.VMEM((N, D_B), F32)]
    y, s1, hseq, xb = pl.pallas_call(
        functools.partial(_mixer_sample_kernel, steps=T),
        out_shape=out_shape,
        grid=(nb // G,),
        in_specs=[whole(D_MODEL), vec(D_MODEL), _resident((D_MODEL, D_IN)), _resident((2, D_A)), vec(D_A),
                  _resident((CONV_W, D_B)), vec(D_B), _resident((D_B, D_B)), vec(D_B), _resident((D_B, D_B)),
                  vec(D_B), vec(D_B), _resident((D_A + D_B, D_MODEL)),
                  state, whole(D_B), whole(D_B), whole(D_B), whole(D_B)],
        out_specs=(whole_out(D_MODEL), state, whole_out(D_B), whole_out(D_B)),
        scratch_shapes=scratch,
        compiler_params=pltpu.CompilerParams(dimension_semantics=("arbitrary",),
                                             vmem_limit_bytes=LAYER_VMEM_LIMIT_BYTES),
        name="mixer_sample",
    )(xr, mn, win, lb, hn, cw, cb, wa, ba, wx, bx, lam, wo, s0, h0x, *hists)
    hseq = hseq.reshape(nb, T, D_B)
    xb = xb.reshape(nb, T, D_B)
    return y, s1, hseq[:, T - 1], xb[:, T - (CONV_W - 1):]


def _block_diag(w):
    n, bi, bj = w.shape
    eye = jnp.eye(n, dtype=w.dtype)
    return (w[:, :, None, :] * eye[:, None, :, None]).reshape(n * bi, n * bj)


def kernel(x_prompt, x_sample, state_hgrn, state_lru, state_conv, ffn1_norm, ffn1_wg, ffn1_wu, ffn1_wd,
           mix_norm, w_in, hgrn_lb, hgrn_norm, conv_w, conv_b, lru_wa, lru_ba, lru_wx, lru_bx, lru_lambda,
           w_o, ffn2_norm, ffn2_wg, ffn2_wu, ffn2_wd, final_norm):
    assert ffn1_norm.shape[0] == 1 and hgrn_lb.shape[0] == 2, "single-layer model"
    bp, tp, _ = x_prompt.shape
    bs, ts, _ = x_sample.shape

    ffn1 = (ffn1_norm, ffn1_wg[0].astype(BF16), ffn1_wu[0].astype(BF16), ffn1_wd[0].astype(BF16))
    ffn2 = (ffn2_norm, ffn2_wg[0].astype(BF16), ffn2_wu[0].astype(BF16), ffn2_wd[0].astype(BF16))
    fn = final_norm.reshape(1, D_MODEL)
    mix = (mix_norm, w_in[0].astype(BF16), hgrn_lb, hgrn_norm, conv_w[0], conv_b,
           _block_diag(lru_wa[0]).astype(BF16), lru_ba[0].reshape(1, D_B),
           _block_diag(lru_wx[0]).astype(BF16), lru_bx[0].reshape(1, D_B), lru_lambda, w_o[0].astype(BF16))

    xs = _ffn(x_sample.reshape(bs * ts, D_MODEL), *ffn1, name="ffn1_sample")
    xs, hg_s, lru_s, cv_s = _mixer_sample(xs.reshape(bs, ts, D_MODEL), *mix, state_hgrn[0], state_lru[0],
                                          state_conv[0])

    yp, ys, hg_p, lru_p, cv_p = _layer_prompt(x_prompt, xs, ffn1, mix, ffn2, fn)

    return (yp.reshape(bp, tp, D_MODEL), ys.reshape(bs, ts, D_MODEL),
            hg_p[None], lru_p.reshape(1, bp, D_B), cv_p[None],
            hg_s[None], lru_s[None], cv_s[None])
```

```python
import functools
import math

import jax
import jax.numpy as jnp
from jax import lax
from jax.experimental import pallas as pl
from jax.experimental.pallas import tpu as pltpu

F32 = jnp.float32
BF16 = jnp.bfloat16

D_MODEL = 1024
D_FF = 2816
D_A = 512
D_B = 512
N_HEADS = 4
D_HEAD = 128
D_IN = 4 * D_A + 2 * D_B
CONV_W = 4
LRU_C = 8.0
EPS = 1e-6

SUBLANES = 8
NEG_LARGE = -1e30
FFN_TILE = 512
MIX_CHUNK = 256
FFN_BOUNDS = (0, 768, 1536, 2304, 2816)
LAYER_FILL_SLOTS = (1, 2, 2, 2, 2, 2, 2, 3)
LAYER_FILL_STEPS = 2
SAMPLE_GROUP = 8
VMEM_LIMIT_BYTES = 56 * 1024 * 1024
LAYER_VMEM_LIMIT_BYTES = 63 * 1024 * 1024

_NT = (((1,), (1,)), ((), ()))


def _dot(a, b):
    return jnp.dot(a, b, preferred_element_type=F32)


def _dot_nt(a, b):
    return lax.dot_general(a, b, _NT, preferred_element_type=F32)


def _rms(x, w):
    ms = jnp.mean(x * x, axis=-1, keepdims=True)
    return x * lax.rsqrt(ms + EPS) * w


def _sigmoid(x):
    return 1.0 / (1.0 + jnp.exp(-x))


def _log1p(x):
    u = 1.0 + x
    return jnp.where(u == 1.0, x, jnp.log(u) * x / (u - 1.0))


def _expm1(x):
    u = jnp.exp(x)
    return jnp.where(u == 1.0, x, (u - 1.0) * x / jnp.log(u))


def _softplus(x):
    return jnp.maximum(x, 0.0) + _log1p(jnp.exp(-jnp.abs(x)))


def _gelu_tanh(x):
    return 0.5 * x * (1.0 + jnp.tanh(math.sqrt(2.0 / math.pi) * (x + 0.044715 * (x * x * x))))


def _split3_bf16(x):
    x1 = x.astype(BF16)
    r1 = x - x1.astype(F32)
    x2 = r1.astype(BF16)
    r2 = r1 - x2.astype(F32)
    return x1, x2, r2.astype(BF16)


def _gate_up(xn, wgu_ref, c):
    lo, hi = FFN_BOUNDS[c], FFN_BOUNDS[c + 1]
    r = _dot(xn, wgu_ref[:, 2 * lo:2 * hi])
    return r[:, :hi - lo], r[:, hi - lo:]


def _ffn_kernel(x_ref, nw_ref, wgu_ref, wd_ref, o_ref):
    x = x_ref[...]
    xn = _rms(x, nw_ref[...]).astype(BF16)
    acc = jnp.zeros(x.shape, F32)
    for c in range(len(FFN_BOUNDS) - 1):
        g, u = _gate_up(xn, wgu_ref, c)
        h = (g * _sigmoid(g) * u).astype(BF16)
        acc = acc + _dot(h, wd_ref[FFN_BOUNDS[c]:FFN_BOUNDS[c + 1], :])
    o_ref[...] = x + 0.5 * acc


def _resident(shape):
    zeros = (0,) * len(shape)
    return pl.BlockSpec(shape, lambda *_: zeros, pipeline_mode=pl.Buffered(1))


def _ffn(x, norm_w, wgu, wd, *, name):
    n = x.shape[0]
    tm = min(FFN_TILE, n)
    assert n % tm == 0
    row = pl.BlockSpec((tm, D_MODEL), lambda i: (i, 0))
    return pl.pallas_call(
        _ffn_kernel,
        out_shape=jax.ShapeDtypeStruct((n, D_MODEL), F32),
        grid=(n // tm,),
        in_specs=[row, _resident((1, D_MODEL)), _resident((D_MODEL, 2 * D_FF)), _resident((D_FF, D_MODEL))],
        out_specs=row,
        compiler_params=pltpu.CompilerParams(dimension_semantics=("arbitrary",),
                                             vmem_limit_bytes=VMEM_LIMIT_BYTES),
        name=name,
    )(x, norm_w, wgu, wd)


def _hgrn_gates(proj_ref, lb_ref):
    l0 = lb_ref[0:1, :]
    l1 = lb_ref[1:2, :]
    m = jnp.maximum(l0, l1)
    e0 = jnp.exp(l0 - m)
    e1 = jnp.exp(l1 - m)
    lb = e0 / (e0 + e1)
    q = proj_ref[:, 0:D_A]
    fz = proj_ref[:, D_A:2 * D_A]
    f = lb + (1.0 - lb) * _sigmoid(fz)
    return q * _sigmoid(q), 1.0 - f, jnp.log(f)


def _pairwise(qs, k, cum, v, kpad_ref, cpad_ref, vpad_ref, out_ref, block, n_rows):
    pad = kpad_ref.shape[0] - n_rows
    kpad_ref[pl.ds(pad, n_rows), :] = k
    cpad_ref[pl.ds(pad, n_rows), :] = cum
    vpad_ref[pl.ds(pad, n_rows), :] = v
    rmod = lax.broadcasted_iota(jnp.int32, (n_rows, 1), 0) & (block - 1)
    out = [jnp.zeros((n_rows, D_HEAD), F32) for _ in range(N_HEADS)]
    for d in range(block):
        if d == 0:
            p = qs * k
            vs = v
        else:
            ks = kpad_ref[pl.ds(pad - d, n_rows), :]
            cs = cpad_ref[pl.ds(pad - d, n_rows), :]
            vs = vpad_ref[pl.ds(pad - d, n_rows), :]
            p = qs * ks * jnp.exp(cum - cs)
        valid = rmod >= d
        for h in range(N_HEADS):
            hs = slice(h * D_HEAD, (h + 1) * D_HEAD)
            s = jnp.sum(p[:, hs], axis=-1, keepdims=True)
            s = jnp.where(valid, s, 0.0)
            out[h] = out[h] + s * vs[:, hs]
    for h in range(N_HEADS):
        out_ref[:, h * D_HEAD:(h + 1) * D_HEAD] = out[h]


def _lru_coeffs(xc, wa_ref, ba_ref, wx_ref, bx_ref, lam_ref):
    xcb = xc.astype(BF16)
    r = _sigmoid(_dot(xcb, wa_ref[...]) + ba_ref[...])
    i = _sigmoid(_dot(xcb, wx_ref[...]) + bx_ref[...])
    log_a = -LRU_C * r * _softplus(-lam_ref[...])
    a = jnp.exp(log_a)
    mult = jnp.sqrt(-_expm1(2.0 * log_a))
    return a, mult, i


def _scan_rows(a, u, rowpos, span):
    s = 1
    while s < span:
        keep = rowpos >= s
        a_sh = jnp.where(keep, pltpu.roll(a, s, 0), 1.0)
        u_sh = jnp.where(keep, pltpu.roll(u, s, 0), 0.0)
        u = a * u_sh + u
        a = a * a_sh
        s *= 2
    return a, u


def _row_bcast(ref, h, r, n):
    return jnp.broadcast_to(ref[h, pl.ds(r, 1), :], (n, D_HEAD))


def _pairwise_tile(qs_ref, k_ref, cum_ref, v_ref, out_ref, h, n_rows):
    pos = lax.broadcasted_iota(jnp.int32, (SUBLANES, 1), 0)
    for j in range(n_rows // SUBLANES):
        rows = pl.ds(SUBLANES * j, SUBLANES)
        q_t = qs_ref[h, rows, :]
        c_t = cum_ref[h, rows, :]
        acc = jnp.zeros((SUBLANES, D_HEAD), F32)
        for s in range(SUBLANES):
            src = SUBLANES * j + s
            d = c_t - _row_bcast(cum_ref, h, src, SUBLANES)
            if s > 0:
                d = jnp.where(pos >= s, d, NEG_LARGE)
            p = q_t * _row_bcast(k_ref, h, src, SUBLANES) * jnp.exp(d)
            acc = acc + jnp.sum(p, axis=-1, keepdims=True) * _row_bcast(v_ref, h, src, SUBLANES)
        out_ref[h, rows, :] = acc


def _level_operands(qs_ref, k_ref, cum_ref, h, n_rows, L):
    half = L // 2
    zeros = jnp.zeros((half, D_HEAD), F32)
    qparts, kparts = [], []
    for b in range(n_rows // L):
        r0 = b * L
        first = pl.ds(r0, half)
        second = pl.ds(r0 + half, half)
        mid = _row_bcast(cum_ref, h, r0 + half - 1, half)
        kparts += [k_ref[h, first, :] * jnp.exp(mid - cum_ref[h, first, :]), zeros]
        qparts += [zeros, qs_ref[h, second, :] * jnp.exp(cum_ref[h, second, :] - mid)]
    return jnp.concatenate(qparts, axis=0).astype(BF16), jnp.concatenate(kparts, axis=0).astype(BF16)


def _scan_tiles(a, u, carry):
    n = a.shape[0]
    a3 = a.reshape(n // SUBLANES, SUBLANES, D_B)
    u3 = u.reshape(n // SUBLANES, SUBLANES, D_B)
    pos = lax.broadcasted_iota(jnp.int32, (1, SUBLANES, 1), 1)
    s = 1
    while s < SUBLANES:
        keep = pos >= s
        a_sh = jnp.where(keep, pltpu.roll(a3, s, 1), 1.0)
        u_sh = jnp.where(keep, pltpu.roll(u3, s, 1), 0.0)
        u3 = a3 * u_sh + u3
        a3 = a3 * a_sh
        s *= 2
    tiles = []
    for j in range(n // SUBLANES):
        hj = u3[j] + a3[j] * carry
        tiles.append(hj)
        carry = hj[SUBLANES - 1:SUBLANES, :]
    return jnp.concatenate(tiles, axis=0)


class _SwigluChunks:
    def __init__(self, x, nw_ref, wgu_ref, wd_ref):
        self.x = x
        self.xn = _rms(x, nw_ref[...]).astype(BF16)
        self.wgu_ref, self.wd_ref = wgu_ref, wd_ref
        self.hidden = {}
        self.acc = None

    def up(self, c):
        g, u = _gate_up(self.xn, self.wgu_ref, c)
        self.hidden[c] = (g * _sigmoid(g) * u).astype(BF16)

    def down(self, c):
        d = _dot(self.hidden.pop(c), self.wd_ref[FFN_BOUNDS[c]:FFN_BOUNDS[c + 1], :])
        self.acc = d if self.acc is None else self.acc + d

    def result(self):
        assert not self.hidden
        return self.x + 0.5 * self.acc

    def pieces(self):
        n = len(FFN_BOUNDS) - 1
        out = [functools.partial(self.up, 0)]
        for c in range(1, n):
            out += [functools.partial(self.up, c), functools.partial(self.down, c - 1)]
        return out + [functools.partial(self.down, n - 1)]


def _layer_prompt_kernel(x_ref, xs_ref, n1_ref, wgu1_ref, wd1_ref,
                         mn_ref, win_ref, lb_ref, hn_ref, cw_ref, cb_ref, wa_ref, ba_ref, wx_ref,
                         bx_ref, lam_ref, wo_ref,
                         n2_ref, wgu2_ref, wd2_ref, fn_ref,
                         y_ref, ys_ref, shg_ref, slru_ref, scv_ref,
                         x1_ref, x2_ref, proj_ref, st_ref, h_ref, xb_ref, qs_ref, k_ref, cum_ref, v_ref,
                         opw_ref, obuf_ref, *, chunk, chunks_per_seq, n_chunks):
    C = chunk
    s = pl.program_id(0)

    @pl.when(s == 0)
    def _():
        x1_ref[...] = jnp.zeros(x1_ref.shape, F32)
        st_ref[...] = jnp.zeros(st_ref.shape, F32)
        h_ref[...] = jnp.zeros(h_ref.shape, F32)
        xb_ref[0:8, :] = jnp.zeros((8, D_B), F32)

    @pl.when(s < LAYER_FILL_STEPS)
    def _():
        x2_ref[...] = xs_ref[...]

    ffn2 = _SwigluChunks(x2_ref[...], n2_ref, wgu2_ref, wd2_ref)
    ffn1 = _SwigluChunks(x_ref[...], n1_ref, wgu1_ref, wd1_ref)

    pos_in_seq = lax.rem(s + (chunks_per_seq - 1), chunks_per_seq)
    first = pos_in_seq == 0
    keep = jnp.where(first, 0.0, 1.0)
    x = x1_ref[...]
    xn = _rms(x, mn_ref[...]).astype(BF16)
    proj_ref[...] = _dot(xn, win_ref[...])
    fill = ffn2.pieces() + ffn1.pieces()
    slots = iter(LAYER_FILL_SLOTS)

    def run_fill():
        for _ in range(next(slots)):
            fill.pop(0)()

    run_fill()

    qs, k, g = _hgrn_gates(proj_ref, lb_ref)
    row = lax.broadcasted_iota(jnp.int32, (C, C), 0)
    col = lax.broadcasted_iota(jnp.int32, (C, C), 1)
    tri = jnp.where(col <= row, 1.0, 0.0).astype(BF16)
    parts = _dot(tri, jnp.concatenate(_split3_bf16(g), axis=-1))
    cum = parts[:, :D_A] + parts[:, D_A:2 * D_A] + parts[:, 2 * D_A:]
    for h in range(N_HEADS):
        hs = slice(h * D_HEAD, (h + 1) * D_HEAD)
        qs_ref[h] = qs[:, hs]
        k_ref[h] = k[:, hs]
        cum_ref[h] = cum[:, hs]
        v_ref[h] = proj_ref[:, 2 * D_A + h * D_HEAD:2 * D_A + (h + 1) * D_HEAD]

    levels = []
    L = 2 * SUBLANES
    while L <= C:
        shift = L.bit_length() - 1
        levels.append((L, None if L == C else (row >> shift) == (col >> shift)))
        L *= 2

    gate = proj_ref[:, 3 * D_A:4 * D_A]
    for h in range(N_HEADS):
        hs = slice(h * D_HEAD, (h + 1) * D_HEAD)
        run_fill()
        _pairwise_tile(qs_ref, k_ref, cum_ref, v_ref, opw_ref, h, C)
        scores = jnp.zeros((C, C), F32)
        for L, same in levels:
            qt, kt = _level_operands(qs_ref, k_ref, cum_ref, h, C, L)
            sc = _dot_nt(qt, kt)
            scores = scores + (sc if same is None else jnp.where(same, sc, 0.0))
        cum_h = cum_ref[h]
        cl = _row_bcast(cum_ref, h, C - 1, C)
        qbar = (qs_ref[h] * jnp.exp(cum_h)).astype(BF16)
        khat = (k_ref[h] * jnp.exp(cl - cum_h)).astype(BF16)
        dec = jnp.exp(cum_ref[h, pl.ds(C - 1, 1), :])
        v_h = v_ref[h]
        st = st_ref[h] * keep
        o = _dot(scores.astype(BF16), v_h.astype(BF16)) + _dot_nt(qbar, st.astype(BF16)) + opw_ref[h]
        st_ref[h] = st * dec + _dot(v_h.T.astype(BF16), khat)
        ms = jnp.mean(o * o, axis=-1, keepdims=True)
        obuf_ref[:, hs] = (o * lax.rsqrt(ms + EPS) * hn_ref[:, hs] * _sigmoid(gate[:, hs])).astype(BF16)

    run_fill()
    y_ref[...] = _rms(ffn2.result(), fn_ref[...])

    rowi = lax.broadcasted_iota(jnp.int32, (C, 1), 0)
    xb = proj_ref[:, 4 * D_A:4 * D_A + D_B]
    yb = proj_ref[:, 4 * D_A + D_B:D_IN]
    xb_ref[0:8, :] = xb_ref[0:8, :] * keep
    xb_ref[pl.ds(8, C), :] = xb
    xc = (cb_ref[...] + cw_ref[3:4, :] * xb + cw_ref[2:3, :] * xb_ref[pl.ds(7, C), :]
          + cw_ref[1:2, :] * xb_ref[pl.ds(6, C), :] + cw_ref[0:1, :] * xb_ref[pl.ds(5, C), :])
    xb_ref[0:8, :] = xb[C - 8:C, :]
    a, mult, gate_i = _lru_coeffs(xc, wa_ref, ba_ref, wx_ref, bx_ref, lam_ref)
    run_fill()
    mult = jnp.where((rowi == 0) & first, 1.0, mult)
    hseq = _scan_tiles(a, mult * gate_i * xc, h_ref[...] * keep)
    h_ref[...] = hseq[C - 1:C, :]
    obuf_ref[:, D_A:D_A + D_B] = (hseq * _gelu_tanh(yb)).astype(BF16)

    x2_ref[...] = x + _dot(obuf_ref[...], wo_ref[...])
    run_fill()
    assert not fill and next(slots, None) is None
    x1_ref[...] = ffn1.result()

    @pl.when(s < LAYER_FILL_STEPS)
    def _():
        ys_ref[...] = y_ref[...]

    @pl.when((pos_in_seq == chunks_per_seq - 1) & (s >= 1) & (s <= n_chunks))
    def _():
        for h in range(N_HEADS):
            shg_ref[0, h] = st_ref[h].T
        slru_ref[0] = hseq[C - 1:C, :]
        scv_ref[0] = xb[C - (CONV_W - 1):C, :]


def _layer_prompt(x, xs, ffn1, mix, ffn2, fn):
    b, t, _ = x.shape
    C = MIX_CHUNK
    assert t % C == 0 and C % (2 * SUBLANES) == 0 and C & (C - 1) == 0
    assert FFN_BOUNDS[0] == 0 and FFN_BOUNDS[-1] == D_FF
    assert sum(LAYER_FILL_SLOTS) == 4 * (len(FFN_BOUNDS) - 1) and len(LAYER_FILL_SLOTS) == N_HEADS + 4
    assert xs.shape == (LAYER_FILL_STEPS * C, D_MODEL), "sample rows must exactly fill the pipeline fill steps"
    nt = t // C
    n_chunks = b * nt
    rows_in = pl.BlockSpec((C, D_MODEL), lambda s: (jnp.minimum(s, n_chunks - 1), 0))
    rows_out = pl.BlockSpec((C, D_MODEL), lambda s: (jnp.clip(s - 2, 0, n_chunks - 1), 0))
    fill_index = lambda s: (jnp.minimum(s, LAYER_FILL_STEPS - 1), 0)
    rows_fill = pl.BlockSpec((C, D_MODEL), fill_index)
    rows_fill_in = pl.BlockSpec((C, D_MODEL), fill_index, pipeline_mode=pl.Buffered(1))
    seq_of = lambda s: jnp.clip((s - 1) // nt, 0, b - 1)
    vec = lambda n: _resident((1, n))
    ffn_specs = [vec(D_MODEL), _resident((D_MODEL, 2 * D_FF)), _resident((D_FF, D_MODEL))]
    mix_specs = [vec(D_MODEL), _resident((D_MODEL, D_IN)), _resident((2, D_A)), vec(D_A),
                 _resident((CONV_W, D_B)), vec(D_B), _resident((D_B, D_B)), vec(D_B), _resident((D_B, D_B)),
                 vec(D_B), vec(D_B), _resident((D_A + D_B, D_MODEL))]
    out_shape = (jax.ShapeDtypeStruct((b * t, D_MODEL), F32),
                 jax.ShapeDtypeStruct(xs.shape, F32),
                 jax.ShapeDtypeStruct((b, N_HEADS, D_HEAD, D_HEAD), F32),
                 jax.ShapeDtypeStruct((b, 1, D_B), F32),
                 jax.ShapeDtypeStruct((b, CONV_W - 1, D_B), F32))
    out_specs = (rows_out, rows_fill,
                 pl.BlockSpec((1, N_HEADS, D_HEAD, D_HEAD), lambda s: (seq_of(s), 0, 0, 0)),
                 pl.BlockSpec((1, 1, D_B), lambda s: (seq_of(s), 0, 0)),
                 pl.BlockSpec((1, CONV_W - 1, D_B), lambda s: (seq_of(s), 0, 0)))
    per_head = pltpu.VMEM((N_HEADS, C, D_HEAD), F32)
    scratch = [pltpu.VMEM((C, D_MODEL), F32),
               pltpu.VMEM((C, D_MODEL), F32),
               pltpu.VMEM((C, D_IN), F32),
               pltpu.VMEM((N_HEADS, D_HEAD, D_HEAD), F32),
               pltpu.VMEM((1, D_B), F32),
               pltpu.VMEM((C + 8, D_B), F32),
               per_head, per_head, per_head, per_head,
               per_head,
               pltpu.VMEM((C, D_A + D_B), BF16)]
    return pl.pallas_call(
        functools.partial(_layer_prompt_kernel, chunk=C, chunks_per_seq=nt, n_chunks=n_chunks),
        out_shape=out_shape,
        grid=(n_chunks + LAYER_FILL_STEPS,),
        in_specs=[rows_in, rows_fill_in] + ffn_specs + mix_specs + ffn_specs + [vec(D_MODEL)],
        out_specs=out_specs,
        scratch_shapes=scratch,
        compiler_params=pltpu.CompilerParams(dimension_semantics=("arbitrary",),
                                             vmem_limit_bytes=LAYER_VMEM_LIMIT_BYTES),
        name="layer_prompt",
    )(x.reshape(b * t, D_MODEL), xs, *ffn1, *mix, *ffn2, fn)


def _mixer_sample_kernel(x_ref, mn_ref, win_ref, lb_ref, hn_ref, cw_ref, cb_ref, wa_ref, ba_ref, wx_ref,
                         bx_ref, lam_ref, wo_ref, s0_ref, h0_ref, hist1_ref, hist2_ref, hist3_ref,
                         y_ref, s1_ref, hseq_ref, xb_out_ref,
                         proj_ref, xb_ref, kpad_ref, cpad_ref, vpad_ref, qbar_ref, khat_ref, ecum_ref,
                         opw_ref, ost_ref, ob_ref, *, steps):
    T = steps
    N = x_ref.shape[0]
    G = s0_ref.shape[0]
    R = G * T
    i = pl.program_id(0)
    tshift = T.bit_length() - 1

    @pl.when(i == 0)
    def _():
        pad = kpad_ref.shape[0] - N
        zpad = jnp.zeros((pad, D_A), F32)
        kpad_ref[0:pad, :] = zpad
        cpad_ref[0:pad, :] = zpad
        vpad_ref[0:pad, :] = zpad
        xb_ref[0:8, :] = jnp.zeros((8, D_B), F32)

        xn = _rms(x_ref[...], mn_ref[...]).astype(BF16)
        proj_ref[...] = _dot(xn, win_ref[...])

        qs, k, g = _hgrn_gates(proj_ref, lb_ref)
        v = proj_ref[:, 2 * D_A:3 * D_A]
        tpos = lax.broadcasted_iota(jnp.int32, (N, 1), 0) & (T - 1)
        cum = g
        tail = jnp.zeros_like(g)
        for d in range(1, T):
            cum = cum + jnp.where(tpos >= d, pltpu.roll(g, d, 0), 0.0)
            tail = tail + jnp.where(tpos < T - d, pltpu.roll(g, N - d, 0), 0.0)
        _pairwise(qs, k, cum, v, kpad_ref, cpad_ref, vpad_ref, opw_ref, T, N)
        ecum = jnp.exp(cum)
        ecum_ref[...] = ecum
        qbar_ref[...] = qs * ecum
        khat_ref[...] = k * jnp.exp(tail)

        xb = proj_ref[:, 4 * D_A:4 * D_A + D_B]
        yb = proj_ref[:, 4 * D_A + D_B:D_IN]
        xb_ref[pl.ds(8, N), :] = xb
        xb_out_ref[...] = xb
        hist = (hist1_ref, hist2_ref, hist3_ref)
        xc = cb_ref[...] + cw_ref[CONV_W - 1:CONV_W, :] * xb
        for d in range(1, CONV_W):
            prev = jnp.where(tpos >= d, xb_ref[pl.ds(8 - d, N), :], hist[d - 1][...])
            xc = xc + cw_ref[CONV_W - 1 - d:CONV_W - d, :] * prev
        a, mult, gate_i = _lru_coeffs(xc, wa_ref, ba_ref, wx_ref, bx_ref, lam_ref)
        a_run, h_run = _scan_rows(a, mult * gate_i * xc, tpos, T)
        hseq = h_run + a_run * h0_ref[...]
        hseq_ref[...] = hseq
        ob_ref[...] = hseq * _gelu_tanh(yb)

    r0 = pl.multiple_of(i * R, R)
    rows = pl.ds(r0, R)
    row8 = lax.broadcasted_iota(jnp.int32, (8, 1), 0)
    seq_of_row = lax.broadcasted_iota(jnp.int32, (R, 1), 0) >> tshift
    for h in range(N_HEADS):
        hs = slice(h * D_HEAD, (h + 1) * D_HEAD)
        khat_t = khat_ref[rows, hs].T.astype(BF16)
        dec_t = ecum_ref[rows, hs].T
        v_g = proj_ref[rows, 2 * D_A + h * D_HEAD:2 * D_A + (h + 1) * D_HEAD]
        for p in range(R // 8):
            tile = pl.ds(pl.multiple_of(r0 + 8 * p, 8), 8)
            q8 = qbar_ref[tile, hs].astype(BF16)
            o8 = jnp.zeros((8, D_HEAD), F32)
            for j in range(8 // T):
                b = p * (8 // T) + j
                s0 = s0_ref[b, h]
                ob = _dot(q8, s0.astype(BF16))
                o8 = jnp.where((row8 >= j * T) & (row8 < (j + 1) * T), ob, o8)
                v_b = jnp.where(seq_of_row == b, v_g, 0.0).astype(BF16)
                s1_ref[b, h] = s0 * dec_t[:, b * T + T - 1:b * T + T] + _dot(khat_t, v_b)
            ost_ref[tile, hs] = o8

    @pl.when(i == pl.num_programs(0) - 1)
    def _():
        gate = proj_ref[:, 3 * D_A:4 * D_A]
        outs = []
        for h in range(N_HEADS):
            hs = slice(h * D_HEAD, (h + 1) * D_HEAD)
            o = opw_ref[:, hs] + ost_ref[:, hs]
            ms = jnp.mean(o * o, axis=-1, keepdims=True)
            outs.append(o * lax.rsqrt(ms + EPS) * hn_ref[:, hs] * _sigmoid(gate[:, hs]))
        o = jnp.concatenate(outs + [ob_ref[...]], axis=-1).astype(BF16)
        y_ref[...] = x_ref[...] + _dot(o, wo_ref[...])


def _mixer_sample(x, mn, win, lb, hn, cw, cb, wa, ba, wx, bx, lam, wo, s0, h0, buf):
    nb, T, _ = x.shape
    assert T == CONV_W and 8 % T == 0, "sample kernel assumes DEC_SEQ == CONV_W == 4"
    G = SAMPLE_GROUP
    N = nb * T
    assert nb % G == 0 and (G * T) % 8 == 0
    xr = x.reshape(N, D_MODEL)
    h0x = jnp.repeat(h0, T, axis=0)
    hists = [jnp.concatenate([buf[:, CONV_W - 1 - d:, :], jnp.zeros((nb, T - d, D_B), F32)], axis=1)
             .reshape(N, D_B) for d in range(1, CONV_W)]
    whole = lambda n: _resident((N, n))
    whole_out = lambda n: pl.BlockSpec((N, n), lambda i: (0, 0))
    state = pl.BlockSpec((G, N_HEADS, D_HEAD, D_HEAD), lambda i: (i, 0, 0, 0))
    vec = lambda n: _resident((1, n))
    out_shape = (jax.ShapeDtypeStruct((N, D_MODEL), F32),
                 jax.ShapeDtypeStruct((nb, N_HEADS, D_HEAD, D_HEAD), F32),
                 jax.ShapeDtypeStruct((N, D_B), F32),
                 jax.ShapeDtypeStruct((N, D_B), F32))
    half = lambda: pltpu.VMEM((N, D_A), F32)
    scratch = [pltpu.VMEM((N, D_IN), F32),
               pltpu.VMEM((N + 8, D_B), F32),
               pltpu.VMEM((N + 8, D_A), F32), pltpu.VMEM((N + 8, D_A), F32), pltpu.VMEM((N + 8, D_A), F32),
               half(), half(), half(),
               half(), half(),
               pltpu.VMEM((N, D_B), F32)]
    y, s1, hseq, xb = pl.pallas_call(
        functools.partial(_mixer_sample_kernel, steps=T),
        out_shape=out_shape,
        grid=(nb // G,),
        in_specs=[whole(D_MODEL), vec(D_MODEL), _resident((D_MODEL, D_IN)), _resident((2, D_A)), vec(D_A),
                  _resident((CONV_W, D_B)), vec(D_B), _resident((D_B, D_B)), vec(D_B), _resident((D_B, D_B)),
                  vec(D_B), vec(D_B), _resident((D_A + D_B, D_MODEL)),
                  state, whole(D_B), whole(D_B), whole(D_B), whole(D_B)],
        out_specs=(whole_out(D_MODEL), state, whole_out(D_B), whole_out(D_B)),
        scratch_shapes=scratch,
        compiler_params=pltpu.CompilerParams(dimension_semantics=("arbitrary",),
                                             vmem_limit_bytes=LAYER_VMEM_LIMIT_BYTES),
        name="mixer_sample",
    )(xr, mn, win, lb, hn, cw, cb, wa, ba, wx, bx, lam, wo, s0, h0x, *hists)
    hseq = hseq.reshape(nb, T, D_B)
    xb = xb.reshape(nb, T, D_B)
    return y, s1, hseq[:, T - 1], xb[:, T - (CONV_W - 1):]


def _block_diag(w):
    n, bi, bj = w.shape
    eye = jnp.eye(n, dtype=w.dtype)
    return (w[:, :, None, :] * eye[:, None, :, None]).reshape(n * bi, n * bj)


def _interleave_gate_up(wg, wu):
    parts = []
    for lo, hi in zip(FFN_BOUNDS[:-1], FFN_BOUNDS[1:]):
        parts += [wg[:, lo:hi], wu[:, lo:hi]]
    return jnp.concatenate(parts, axis=1).astype(BF16)


def kernel(x_prompt, x_sample, state_hgrn, state_lru, state_conv, ffn1_norm, ffn1_wg, ffn1_wu, ffn1_wd,
           mix_norm, w_in, hgrn_lb, hgrn_norm, conv_w, conv_b, lru_wa, lru_ba, lru_wx, lru_bx, lru_lambda,
           w_o, ffn2_norm, ffn2_wg, ffn2_wu, ffn2_wd, final_norm):
    assert ffn1_norm.shape[0] == 1 and hgrn_lb.shape[0] == 2, "single-layer model"
    bp, tp, _ = x_prompt.shape
    bs, ts, _ = x_sample.shape

    ffn1 = (ffn1_norm, _interleave_gate_up(ffn1_wg[0], ffn1_wu[0]), ffn1_wd[0].astype(BF16))
    ffn2 = (ffn2_norm, _interleave_gate_up(ffn2_wg[0], ffn2_wu[0]), ffn2_wd[0].astype(BF16))
    fn = final_norm.reshape(1, D_MODEL)
    mix = (mix_norm, w_in[0].astype(BF16), hgrn_lb, hgrn_norm, conv_w[0], conv_b,
           _block_diag(lru_wa[0]).astype(BF16), lru_ba[0].reshape(1, D_B),
           _block_diag(lru_wx[0]).astype(BF16), lru_bx[0].reshape(1, D_B), lru_lambda, w_o[0].astype(BF16))

    xs = _ffn(x_sample.reshape(bs * ts, D_MODEL), *ffn1, name="ffn1_sample")
    xs, hg_s, lru_s, cv_s = _mixer_sample(xs.reshape(bs, ts, D_MODEL), *mix, state_hgrn[0], state_lru[0],
                                          state_conv[0])

    yp, ys, hg_p, lru_p, cv_p = _layer_prompt(x_prompt, xs, ffn1, mix, ffn2, fn)

    return (yp.reshape(bp, tp, D_MODEL), ys.reshape(bs, ts, D_MODEL),
            hg_p[None], lru_p.reshape(1, bp, D_B), cv_p[None],
            hg_s[None], lru_s[None], cv_s[None])
```

```python
import functools
import math

import jax
import jax.numpy as jnp
from jax import lax
from jax.experimental import pallas as pl
from jax.experimental.pallas import tpu as pltpu

F32 = jnp.float32
BF16 = jnp.bfloat16

D_MODEL = 1024
D_FF = 2816
D_A = 512
D_B = 512
N_HEADS = 4
D_HEAD = 128
D_IN = 4 * D_A + 2 * D_B
CONV_W = 4
LRU_C = 8.0
EPS = 1e-6

SUBLANES = 8
GATE_TILE = 256
NEG_LARGE = -1e30
FFN_TILE = 512
MIX_CHUNK = 256
FFN_CHUNK = 256
FFN_BOUNDS = (0, 768, 1536, 2304, 2816)
LAYER_FILL_SLOTS = (1, 2, 2, 2, 2, 2, 2, 3)
LAYER_FILL_STEPS = 2
SAMPLE_GROUP = 8
VMEM_LIMIT_BYTES = 56 * 1024 * 1024
LAYER_VMEM_LIMIT_BYTES = 63 * 1024 * 1024

_NT = (((1,), (1,)), ((), ()))


def _dot(a, b):
    return jnp.dot(a, b, preferred_element_type=F32)


def _dot_nt(a, b):
    return lax.dot_general(a, b, _NT, preferred_element_type=F32)


def _rms(x, w):
    ms = jnp.mean(x * x, axis=-1, keepdims=True)
    return x * lax.rsqrt(ms + EPS) * w


def _sigmoid(x):
    return 1.0 / (1.0 + jnp.exp(-x))


def _log1p(x):
    u = 1.0 + x
    return jnp.where(u == 1.0, x, jnp.log(u) * x / (u - 1.0))


def _expm1(x):
    u = jnp.exp(x)
    return jnp.where(u == 1.0, x, (u - 1.0) * x / jnp.log(u))


def _softplus(x):
    return jnp.maximum(x, 0.0) + _log1p(jnp.exp(-jnp.abs(x)))


def _gelu_tanh(x):
    return 0.5 * x * (1.0 + jnp.tanh(math.sqrt(2.0 / math.pi) * (x + 0.044715 * (x * x * x))))


def _ffn_kernel(x_ref, nw_ref, wg_ref, wu_ref, wd_ref, o_ref):
    x = x_ref[...]
    xn = _rms(x, nw_ref[...]).astype(BF16)
    acc = jnp.zeros(x.shape, F32)
    for c in range(D_FF // FFN_CHUNK):
        sl = slice(c * FFN_CHUNK, (c + 1) * FFN_CHUNK)
        g = _dot(xn, wg_ref[:, sl])
        u = _dot(xn, wu_ref[:, sl])
        h = (g * _sigmoid(g) * u).astype(BF16)
        acc = acc + _dot(h, wd_ref[sl, :])
    o_ref[...] = x + 0.5 * acc


def _resident(shape):
    zeros = (0,) * len(shape)
    return pl.BlockSpec(shape, lambda *_: zeros, pipeline_mode=pl.Buffered(1))


def _ffn(x, norm_w, wg, wu, wd, *, name):
    n = x.shape[0]
    tm = min(FFN_TILE, n)
    assert n % tm == 0 and D_FF % FFN_CHUNK == 0
    row = pl.BlockSpec((tm, D_MODEL), lambda i: (i, 0))
    return pl.pallas_call(
        _ffn_kernel,
        out_shape=jax.ShapeDtypeStruct((n, D_MODEL), F32),
        grid=(n // tm,),
        in_specs=[row, _resident((1, D_MODEL)), _resident((D_MODEL, D_FF)), _resident((D_MODEL, D_FF)),
                  _resident((D_FF, D_MODEL))],
        out_specs=row,
        compiler_params=pltpu.CompilerParams(dimension_semantics=("arbitrary",),
                                             vmem_limit_bytes=VMEM_LIMIT_BYTES),
        name=name,
    )(x, norm_w, wg, wu, wd)


class _Rows:
    def __init__(self, ref, start, n):
        self.ref, self.start, self.n = ref, start, n

    def __getitem__(self, idx):
        rows, cols = (slice(None), slice(None)) if idx is Ellipsis else idx
        lo = self.start + (rows.start or 0)
        hi = self.start + (self.n if rows.stop is None else rows.stop)
        return self.ref[lo:hi, cols]


def _mixer_vectors(pv_ref):
    sizes = (2, 1, CONV_W, 1, 1, 1, 1)
    views, start = [], 0
    for n in sizes:
        views.append(_Rows(pv_ref, start, n))
        start += n
    return views


def _hgrn_gates(proj_ref, lb_ref):
    l0 = lb_ref[0:1, :]
    l1 = lb_ref[1:2, :]
    m = jnp.maximum(l0, l1)
    e0 = jnp.exp(l0 - m)
    e1 = jnp.exp(l1 - m)
    lb = e0 / (e0 + e1)
    q = proj_ref[:, 0:D_A]
    fz = proj_ref[:, D_A:2 * D_A]
    f = lb + (1.0 - lb) * _sigmoid(fz)
    return q * _sigmoid(q), 1.0 - f, jnp.log(f)


def _pairwise(qs, k, cum, v, kpad_ref, cpad_ref, vpad_ref, out_ref, block, n_rows):
    pad = kpad_ref.shape[0] - n_rows
    kpad_ref[pl.ds(pad, n_rows), :] = k
    cpad_ref[pl.ds(pad, n_rows), :] = cum
    vpad_ref[pl.ds(pad, n_rows), :] = v
    rmod = lax.broadcasted_iota(jnp.int32, (n_rows, 1), 0) & (block - 1)
    out = [jnp.zeros((n_rows, D_HEAD), F32) for _ in range(N_HEADS)]
    for d in range(block):
        if d == 0:
            p = qs * k
            vs = v
        else:
            ks = kpad_ref[pl.ds(pad - d, n_rows), :]
            cs = cpad_ref[pl.ds(pad - d, n_rows), :]
            vs = vpad_ref[pl.ds(pad - d, n_rows), :]
            p = qs * ks * jnp.exp(cum - cs)
        valid = rmod >= d
        for h in range(N_HEADS):
            hs = slice(h * D_HEAD, (h + 1) * D_HEAD)
            s = jnp.sum(p[:, hs], axis=-1, keepdims=True)
            s = jnp.where(valid, s, 0.0)
            out[h] = out[h] + s * vs[:, hs]
    for h in range(N_HEADS):
        out_ref[:, h * D_HEAD:(h + 1) * D_HEAD] = out[h]


def _lru_coeffs(xc, wa_ref, ba_ref, wx_ref, bx_ref, lam_ref):
    xcb = xc.astype(BF16)

    def gate(w_ref, b_ref):
        tiles = [slice(j, j + GATE_TILE) for j in range(0, D_B, GATE_TILE)]
        z = jnp.concatenate([_dot(xcb[:, t], w_ref[t, t]) for t in tiles], axis=-1)
        return _sigmoid(z + b_ref[...])

    r = gate(wa_ref, ba_ref)
    i = gate(wx_ref, bx_ref)
    log_a = -LRU_C * r * _softplus(-lam_ref[...])
    a = jnp.exp(log_a)
    mult = jnp.sqrt(-_expm1(2.0 * log_a))
    return a, mult, i


def _scan_rows(a, u, rowpos, span):
    s = 1
    while s < span:
        keep = rowpos >= s
        a_sh = jnp.where(keep, pltpu.roll(a, s, 0), 1.0)
        u_sh = jnp.where(keep, pltpu.roll(u, s, 0), 0.0)
        u = a * u_sh + u
        a = a * a_sh
        s *= 2
    return a, u


def _row_bcast(ref, h, r, n):
    return jnp.broadcast_to(ref[h, pl.ds(r, 1), :], (n, D_HEAD))


def _pairwise_tile(qs_ref, k_ref, cum_ref, v_ref, out_ref, h, n_rows):
    pos = lax.broadcasted_iota(jnp.int32, (SUBLANES, 1), 0)
    for j in range(n_rows // SUBLANES):
        rows = pl.ds(SUBLANES * j, SUBLANES)
        q_t = qs_ref[h, rows, :]
        c_t = cum_ref[h, rows, :]
        acc = jnp.zeros((SUBLANES, D_HEAD), F32)
        for s in range(SUBLANES):
            src = SUBLANES * j + s
            d = c_t - _row_bcast(cum_ref, h, src, SUBLANES)
            if s > 0:
                d = jnp.where(pos >= s, d, NEG_LARGE)
            p = q_t * _row_bcast(k_ref, h, src, SUBLANES) * jnp.exp(d)
            acc = acc + jnp.sum(p, axis=-1, keepdims=True) * _row_bcast(v_ref, h, src, SUBLANES)
        out_ref[h, rows, :] = acc


def _level_operands(qs_ref, k_ref, cum_ref, h, n_rows, L):
    half = L // 2
    zeros = jnp.zeros((half, D_HEAD), F32)
    qparts, kparts = [], []
    for b in range(n_rows // L):
        r0 = b * L
        first = pl.ds(r0, half)
        second = pl.ds(r0 + half, half)
        mid = _row_bcast(cum_ref, h, r0 + half - 1, half)
        kparts += [k_ref[h, first, :] * jnp.exp(mid - cum_ref[h, first, :]), zeros]
        qparts += [zeros, qs_ref[h, second, :] * jnp.exp(cum_ref[h, second, :] - mid)]
    return jnp.concatenate(qparts, axis=0).astype(BF16), jnp.concatenate(kparts, axis=0).astype(BF16)


def _scan_tiles(a, u, carry):
    n, width = u.shape
    u3 = u.reshape(n // SUBLANES, SUBLANES, width)
    a3 = None if a is None else a.reshape(n // SUBLANES, SUBLANES, width)
    pos = lax.broadcasted_iota(jnp.int32, (1, SUBLANES, 1), 1)
    s = 1
    while s < SUBLANES:
        keep = pos >= s
        u_sh = jnp.where(keep, pltpu.roll(u3, s, 1), 0.0)
        if a3 is None:
            u3 = u3 + u_sh
        else:
            u3 = a3 * u_sh + u3
            a3 = a3 * jnp.where(keep, pltpu.roll(a3, s, 1), 1.0)
        s *= 2
    tiles = []
    for j in range(n // SUBLANES):
        hj = u3[j] + (carry if a3 is None else a3[j] * carry)
        tiles.append(hj)
        carry = hj[SUBLANES - 1:SUBLANES, :]
    return jnp.concatenate(tiles, axis=0)


class _SwigluChunks:
    def __init__(self, x, nw_ref, wg_ref, wu_ref, wd_ref):
        self.x = x
        self.xn = _rms(x, nw_ref[...]).astype(BF16)
        self.wg_ref, self.wu_ref, self.wd_ref = wg_ref, wu_ref, wd_ref
        self.hidden = {}
        self.acc = None

    @staticmethod
    def _cols(c):
        return slice(FFN_BOUNDS[c], FFN_BOUNDS[c + 1])

    def up(self, c):
        g = _dot(self.xn, self.wg_ref[:, self._cols(c)])
        u = _dot(self.xn, self.wu_ref[:, self._cols(c)])
        self.hidden[c] = (g * _sigmoid(g) * u).astype(BF16)

    def down(self, c):
        d = _dot(self.hidden.pop(c), self.wd_ref[self._cols(c), :])
        self.acc = d if self.acc is None else self.acc + d

    def result(self):
        assert not self.hidden
        return self.x + 0.5 * self.acc

    def pieces(self):
        n = len(FFN_BOUNDS) - 1
        out = [functools.partial(self.up, 0)]
        for c in range(1, n):
            out += [functools.partial(self.up, c), functools.partial(self.down, c - 1)]
        return out + [functools.partial(self.down, n - 1)]


def _layer_prompt_kernel(x_ref, xs_ref, n1_ref, wg1_ref, wu1_ref, wd1_ref,
                         mn_ref, win_ref, pv_ref, wa_ref, wx_ref, wo_ref,
                         n2_ref, wg2_ref, wu2_ref, wd2_ref, fn_ref,
                         y_ref, ys_ref, shg_ref, slru_ref, scv_ref,
                         x1_ref, x2_ref, proj_ref, st_ref, h_ref, xb_ref, qs_ref, k_ref, cum_ref, v_ref,
                         opw_ref, obuf_ref, *, chunk, chunks_per_seq, n_chunks):
    C = chunk
    s = pl.program_id(0)
    lb_ref, hn_ref, cw_ref, cb_ref, ba_ref, bx_ref, lam_ref = _mixer_vectors(pv_ref)

    @pl.when(s == 0)
    def _():
        x1_ref[...] = jnp.zeros(x1_ref.shape, F32)
        st_ref[...] = jnp.zeros(st_ref.shape, F32)
        h_ref[...] = jnp.zeros(h_ref.shape, F32)
        xb_ref[0:8, :] = jnp.zeros((8, D_B), F32)

    @pl.when(s < LAYER_FILL_STEPS)
    def _():
        x2_ref[...] = xs_ref[...]

    ffn2 = _SwigluChunks(x2_ref[...], n2_ref, wg2_ref, wu2_ref, wd2_ref)
    ffn1 = _SwigluChunks(x_ref[...], n1_ref, wg1_ref, wu1_ref, wd1_ref)

    pos_in_seq = lax.rem(s + (chunks_per_seq - 1), chunks_per_seq)
    first = pos_in_seq == 0
    keep = jnp.where(first, 0.0, 1.0)
    x = x1_ref[...]
    xn = _rms(x, mn_ref[...]).astype(BF16)
    proj_ref[...] = _dot(xn, win_ref[...])
    fill = ffn2.pieces() + ffn1.pieces()
    slots = iter(LAYER_FILL_SLOTS)

    def run_fill():
        for _ in range(next(slots)):
            fill.pop(0)()

    run_fill()

    qs, k, g = _hgrn_gates(proj_ref, lb_ref)
    row = lax.broadcasted_iota(jnp.int32, (C, C), 0)
    col = lax.broadcasted_iota(jnp.int32, (C, C), 1)
    cum = _scan_tiles(None, g, jnp.zeros((1, D_A), F32))
    for h in range(N_HEADS):
        hs = slice(h * D_HEAD, (h + 1) * D_HEAD)
        qs_ref[h] = qs[:, hs]
        k_ref[h] = k[:, hs]
        cum_ref[h] = cum[:, hs]
        v_ref[h] = proj_ref[:, 2 * D_A + h * D_HEAD:2 * D_A + (h + 1) * D_HEAD]

    levels = []
    L = 2 * SUBLANES
    while L <= C:
        shift = L.bit_length() - 1
        levels.append((L, None if L == C else (row >> shift) == (col >> shift)))
        L *= 2

    gate = proj_ref[:, 3 * D_A:4 * D_A]
    for h in range(N_HEADS):
        hs = slice(h * D_HEAD, (h + 1) * D_HEAD)
        run_fill()
        _pairwise_tile(qs_ref, k_ref, cum_ref, v_ref, opw_ref, h, C)
        scores = jnp.zeros((C, C), F32)
        for L, same in levels:
            qt, kt = _level_operands(qs_ref, k_ref, cum_ref, h, C, L)
            sc = _dot_nt(qt, kt)
            scores = scores + (sc if same is None else jnp.where(same, sc, 0.0))
        cum_h = cum_ref[h]
        cl = _row_bcast(cum_ref, h, C - 1, C)
        qbar = (qs_ref[h] * jnp.exp(cum_h)).astype(BF16)
        khat = (k_ref[h] * jnp.exp(cl - cum_h)).astype(BF16)
        dec = jnp.exp(cum_ref[h, pl.ds(C - 1, 1), :])
        v_h = v_ref[h]
        st = st_ref[h] * keep
        o = _dot(scores.astype(BF16), v_h.astype(BF16)) + _dot_nt(qbar, st.astype(BF16)) + opw_ref[h]
        st_ref[h] = st * dec + _dot(v_h.T.astype(BF16), khat)
        ms = jnp.mean(o * o, axis=-1, keepdims=True)
        obuf_ref[:, hs] = (o * lax.rsqrt(ms + EPS) * hn_ref[:, hs] * _sigmoid(gate[:, hs])).astype(BF16)

    run_fill()
    y_ref[...] = _rms(ffn2.result(), fn_ref[...])

    rowi = lax.broadcasted_iota(jnp.int32, (C, 1), 0)
    xb = proj_ref[:, 4 * D_A:4 * D_A + D_B]
    yb = proj_ref[:, 4 * D_A + D_B:D_IN]
    xb_ref[0:8, :] = xb_ref[0:8, :] * keep
    xb_ref[pl.ds(8, C), :] = xb
    xc = (cb_ref[...] + cw_ref[3:4, :] * xb + cw_ref[2:3, :] * xb_ref[pl.ds(7, C), :]
          + cw_ref[1:2, :] * xb_ref[pl.ds(6, C), :] + cw_ref[0:1, :] * xb_ref[pl.ds(5, C), :])
    xb_ref[0:8, :] = xb[C - 8:C, :]
    a, mult, gate_i = _lru_coeffs(xc, wa_ref, ba_ref, wx_ref, bx_ref, lam_ref)
    run_fill()
    mult = jnp.where((rowi == 0) & first, 1.0, mult)
    hseq = _scan_tiles(a, mult * gate_i * xc, h_ref[...] * keep)
    h_ref[...] = hseq[C - 1:C, :]
    obuf_ref[:, D_A:D_A + D_B] = (hseq * _gelu_tanh(yb)).astype(BF16)

    x2_ref[...] = x + _dot(obuf_ref[...], wo_ref[...])
    run_fill()
    assert not fill and next(slots, None) is None
    x1_ref[...] = ffn1.result()

    @pl.when(s < LAYER_FILL_STEPS)
    def _():
        ys_ref[...] = y_ref[...]

    @pl.when((pos_in_seq == chunks_per_seq - 1) & (s >= 1) & (s <= n_chunks))
    def _():
        for h in range(N_HEADS):
            shg_ref[0, h] = st_ref[h].T
        slru_ref[0] = hseq[C - 1:C, :]
        scv_ref[0] = xb[C - (CONV_W - 1):C, :]


def _layer_prompt(x, xs, ffn1, mix, ffn2, fn):
    b, t, _ = x.shape
    C = MIX_CHUNK
    assert t % C == 0 and C % (2 * SUBLANES) == 0 and C & (C - 1) == 0
    assert FFN_BOUNDS[0] == 0 and FFN_BOUNDS[-1] == D_FF
    assert sum(LAYER_FILL_SLOTS) == 4 * (len(FFN_BOUNDS) - 1) and len(LAYER_FILL_SLOTS) == N_HEADS + 4
    assert xs.shape == (LAYER_FILL_STEPS * C, D_MODEL), "sample rows must exactly fill the pipeline fill steps"
    nt = t // C
    n_chunks = b * nt
    rows_in = pl.BlockSpec((C, D_MODEL), lambda s: (jnp.minimum(s, n_chunks - 1), 0))
    rows_out = pl.BlockSpec((C, D_MODEL), lambda s: (jnp.clip(s - 2, 0, n_chunks - 1), 0))
    fill_index = lambda s: (jnp.minimum(s, LAYER_FILL_STEPS - 1), 0)
    rows_fill = pl.BlockSpec((C, D_MODEL), fill_index)
    rows_fill_in = pl.BlockSpec((C, D_MODEL), fill_index, pipeline_mode=pl.Buffered(1))
    seq_of = lambda s: jnp.clip((s - 1) // nt, 0, b - 1)
    vec = lambda n: _resident((1, n))
    ffn_specs = [vec(D_MODEL), _resident((D_MODEL, D_FF)), _resident((D_MODEL, D_FF)), _resident((D_FF, D_MODEL))]
    mix_specs = [_resident(a.shape) for a in mix]
    out_shape = (jax.ShapeDtypeStruct((b * t, D_MODEL), F32),
                 jax.ShapeDtypeStruct(xs.shape, F32),
                 jax.ShapeDtypeStruct((b, N_HEADS, D_HEAD, D_HEAD), F32),
                 jax.ShapeDtypeStruct((b, 1, D_B), F32),
                 jax.ShapeDtypeStruct((b, CONV_W - 1, D_B), F32))
    out_specs = (rows_out, rows_fill,
                 pl.BlockSpec((1, N_HEADS, D_HEAD, D_HEAD), lambda s: (seq_of(s), 0, 0, 0)),
                 pl.BlockSpec((1, 1, D_B), lambda s: (seq_of(s), 0, 0)),
                 pl.BlockSpec((1, CONV_W - 1, D_B), lambda s: (seq_of(s), 0, 0)))
    per_head = pltpu.VMEM((N_HEADS, C, D_HEAD), F32)
    scratch = [pltpu.VMEM((C, D_MODEL), F32),
               pltpu.VMEM((C, D_MODEL), F32),
               pltpu.VMEM((C, D_IN), F32),
               pltpu.VMEM((N_HEADS, D_HEAD, D_HEAD), F32),
               pltpu.VMEM((1, D_B), F32),
               pltpu.VMEM((C + 8, D_B), F32),
               per_head, per_head, per_head, per_head,
               per_head,
               pltpu.VMEM((C, D_A + D_B), BF16)]
    return pl.pallas_call(
        functools.partial(_layer_prompt_kernel, chunk=C, chunks_per_seq=nt, n_chunks=n_chunks),
        out_shape=out_shape,
        grid=(n_chunks + LAYER_FILL_STEPS,),
        in_specs=[rows_in, rows_fill_in] + ffn_specs + mix_specs + ffn_specs + [vec(D_MODEL)],
        out_specs=out_specs,
        scratch_shapes=scratch,
        compiler_params=pltpu.CompilerParams(dimension_semantics=("arbitrary",),
                                             vmem_limit_bytes=LAYER_VMEM_LIMIT_BYTES),
        name="layer_prompt",
    )(x.reshape(b * t, D_MODEL), xs, *ffn1, *mix, *ffn2, fn)


def _mixer_sample_kernel(x_ref, mn_ref, win_ref, pv_ref, wa_ref, wx_ref, wo_ref, s0_ref, rowstate_ref,
                         y_ref, s1_ref, hseq_ref, xb_out_ref,
                         proj_ref, xb_ref, kpad_ref, cpad_ref, vpad_ref, qbar_ref, khat_ref, ecum_ref,
                         opw_ref, ost_ref, ob_ref, *, steps):
    T = steps
    N = x_ref.shape[0]
    G = s0_ref.shape[0]
    R = G * T
    i = pl.program_id(0)
    tshift = T.bit_length() - 1
    lb_ref, hn_ref, cw_ref, cb_ref, ba_ref, bx_ref, lam_ref = _mixer_vectors(pv_ref)

    @pl.when(i == 0)
    def _():
        pad = kpad_ref.shape[0] - N
        zpad = jnp.zeros((pad, D_A), F32)
        kpad_ref[0:pad, :] = zpad
        cpad_ref[0:pad, :] = zpad
        vpad_ref[0:pad, :] = zpad
        xb_ref[0:8, :] = jnp.zeros((8, D_B), F32)

        xn = _rms(x_ref[...], mn_ref[...]).astype(BF16)
        proj_ref[...] = _dot(xn, win_ref[...])

        qs, k, g = _hgrn_gates(proj_ref, lb_ref)
        v = proj_ref[:, 2 * D_A:3 * D_A]
        tpos = lax.broadcasted_iota(jnp.int32, (N, 1), 0) & (T - 1)
        cum = g
        tail = jnp.zeros_like(g)
        for d in range(1, T):
            cum = cum + jnp.where(tpos >= d, pltpu.roll(g, d, 0), 0.0)
            tail = tail + jnp.where(tpos < T - d, pltpu.roll(g, N - d, 0), 0.0)
        _pairwise(qs, k, cum, v, kpad_ref, cpad_ref, vpad_ref, opw_ref, T, N)
        ecum = jnp.exp(cum)
        ecum_ref[...] = ecum
        qbar_ref[...] = qs * ecum
        khat_ref[...] = k * jnp.exp(tail)

        xb = proj_ref[:, 4 * D_A:4 * D_A + D_B]
        yb = proj_ref[:, 4 * D_A + D_B:D_IN]
        xb_ref[pl.ds(8, N), :] = xb
        xb_out_ref[...] = xb
        xc = cb_ref[...] + cw_ref[CONV_W - 1:CONV_W, :] * xb
        for d in range(1, CONV_W):
            prev = jnp.where(tpos >= d, xb_ref[pl.ds(8 - d, N), :], rowstate_ref[d])
            xc = xc + cw_ref[CONV_W - 1 - d:CONV_W - d, :] * prev
        a, mult, gate_i = _lru_coeffs(xc, wa_ref, ba_ref, wx_ref, bx_ref, lam_ref)
        a_run, h_run = _scan_rows(a, mult * gate_i * xc, tpos, T)
        hseq = h_run + a_run * rowstate_ref[0]
        hseq_ref[...] = hseq
        ob_ref[...] = hseq * _gelu_tanh(yb)

    r0 = pl.multiple_of(i * R, R)
    rows = pl.ds(r0, R)
    row8 = lax.broadcasted_iota(jnp.int32, (8, 1), 0)
    seq_of_row = lax.broadcasted_iota(jnp.int32, (R, 1), 0) >> tshift
    for h in range(N_HEADS):
        hs = slice(h * D_HEAD, (h + 1) * D_HEAD)
        khat_t = khat_ref[rows, hs].T.astype(BF16)
        dec_t = ecum_ref[rows, hs].T
        v_g = proj_ref[rows, 2 * D_A + h * D_HEAD:2 * D_A + (h + 1) * D_HEAD]
        for p in range(R // 8):
            tile = pl.ds(pl.multiple_of(r0 + 8 * p, 8), 8)
            q8 = qbar_ref[tile, hs].astype(BF16)
            o8 = jnp.zeros((8, D_HEAD), F32)
            for j in range(8 // T):
                b = p * (8 // T) + j
                s0 = s0_ref[b, h]
                ob = _dot(q8, s0.astype(BF16))
                o8 = jnp.where((row8 >= j * T) & (row8 < (j + 1) * T), ob, o8)
                v_b = jnp.where(seq_of_row == b, v_g, 0.0).astype(BF16)
                s1_ref[b, h] = s0 * dec_t[:, b * T + T - 1:b * T + T] + _dot(khat_t, v_b)
            ost_ref[tile, hs] = o8

    @pl.when(i == pl.num_programs(0) - 1)
    def _():
        gate = proj_ref[:, 3 * D_A:4 * D_A]
        outs = []
        for h in range(N_HEADS):
            hs = slice(h * D_HEAD, (h + 1) * D_HEAD)
            o = opw_ref[:, hs] + ost_ref[:, hs]
            ms = jnp.mean(o * o, axis=-1, keepdims=True)
            outs.append(o * lax.rsqrt(ms + EPS) * hn_ref[:, hs] * _sigmoid(gate[:, hs]))
        o = jnp.concatenate(outs + [ob_ref[...]], axis=-1).astype(BF16)
        y_ref[...] = x_ref[...] + _dot(o, wo_ref[...])


def _mixer_sample(x, mix, s0, h0, buf):
    nb, T, _ = x.shape
    assert T == CONV_W and 8 % T == 0, "sample kernel assumes DEC_SEQ == CONV_W == 4"
    G = SAMPLE_GROUP
    N = nb * T
    assert nb % G == 0 and (G * T) % 8 == 0
    xr = x.reshape(N, D_MODEL)
    rowstate = [jnp.broadcast_to(h0[:, None, :], (nb, T, D_B))]
    rowstate += [jnp.concatenate([buf[:, CONV_W - 1 - d:, :], jnp.zeros((nb, T - d, D_B), F32)], axis=1)
                 for d in range(1, CONV_W)]
    rowstate = jnp.stack(rowstate).reshape(CONV_W, N, D_B)
    whole_out = lambda n: pl.BlockSpec((N, n), lambda i: (0, 0))
    state = pl.BlockSpec((G, N_HEADS, D_HEAD, D_HEAD), lambda i: (i, 0, 0, 0))
    out_shape = (jax.ShapeDtypeStruct((N, D_MODEL), F32),
                 jax.ShapeDtypeStruct((nb, N_HEADS, D_HEAD, D_HEAD), F32),
                 jax.ShapeDtypeStruct((N, D_B), F32),
                 jax.ShapeDtypeStruct((N, D_B), F32))
    half = lambda: pltpu.VMEM((N, D_A), F32)
    scratch = [pltpu.VMEM((N, D_IN), F32),
               pltpu.VMEM((N + 8, D_B), F32),
               pltpu.VMEM((N + 8, D_A), F32), pltpu.VMEM((N + 8, D_A), F32), pltpu.VMEM((N + 8, D_A), F32),
               half(), half(), half(),
               half(), half(),
               pltpu.VMEM((N, D_B), F32)]
    y, s1, hseq, xb = pl.pallas_call(
        functools.partial(_mixer_sample_kernel, steps=T),
        out_shape=out_shape,
        grid=(nb // G,),
        in_specs=[_resident(xr.shape)] + [_resident(a.shape) for a in mix] + [state, _resident(rowstate.shape)],
        out_specs=(whole_out(D_MODEL), state, whole_out(D_B), whole_out(D_B)),
        scratch_shapes=scratch,
        compiler_params=pltpu.CompilerParams(dimension_semantics=("arbitrary",),
                                             vmem_limit_bytes=LAYER_VMEM_LIMIT_BYTES),
        name="mixer_sample",
    )(xr, *mix, s0, rowstate)
    hseq = hseq.reshape(nb, T, D_B)
    xb = xb.reshape(nb, T, D_B)
    return y, s1, hseq[:, T - 1], xb[:, T - (CONV_W - 1):]


def _block_diag(w):
    n, bi, bj = w.shape
    eye = jnp.eye(n, dtype=w.dtype)
    return (w[:, :, None, :] * eye[:, None, :, None]).reshape(n * bi, n * bj)


def kernel(x_prompt, x_sample, state_hgrn, state_lru, state_conv, ffn1_norm, ffn1_wg, ffn1_wu, ffn1_wd,
           mix_norm, w_in, hgrn_lb, hgrn_norm, conv_w, conv_b, lru_wa, lru_ba, lru_wx, lru_bx, lru_lambda,
           w_o, ffn2_norm, ffn2_wg, ffn2_wu, ffn2_wd, final_norm):
    assert ffn1_norm.shape[0] == 1 and hgrn_lb.shape[0] == 2, "single-layer model"
    bp, tp, _ = x_prompt.shape
    bs, ts, _ = x_sample.shape

    ffn1 = (ffn1_norm, ffn1_wg[0].astype(BF16), ffn1_wu[0].astype(BF16), ffn1_wd[0].astype(BF16))
    ffn2 = (ffn2_norm, ffn2_wg[0].astype(BF16), ffn2_wu[0].astype(BF16), ffn2_wd[0].astype(BF16))
    fn = final_norm.reshape(1, D_MODEL)
    vectors = jnp.concatenate([hgrn_lb, hgrn_norm, conv_w[0], conv_b, lru_ba[0].reshape(1, D_B),
                               lru_bx[0].reshape(1, D_B), lru_lambda], axis=0)
    mix = (mix_norm, w_in[0].astype(BF16), vectors, _block_diag(lru_wa[0]).astype(BF16),
           _block_diag(lru_wx[0]).astype(BF16), w_o[0].astype(BF16))

    xs = _ffn(x_sample.reshape(bs * ts, D_MODEL), *ffn1, name="ffn1_sample")
    xs, hg_s, lru_s, cv_s = _mixer_sample(xs.reshape(bs, ts, D_MODEL), mix, state_hgrn[0], state_lru[0],
                                          state_conv[0])

    yp, ys, hg_p, lru_p, cv_p = _layer_prompt(x_prompt, xs, ffn1, mix, ffn2, fn)

    return (yp.reshape(bp, tp, D_MODEL), ys.reshape(bs, ts, D_MODEL),
            hg_p[None], lru_p.reshape(1, bp, D_B), cv_p[None],
            hg_s[None], lru_s[None], cv_s[None])
```

```python
import functools
import math

import jax
import jax.numpy as jnp
from jax import lax
from jax.experimental import pallas as pl
from jax.experimental.pallas import tpu as pltpu

F32 = jnp.float32
BF16 = jnp.bfloat16

D_MODEL = 1024
D_FF = 2816
D_A = 512
D_B = 512
N_HEADS = 4
D_HEAD = 128
D_IN = 4 * D_A + 2 * D_B
CONV_W = 4
LRU_C = 8.0
EPS = 1e-6

SUBLANES = 8
GATE_TILE = 256
NEG_LARGE = -1e30
FFN_TILE = 512
MIX_CHUNK = 256
FFN_CHUNK = 256
FFN_BOUNDS = (0, 768, 1536, 2304, 2816)
LAYER_FILL_SLOTS = (1, 2, 2, 2, 2, 2, 2, 3)
LAYER_FILL_STEPS = 2
SAMPLE_GROUP = 8
VMEM_LIMIT_BYTES = 56 * 1024 * 1024
LAYER_VMEM_LIMIT_BYTES = 63 * 1024 * 1024

_NT = (((1,), (1,)), ((), ()))


def _dot(a, b):
    return jnp.dot(a, b, preferred_element_type=F32)


def _dot_nt(a, b):
    return lax.dot_general(a, b, _NT, preferred_element_type=F32)


def _rms(x, w):
    ms = jnp.mean(x * x, axis=-1, keepdims=True)
    return x * lax.rsqrt(ms + EPS) * w


def _sigmoid(x):
    return 1.0 / (1.0 + jnp.exp(-x))


def _log1p(x):
    u = 1.0 + x
    return jnp.where(u == 1.0, x, jnp.log(u) * x / (u - 1.0))


def _expm1(x):
    u = jnp.exp(x)
    return jnp.where(u == 1.0, x, (u - 1.0) * x / jnp.log(u))


def _softplus(x):
    return jnp.maximum(x, 0.0) + _log1p(jnp.exp(-jnp.abs(x)))


def _gelu_tanh(x):
    return 0.5 * x * (1.0 + jnp.tanh(math.sqrt(2.0 / math.pi) * (x + 0.044715 * (x * x * x))))


def _split3_bf16(x):
    x1 = x.astype(BF16)
    r1 = x - x1.astype(F32)
    x2 = r1.astype(BF16)
    r2 = r1 - x2.astype(F32)
    return x1, x2, r2.astype(BF16)


def _ffn_kernel(x_ref, nw_ref, wg_ref, wu_ref, wd_ref, o_ref):
    x = x_ref[...]
    xn = _rms(x, nw_ref[...]).astype(BF16)
    acc = jnp.zeros(x.shape, F32)
    for c in range(D_FF // FFN_CHUNK):
        sl = slice(c * FFN_CHUNK, (c + 1) * FFN_CHUNK)
        g = _dot(xn, wg_ref[:, sl])
        u = _dot(xn, wu_ref[:, sl])
        h = (g * _sigmoid(g) * u).astype(BF16)
        acc = acc + _dot(h, wd_ref[sl, :])
    o_ref[...] = x + 0.5 * acc


def _resident(shape):
    zeros = (0,) * len(shape)
    return pl.BlockSpec(shape, lambda *_: zeros, pipeline_mode=pl.Buffered(1))


def _ffn(x, norm_w, wg, wu, wd, *, name):
    n = x.shape[0]
    tm = min(FFN_TILE, n)
    assert n % tm == 0 and D_FF % FFN_CHUNK == 0
    row = pl.BlockSpec((tm, D_MODEL), lambda i: (i, 0))
    return pl.pallas_call(
        _ffn_kernel,
        out_shape=jax.ShapeDtypeStruct((n, D_MODEL), F32),
        grid=(n // tm,),
        in_specs=[row, _resident((1, D_MODEL)), _resident((D_MODEL, D_FF)), _resident((D_MODEL, D_FF)),
                  _resident((D_FF, D_MODEL))],
        out_specs=row,
        compiler_params=pltpu.CompilerParams(dimension_semantics=("arbitrary",),
                                             vmem_limit_bytes=VMEM_LIMIT_BYTES),
        name=name,
    )(x, norm_w, wg, wu, wd)


class _Rows:
    def __init__(self, ref, start, n):
        self.ref, self.start, self.n = ref, start, n

    def __getitem__(self, idx):
        rows, cols = (slice(None), slice(None)) if idx is Ellipsis else idx
        lo = self.start + (rows.start or 0)
        hi = self.start + (self.n if rows.stop is None else rows.stop)
        return self.ref[lo:hi, cols]


def _mixer_vectors(pv_ref):
    sizes = (2, 1, CONV_W, 1, 1, 1, 1)
    views, start = [], 0
    for n in sizes:
        views.append(_Rows(pv_ref, start, n))
        start += n
    return views


def _hgrn_gates(proj_ref, lb_ref):
    l0 = lb_ref[0:1, :]
    l1 = lb_ref[1:2, :]
    m = jnp.maximum(l0, l1)
    e0 = jnp.exp(l0 - m)
    e1 = jnp.exp(l1 - m)
    lb = e0 / (e0 + e1)
    q = proj_ref[:, 0:D_A]
    fz = proj_ref[:, D_A:2 * D_A]
    f = lb + (1.0 - lb) * _sigmoid(fz)
    return q * _sigmoid(q), 1.0 - f, jnp.log(f)


def _pairwise(qs, k, cum, v, kpad_ref, cpad_ref, vpad_ref, out_ref, block, n_rows):
    pad = kpad_ref.shape[0] - n_rows
    kpad_ref[pl.ds(pad, n_rows), :] = k
    cpad_ref[pl.ds(pad, n_rows), :] = cum
    vpad_ref[pl.ds(pad, n_rows), :] = v
    rmod = lax.broadcasted_iota(jnp.int32, (n_rows, 1), 0) & (block - 1)
    out = [jnp.zeros((n_rows, D_HEAD), F32) for _ in range(N_HEADS)]
    for d in range(block):
        if d == 0:
            p = qs * k
            vs = v
        else:
            ks = kpad_ref[pl.ds(pad - d, n_rows), :]
            cs = cpad_ref[pl.ds(pad - d, n_rows), :]
            vs = vpad_ref[pl.ds(pad - d, n_rows), :]
            p = qs * ks * jnp.exp(cum - cs)
        valid = rmod >= d
        for h in range(N_HEADS):
            hs = slice(h * D_HEAD, (h + 1) * D_HEAD)
            s = jnp.sum(p[:, hs], axis=-1, keepdims=True)
            s = jnp.where(valid, s, 0.0)
            out[h] = out[h] + s * vs[:, hs]
    for h in range(N_HEADS):
        out_ref[:, h * D_HEAD:(h + 1) * D_HEAD] = out[h]


def _lru_coeffs(xc, wa_ref, ba_ref, wx_ref, bx_ref, lam_ref):
    xcb = xc.astype(BF16)

    def gate(w_ref, b_ref):
        tiles = [slice(j, j + GATE_TILE) for j in range(0, D_B, GATE_TILE)]
        z = jnp.concatenate([_dot(xcb[:, t], w_ref[t, t]) for t in tiles], axis=-1)
        return _sigmoid(z + b_ref[...])

    r = gate(wa_ref, ba_ref)
    i = gate(wx_ref, bx_ref)
    log_a = -LRU_C * r * _softplus(-lam_ref[...])
    a = jnp.exp(log_a)
    mult = jnp.sqrt(-_expm1(2.0 * log_a))
    return a, mult, i


def _scan_rows(a, u, rowpos, span):
    s = 1
    while s < span:
        keep = rowpos >= s
        a_sh = jnp.where(keep, pltpu.roll(a, s, 0), 1.0)
        u_sh = jnp.where(keep, pltpu.roll(u, s, 0), 0.0)
        u = a * u_sh + u
        a = a * a_sh
        s *= 2
    return a, u


def _row_bcast(ref, h, r, n):
    return jnp.broadcast_to(ref[h, pl.ds(r, 1), :], (n, D_HEAD))


def _pairwise_tile(qs_ref, k_ref, cum_ref, v_ref, out_ref, h, n_rows):
    pos = lax.broadcasted_iota(jnp.int32, (SUBLANES, 1), 0)
    for j in range(n_rows // SUBLANES):
        rows = pl.ds(SUBLANES * j, SUBLANES)
        q_t = qs_ref[h, rows, :]
        c_t = cum_ref[h, rows, :]
        acc = jnp.zeros((SUBLANES, D_HEAD), F32)
        for s in range(SUBLANES):
            src = SUBLANES * j + s
            d = c_t - _row_bcast(cum_ref, h, src, SUBLANES)
            if s > 0:
                d = jnp.where(pos >= s, d, NEG_LARGE)
            p = q_t * _row_bcast(k_ref, h, src, SUBLANES) * jnp.exp(d)
            acc = acc + jnp.sum(p, axis=-1, keepdims=True) * _row_bcast(v_ref, h, src, SUBLANES)
        out_ref[h, rows, :] = acc


def _level_operands(qs_ref, k_ref, cum_ref, h, n_rows, L):
    half = L // 2
    zeros = jnp.zeros((half, D_HEAD), F32)
    qparts, kparts = [], []
    for b in range(n_rows // L):
        r0 = b * L
        first = pl.ds(r0, half)
        second = pl.ds(r0 + half, half)
        mid = _row_bcast(cum_ref, h, r0 + half - 1, half)
        kparts += [k_ref[h, first, :] * jnp.exp(mid - cum_ref[h, first, :]), zeros]
        qparts += [zeros, qs_ref[h, second, :] * jnp.exp(cum_ref[h, second, :] - mid)]
    return jnp.concatenate(qparts, axis=0).astype(BF16), jnp.concatenate(kparts, axis=0).astype(BF16)


def _scan_tiles(a, u, carry):
    n = a.shape[0]
    a3 = a.reshape(n // SUBLANES, SUBLANES, D_B)
    u3 = u.reshape(n // SUBLANES, SUBLANES, D_B)
    pos = lax.broadcasted_iota(jnp.int32, (1, SUBLANES, 1), 1)
    s = 1
    while s < SUBLANES:
        keep = pos >= s
        a_sh = jnp.where(keep, pltpu.roll(a3, s, 1), 1.0)
        u_sh = jnp.where(keep, pltpu.roll(u3, s, 1), 0.0)
        u3 = a3 * u_sh + u3
        a3 = a3 * a_sh
        s *= 2
    tiles = []
    for j in range(n // SUBLANES):
        hj = u3[j] + a3[j] * carry
        tiles.append(hj)
        carry = hj[SUBLANES - 1:SUBLANES, :]
    return jnp.concatenate(tiles, axis=0)


class _SwigluChunks:
    def __init__(self, x, nw_ref, wg_ref, wu_ref, wd_ref):
        self.x = x
        self.xn = _rms(x, nw_ref[...]).astype(BF16)
        self.wg_ref, self.wu_ref, self.wd_ref = wg_ref, wu_ref, wd_ref
        self.hidden = {}
        self.acc = None

    @staticmethod
    def _cols(c):
        return slice(FFN_BOUNDS[c], FFN_BOUNDS[c + 1])

    def up(self, c):
        g = _dot(self.xn, self.wg_ref[:, self._cols(c)])
        u = _dot(self.xn, self.wu_ref[:, self._cols(c)])
        self.hidden[c] = (g * _sigmoid(g) * u).astype(BF16)

    def down(self, c):
        d = _dot(self.hidden.pop(c), self.wd_ref[self._cols(c), :])
        self.acc = d if self.acc is None else self.acc + d

    def result(self):
        assert not self.hidden
        return self.x + 0.5 * self.acc

    def pieces(self):
        n = len(FFN_BOUNDS) - 1
        out = [functools.partial(self.up, 0)]
        for c in range(1, n):
            out += [functools.partial(self.up, c), functools.partial(self.down, c - 1)]
        return out + [functools.partial(self.down, n - 1)]


def _layer_prompt_kernel(x_ref, xs_ref, n1_ref, wg1_ref, wu1_ref, wd1_ref,
                         mn_ref, win_ref, pv_ref, wa_ref, wx_ref, wo_ref,
                         n2_ref, wg2_ref, wu2_ref, wd2_ref, fn_ref,
                         y_ref, ys_ref, shg_ref, slru_ref, scv_ref,
                         x1_ref, x2_ref, proj_ref, st_ref, h_ref, xb_ref, qs_ref, k_ref, cum_ref, v_ref,
                         opw_ref, obuf_ref, *, chunk, chunks_per_seq, n_chunks):
    C = chunk
    s = pl.program_id(0)
    lb_ref, hn_ref, cw_ref, cb_ref, ba_ref, bx_ref, lam_ref = _mixer_vectors(pv_ref)

    @pl.when(s == 0)
    def _():
        x1_ref[...] = jnp.zeros(x1_ref.shape, F32)
        st_ref[...] = jnp.zeros(st_ref.shape, F32)
        h_ref[...] = jnp.zeros(h_ref.shape, F32)
        xb_ref[0:8, :] = jnp.zeros((8, D_B), F32)

    @pl.when(s < LAYER_FILL_STEPS)
    def _():
        x2_ref[...] = xs_ref[...]

    ffn2 = _SwigluChunks(x2_ref[...], n2_ref, wg2_ref, wu2_ref, wd2_ref)
    ffn1 = _SwigluChunks(x_ref[...], n1_ref, wg1_ref, wu1_ref, wd1_ref)

    pos_in_seq = lax.rem(s + (chunks_per_seq - 1), chunks_per_seq)
    first = pos_in_seq == 0
    keep = jnp.where(first, 0.0, 1.0)
    x = x1_ref[...]
    xn = _rms(x, mn_ref[...]).astype(BF16)
    proj_ref[...] = _dot(xn, win_ref[...])
    fill = ffn2.pieces() + ffn1.pieces()
    slots = iter(LAYER_FILL_SLOTS)

    def run_fill():
        for _ in range(next(slots)):
            fill.pop(0)()

    run_fill()

    qs, k, g = _hgrn_gates(proj_ref, lb_ref)
    row = lax.broadcasted_iota(jnp.int32, (C, C), 0)
    col = lax.broadcasted_iota(jnp.int32, (C, C), 1)
    tri = jnp.where(col <= row, 1.0, 0.0).astype(BF16)
    g1, g2, g3 = _split3_bf16(g)
    cum = _dot(tri, g1) + _dot(tri, g2) + _dot(tri, g3)
    for h in range(N_HEADS):
        hs = slice(h * D_HEAD, (h + 1) * D_HEAD)
        qs_ref[h] = qs[:, hs]
        k_ref[h] = k[:, hs]
        cum_ref[h] = cum[:, hs]
        v_ref[h] = proj_ref[:, 2 * D_A + h * D_HEAD:2 * D_A + (h + 1) * D_HEAD]

    levels = []
    L = 2 * SUBLANES
    while L <= C:
        shift = L.bit_length() - 1
        levels.append((L, None if L == C else (row >> shift) == (col >> shift)))
        L *= 2

    gate = proj_ref[:, 3 * D_A:4 * D_A]
    for h in range(N_HEADS):
        hs = slice(h * D_HEAD, (h + 1) * D_HEAD)
        run_fill()
        _pairwise_tile(qs_ref, k_ref, cum_ref, v_ref, opw_ref, h, C)
        scores = jnp.zeros((C, C), F32)
        for L, same in levels:
            qt, kt = _level_operands(qs_ref, k_ref, cum_ref, h, C, L)
            sc = _dot_nt(qt, kt)
            scores = scores + (sc if same is None else jnp.where(same, sc, 0.0))
        cum_h = cum_ref[h]
        cl = _row_bcast(cum_ref, h, C - 1, C)
        qbar = (qs_ref[h] * jnp.exp(cum_h)).astype(BF16)
        khat = (k_ref[h] * jnp.exp(cl - cum_h)).astype(BF16)
        dec = jnp.exp(cum_ref[h, pl.ds(C - 1, 1), :])
        v_h = v_ref[h]
        st = st_ref[h] * keep
        o = _dot(scores.astype(BF16), v_h.astype(BF16)) + _dot_nt(qbar, st.astype(BF16)) + opw_ref[h]
        st_ref[h] = st * dec + _dot(v_h.T.astype(BF16), khat)
        ms = jnp.mean(o * o, axis=-1, keepdims=True)
        obuf_ref[:, hs] = (o * lax.rsqrt(ms + EPS) * hn_ref[:, hs] * _sigmoid(gate[:, hs])).astype(BF16)

    run_fill()
    y_ref[...] = _rms(ffn2.result(), fn_ref[...])

    rowi = lax.broadcasted_iota(jnp.int32, (C, 1), 0)
    xb = proj_ref[:, 4 * D_A:4 * D_A + D_B]
    yb = proj_ref[:, 4 * D_A + D_B:D_IN]
    xb_ref[0:8, :] = xb_ref[0:8, :] * keep
    xb_ref[pl.ds(8, C), :] = xb
    xc = (cb_ref[...] + cw_ref[3:4, :] * xb + cw_ref[2:3, :] * xb_ref[pl.ds(7, C), :]
          + cw_ref[1:2, :] * xb_ref[pl.ds(6, C), :] + cw_ref[0:1, :] * xb_ref[pl.ds(5, C), :])
    xb_ref[0:8, :] = xb[C - 8:C, :]
    a, mult, gate_i = _lru_coeffs(xc, wa_ref, ba_ref, wx_ref, bx_ref, lam_ref)
    run_fill()
    mult = jnp.where((rowi == 0) & first, 1.0, mult)
    hseq = _scan_tiles(a, mult * gate_i * xc, h_ref[...] * keep)
    h_ref[...] = hseq[C - 1:C, :]
    obuf_ref[:, D_A:D_A + D_B] = (hseq * _gelu_tanh(yb)).astype(BF16)

    x2_ref[...] = x + _dot(obuf_ref[...], wo_ref[...])
    run_fill()
    assert not fill and next(slots, None) is None
    x1_ref[...] = ffn1.result()

    @pl.when(s < LAYER_FILL_STEPS)
    def _():
        ys_ref[...] = y_ref[...]

    @pl.when((pos_in_seq == chunks_per_seq - 1) & (s >= 1) & (s <= n_chunks))
    def _():
        for h in range(N_HEADS):
            shg_ref[0, h] = st_ref[h].T
        slru_ref[0] = hseq[C - 1:C, :]
        scv_ref[0] = xb[C - (CONV_W - 1):C, :]


def _layer_prompt(x, xs, ffn1, mix, ffn2, fn):
    b, t, _ = x.shape
    C = MIX_CHUNK
    assert t % C == 0 and C % (2 * SUBLANES) == 0 and C & (C - 1) == 0
    assert FFN_BOUNDS[0] == 0 and FFN_BOUNDS[-1] == D_FF
    assert sum(LAYER_FILL_SLOTS) == 4 * (len(FFN_BOUNDS) - 1) and len(LAYER_FILL_SLOTS) == N_HEADS + 4
    assert xs.shape == (LAYER_FILL_STEPS * C, D_MODEL), "sample rows must exactly fill the pipeline fill steps"
    nt = t // C
    n_chunks = b * nt
    rows_in = pl.BlockSpec((C, D_MODEL), lambda s: (jnp.minimum(s, n_chunks - 1), 0))
    rows_out = pl.BlockSpec((C, D_MODEL), lambda s: (jnp.clip(s - 2, 0, n_chunks - 1), 0))
    fill_index = lambda s: (jnp.minimum(s, LAYER_FILL_STEPS - 1), 0)
    rows_fill = pl.BlockSpec((C, D_MODEL), fill_index)
    rows_fill_in = pl.BlockSpec((C, D_MODEL), fill_index, pipeline_mode=pl.Buffered(1))
    seq_of = lambda s: jnp.clip((s - 1) // nt, 0, b - 1)
    vec = lambda n: _resident((1, n))
    ffn_specs = [vec(D_MODEL), _resident((D_MODEL, D_FF)), _resident((D_MODEL, D_FF)), _resident((D_FF, D_MODEL))]
    mix_specs = [_resident(a.shape) for a in mix]
    out_shape = (jax.ShapeDtypeStruct((b * t, D_MODEL), F32),
                 jax.ShapeDtypeStruct(xs.shape, F32),
                 jax.ShapeDtypeStruct((b, N_HEADS, D_HEAD, D_HEAD), F32),
                 jax.ShapeDtypeStruct((b, 1, D_B), F32),
                 jax.ShapeDtypeStruct((b, CONV_W - 1, D_B), F32))
    out_specs = (rows_out, rows_fill,
                 pl.BlockSpec((1, N_HEADS, D_HEAD, D_HEAD), lambda s: (seq_of(s), 0, 0, 0)),
                 pl.BlockSpec((1, 1, D_B), lambda s: (seq_of(s), 0, 0)),
                 pl.BlockSpec((1, CONV_W - 1, D_B), lambda s: (seq_of(s), 0, 0)))
    per_head = pltpu.VMEM((N_HEADS, C, D_HEAD), F32)
    scratch = [pltpu.VMEM((C, D_MODEL), F32),
               pltpu.VMEM((C, D_MODEL), F32),
               pltpu.VMEM((C, D_IN), F32),
               pltpu.VMEM((N_HEADS, D_HEAD, D_HEAD), F32),
               pltpu.VMEM((1, D_B), F32),
               pltpu.VMEM((C + 8, D_B), F32),
               per_head, per_head, per_head, per_head,
               per_head,
               pltpu.VMEM((C, D_A + D_B), BF16)]
    return pl.pallas_call(
        functools.partial(_layer_prompt_kernel, chunk=C, chunks_per_seq=nt, n_chunks=n_chunks),
        out_shape=out_shape,
        grid=(n_chunks + LAYER_FILL_STEPS,),
        in_specs=[rows_in, rows_fill_in] + ffn_specs + mix_specs + ffn_specs + [vec(D_MODEL)],
        out_specs=out_specs,
        scratch_shapes=scratch,
        compiler_params=pltpu.CompilerParams(dimension_semantics=("arbitrary",),
                                             vmem_limit_bytes=LAYER_VMEM_LIMIT_BYTES),
        name="layer_prompt",
    )(x.reshape(b * t, D_MODEL), xs, *ffn1, *mix, *ffn2, fn)


def _mixer_sample_kernel(x_ref, mn_ref, win_ref, pv_ref, wa_ref, wx_ref, wo_ref, s0_ref, rowstate_ref,
                         y_ref, s1_ref, hseq_ref, xb_out_ref,
                         proj_ref, xb_ref, kpad_ref, cpad_ref, vpad_ref, qbar_ref, khat_ref, ecum_ref,
                         opw_ref, ost_ref, ob_ref, *, steps):
    T = steps
    N = x_ref.shape[0]
    G = s0_ref.shape[0]
    R = G * T
    i = pl.program_id(0)
    tshift = T.bit_length() - 1
    lb_ref, hn_ref, cw_ref, cb_ref, ba_ref, bx_ref, lam_ref = _mixer_vectors(pv_ref)

    @pl.when(i == 0)
    def _():
        pad = kpad_ref.shape[0] - N
        zpad = jnp.zeros((pad, D_A), F32)
        kpad_ref[0:pad, :] = zpad
        cpad_ref[0:pad, :] = zpad
        vpad_ref[0:pad, :] = zpad
        xb_ref[0:8, :] = jnp.zeros((8, D_B), F32)

        xn = _rms(x_ref[...], mn_ref[...]).astype(BF16)
        proj_ref[...] = _dot(xn, win_ref[...])

        qs, k, g = _hgrn_gates(proj_ref, lb_ref)
        v = proj_ref[:, 2 * D_A:3 * D_A]
        tpos = lax.broadcasted_iota(jnp.int32, (N, 1), 0) & (T - 1)
        cum = g
        tail = jnp.zeros_like(g)
        for d in range(1, T):
            cum = cum + jnp.where(tpos >= d, pltpu.roll(g, d, 0), 0.0)
            tail = tail + jnp.where(tpos < T - d, pltpu.roll(g, N - d, 0), 0.0)
        _pairwise(qs, k, cum, v, kpad_ref, cpad_ref, vpad_ref, opw_ref, T, N)
        ecum = jnp.exp(cum)
        ecum_ref[...] = ecum
        qbar_ref[...] = qs * ecum
        khat_ref[...] = k * jnp.exp(tail)

        xb = proj_ref[:, 4 * D_A:4 * D_A + D_B]
        yb = proj_ref[:, 4 * D_A + D_B:D_IN]
        xb_ref[pl.ds(8, N), :] = xb
        xb_out_ref[...] = xb
        xc = cb_ref[...] + cw_ref[CONV_W - 1:CONV_W, :] * xb
        for d in range(1, CONV_W):
            prev = jnp.where(tpos >= d, xb_ref[pl.ds(8 - d, N), :], rowstate_ref[d])
            xc = xc + cw_ref[CONV_W - 1 - d:CONV_W - d, :] * prev
        a, mult, gate_i = _lru_coeffs(xc, wa_ref, ba_ref, wx_ref, bx_ref, lam_ref)
        a_run, h_run = _scan_rows(a, mult * gate_i * xc, tpos, T)
        hseq = h_run + a_run * rowstate_ref[0]
        hseq_ref[...] = hseq
        ob_ref[...] = hseq * _gelu_tanh(yb)

    r0 = pl.multiple_of(i * R, R)
    rows = pl.ds(r0, R)
    row8 = lax.broadcasted_iota(jnp.int32, (8, 1), 0)
    seq_of_row = lax.broadcasted_iota(jnp.int32, (R, 1), 0) >> tshift
    for h in range(N_HEADS):
        hs = slice(h * D_HEAD, (h + 1) * D_HEAD)
        khat_t = khat_ref[rows, hs].T.astype(BF16)
        dec_t = ecum_ref[rows, hs].T
        v_g = proj_ref[rows, 2 * D_A + h * D_HEAD:2 * D_A + (h + 1) * D_HEAD]
        for p in range(R // 8):
            tile = pl.ds(pl.multiple_of(r0 + 8 * p, 8), 8)
            q8 = qbar_ref[tile, hs].astype(BF16)
            o8 = jnp.zeros((8, D_HEAD), F32)
            for j in range(8 // T):
                b = p * (8 // T) + j
                s0 = s0_ref[b, h]
                ob = _dot(q8, s0.astype(BF16))
                o8 = jnp.where((row8 >= j * T) & (row8 < (j + 1) * T), ob, o8)
                v_b = jnp.where(seq_of_row == b, v_g, 0.0).astype(BF16)
                s1_ref[b, h] = s0 * dec_t[:, b * T + T - 1:b * T + T] + _dot(khat_t, v_b)
            ost_ref[tile, hs] = o8

    @pl.when(i == pl.num_programs(0) - 1)
    def _():
        gate = proj_ref[:, 3 * D_A:4 * D_A]
        outs = []
        for h in range(N_HEADS):
            hs = slice(h * D_HEAD, (h + 1) * D_HEAD)
            o = opw_ref[:, hs] + ost_ref[:, hs]
            ms = jnp.mean(o * o, axis=-1, keepdims=True)
            outs.append(o * lax.rsqrt(ms + EPS) * hn_ref[:, hs] * _sigmoid(gate[:, hs]))
        o = jnp.concatenate(outs + [ob_ref[...]], axis=-1).astype(BF16)
        y_ref[...] = x_ref[...] + _dot(o, wo_ref[...])


def _mixer_sample(x, mix, s0, h0, buf):
    nb, T, _ = x.shape
    assert T == CONV_W and 8 % T == 0, "sample kernel assumes DEC_SEQ == CONV_W == 4"
    G = SAMPLE_GROUP
    N = nb * T
    assert nb % G == 0 and (G * T) % 8 == 0
    xr = x.reshape(N, D_MODEL)
    rowstate = [jnp.broadcast_to(h0[:, None, :], (nb, T, D_B))]
    rowstate += [jnp.concatenate([buf[:, CONV_W - 1 - d:, :], jnp.zeros((nb, T - d, D_B), F32)], axis=1)
                 for d in range(1, CONV_W)]
    rowstate = jnp.stack(rowstate).reshape(CONV_W, N, D_B)
    whole_out = lambda n: pl.BlockSpec((N, n), lambda i: (0, 0))
    state = pl.BlockSpec((G, N_HEADS, D_HEAD, D_HEAD), lambda i: (i, 0, 0, 0))
    out_shape = (jax.ShapeDtypeStruct((N, D_MODEL), F32),
                 jax.ShapeDtypeStruct((nb, N_HEADS, D_HEAD, D_HEAD), F32),
                 jax.ShapeDtypeStruct((N, D_B), F32),
                 jax.ShapeDtypeStruct((N, D_B), F32))
    half = lambda: pltpu.VMEM((N, D_A), F32)
    scratch = [pltpu.VMEM((N, D_IN), F32),
               pltpu.VMEM((N + 8, D_B), F32),
               pltpu.VMEM((N + 8, D_A), F32), pltpu.VMEM((N + 8, D_A), F32), pltpu.VMEM((N + 8, D_A), F32),
               half(), half(), half(),
               half(), half(),
               pltpu.VMEM((N, D_B), F32)]
    y, s1, hseq, xb = pl.pallas_call(
        functools.partial(_mixer_sample_kernel, steps=T),
        out_shape=out_shape,
        grid=(nb // G,),
        in_specs=[_resident(xr.shape)] + [_resident(a.shape) for a in mix] + [state, _resident(rowstate.shape)],
        out_specs=(whole_out(D_MODEL), state, whole_out(D_B), whole_out(D_B)),
        scratch_shapes=scratch,
        compiler_params=pltpu.CompilerParams(dimension_semantics=("arbitrary",),
                                             vmem_limit_bytes=LAYER_VMEM_LIMIT_BYTES),
        name="mixer_sample",
    )(xr, *mix, s0, rowstate)
    hseq = hseq.reshape(nb, T, D_B)
    xb = xb.reshape(nb, T, D_B)
    return y, s1, hseq[:, T - 1], xb[:, T - (CONV_W - 1):]


def _block_diag(w):
    n, bi, bj = w.shape
    eye = jnp.eye(n, dtype=w.dtype)
    return (w[:, :, None, :] * eye[:, None, :, None]).reshape(n * bi, n * bj)


def kernel(x_prompt, x_sample, state_hgrn, state_lru, state_conv, ffn1_norm, ffn1_wg, ffn1_wu, ffn1_wd,
           mix_norm, w_in, hgrn_lb, hgrn_norm, conv_w, conv_b, lru_wa, lru_ba, lru_wx, lru_bx, lru_lambda,
           w_o, ffn2_norm, ffn2_wg, ffn2_wu, ffn2_wd, final_norm):
    assert ffn1_norm.shape[0] == 1 and hgrn_lb.shape[0] == 2, "single-layer model"
    bp, tp, _ = x_prompt.shape
    bs, ts, _ = x_sample.shape

    ffn1 = (ffn1_norm, ffn1_wg[0].astype(BF16), ffn1_wu[0].astype(BF16), ffn1_wd[0].astype(BF16))
    ffn2 = (ffn2_norm, ffn2_wg[0].astype(BF16), ffn2_wu[0].astype(BF16), ffn2_wd[0].astype(BF16))
    fn = final_norm.reshape(1, D_MODEL)
    vectors = jnp.concatenate([hgrn_lb, hgrn_norm, conv_w[0], conv_b, lru_ba[0].reshape(1, D_B),
                               lru_bx[0].reshape(1, D_B), lru_lambda], axis=0)
    mix = (mix_norm, w_in[0].astype(BF16), vectors, _block_diag(lru_wa[0]).astype(BF16),
           _block_diag(lru_wx[0]).astype(BF16), w_o[0].astype(BF16))

    xs = _ffn(x_sample.reshape(bs * ts, D_MODEL), *ffn1, name="ffn1_sample")
    xs, hg_s, lru_s, cv_s = _mixer_sample(xs.reshape(bs, ts, D_MODEL), mix, state_hgrn[0], state_lru[0],
                                          state_conv[0])

    yp, ys, hg_p, lru_p, cv_p = _layer_prompt(x_prompt, xs, ffn1, mix, ffn2, fn)

    return (yp.reshape(bp, tp, D_MODEL), ys.reshape(bs, ts, D_MODEL),
            hg_p[None], lru_p.reshape(1, bp, D_B), cv_p[None],
            hg_s[None], lru_s[None], cv_s[None])
```

```python
import functools
import math

import jax
import jax.numpy as jnp
from jax import lax
from jax.experimental import pallas as pl
from jax.experimental.pallas import tpu as pltpu

F32 = jnp.float32
BF16 = jnp.bfloat16

D_MODEL = 1024
D_FF = 2816
D_A = 512
D_B = 512
N_HEADS = 4
D_HEAD = 128
D_IN = 4 * D_A + 2 * D_B
CONV_W = 4
LRU_C = 8.0
EPS = 1e-6

SUBLANES = 8
GATE_TILE = 256
NEG_LARGE = -1e30
MIX_CHUNK = 256
FFN_CHUNK = 256
FFN_BOUNDS = (0, 768, 1536, 2304, 2816)
LAYER_FILL_SLOTS = (1, 2, 2, 2, 2, 2, 2, 3)
LAYER_FILL_STEPS = 2
SAMPLE_GROUP = 8
VMEM_LIMIT_BYTES = 56 * 1024 * 1024
LAYER_VMEM_LIMIT_BYTES = 63 * 1024 * 1024

_NT = (((1,), (1,)), ((), ()))


def _dot(a, b):
    return jnp.dot(a, b, preferred_element_type=F32)


def _dot_nt(a, b):
    return lax.dot_general(a, b, _NT, preferred_element_type=F32)


def _rms(x, w):
    ms = jnp.mean(x * x, axis=-1, keepdims=True)
    return x * lax.rsqrt(ms + EPS) * w


def _sigmoid(x):
    return 1.0 / (1.0 + jnp.exp(-x))


def _log1p(x):
    u = 1.0 + x
    return jnp.where(u == 1.0, x, jnp.log(u) * x / (u - 1.0))


def _expm1(x):
    u = jnp.exp(x)
    return jnp.where(u == 1.0, x, (u - 1.0) * x / jnp.log(u))


def _softplus(x):
    return jnp.maximum(x, 0.0) + _log1p(jnp.exp(-jnp.abs(x)))


def _gelu_tanh(x):
    return 0.5 * x * (1.0 + jnp.tanh(math.sqrt(2.0 / math.pi) * (x + 0.044715 * (x * x * x))))


def _split3_bf16(x):
    x1 = x.astype(BF16)
    r1 = x - x1.astype(F32)
    x2 = r1.astype(BF16)
    r2 = r1 - x2.astype(F32)
    return x1, x2, r2.astype(BF16)


def _ffn_kernel(x_ref, nw_ref, wg_ref, wu_ref, wd_ref, o_ref, wgb_ref, wub_ref, wdb_ref, xn_ref, acc_ref):
    c = pl.program_id(0)

    @pl.when(c == 0)
    def _():
        xn_ref[...] = _rms(x_ref[...], nw_ref[...]).astype(BF16)
        acc_ref[...] = jnp.zeros(acc_ref.shape, F32)

    wg = wg_ref[...].astype(BF16)
    wu = wu_ref[...].astype(BF16)
    wd = wd_ref[...].astype(BF16)
    wgb_ref[...] = wg
    wub_ref[...] = wu
    wdb_ref[...] = wd
    xn = xn_ref[...]
    g = _dot(xn, wg)
    u = _dot(xn, wu)
    acc_ref[...] += _dot((g * _sigmoid(g) * u).astype(BF16), wd)

    @pl.when(c == pl.num_programs(0) - 1)
    def _():
        o_ref[...] = x_ref[...] + 0.5 * acc_ref[...]


def _resident(shape):
    zeros = (0,) * len(shape)
    return pl.BlockSpec(shape, lambda *_: zeros, pipeline_mode=pl.Buffered(1))


def _ffn_and_round_weights(x, norm_w, wg, wu, wd, *, name):
    n = x.shape[0]
    assert D_FF % FFN_CHUNK == 0
    cols = pl.BlockSpec((D_MODEL, FFN_CHUNK), lambda c: (0, c))
    rows = pl.BlockSpec((FFN_CHUNK, D_MODEL), lambda c: (c, 0))
    return pl.pallas_call(
        _ffn_kernel,
        out_shape=(jax.ShapeDtypeStruct((n, D_MODEL), F32), jax.ShapeDtypeStruct(wg.shape, BF16),
                   jax.ShapeDtypeStruct(wu.shape, BF16), jax.ShapeDtypeStruct(wd.shape, BF16)),
        grid=(D_FF // FFN_CHUNK,),
        in_specs=[_resident((n, D_MODEL)), _resident((1, D_MODEL)), cols, cols, rows],
        out_specs=(pl.BlockSpec((n, D_MODEL), lambda c: (0, 0)), cols, cols, rows),
        scratch_shapes=[pltpu.VMEM((n, D_MODEL), BF16), pltpu.VMEM((n, D_MODEL), F32)],
        compiler_params=pltpu.CompilerParams(dimension_semantics=("arbitrary",),
                                             vmem_limit_bytes=VMEM_LIMIT_BYTES),
        name=name,
    )(x, norm_w, wg, wu, wd)


class _Rows:
    def __init__(self, ref, start, n):
        self.ref, self.start, self.n = ref, start, n

    def __getitem__(self, idx):
        rows, cols = (slice(None), slice(None)) if idx is Ellipsis else idx
        lo = self.start + (rows.start or 0)
        hi = self.start + (self.n if rows.stop is None else rows.stop)
        return self.ref[lo:hi, cols]


def _mixer_vectors(pv_ref):
    sizes = (2, 1, CONV_W, 1, 1, 1, 1)
    views, start = [], 0
    for n in sizes:
        views.append(_Rows(pv_ref, start, n))
        start += n
    return views


def _hgrn_gates(proj_ref, lb_ref):
    l0 = lb_ref[0:1, :]
    l1 = lb_ref[1:2, :]
    m = jnp.maximum(l0, l1)
    e0 = jnp.exp(l0 - m)
    e1 = jnp.exp(l1 - m)
    lb = e0 / (e0 + e1)
    q = proj_ref[:, 0:D_A]
    fz = proj_ref[:, D_A:2 * D_A]
    f = lb + (1.0 - lb) * _sigmoid(fz)
    return q * _sigmoid(q), 1.0 - f, jnp.log(f)


def _pairwise(qs, k, cum, v, kpad_ref, cpad_ref, vpad_ref, out_ref, block, n_rows):
    pad = kpad_ref.shape[0] - n_rows
    kpad_ref[pl.ds(pad, n_rows), :] = k
    cpad_ref[pl.ds(pad, n_rows), :] = cum
    vpad_ref[pl.ds(pad, n_rows), :] = v
    rmod = lax.broadcasted_iota(jnp.int32, (n_rows, 1), 0) & (block - 1)
    out = [jnp.zeros((n_rows, D_HEAD), F32) for _ in range(N_HEADS)]
    for d in range(block):
        if d == 0:
            p = qs * k
            vs = v
        else:
            ks = kpad_ref[pl.ds(pad - d, n_rows), :]
            cs = cpad_ref[pl.ds(pad - d, n_rows), :]
            vs = vpad_ref[pl.ds(pad - d, n_rows), :]
            p = qs * ks * jnp.exp(cum - cs)
        valid = rmod >= d
        for h in range(N_HEADS):
            hs = slice(h * D_HEAD, (h + 1) * D_HEAD)
            s = jnp.sum(p[:, hs], axis=-1, keepdims=True)
            s = jnp.where(valid, s, 0.0)
            out[h] = out[h] + s * vs[:, hs]
    for h in range(N_HEADS):
        out_ref[:, h * D_HEAD:(h + 1) * D_HEAD] = out[h]


def _lru_coeffs(xc, wa_ref, ba_ref, wx_ref, bx_ref, lam_ref):
    xcb = xc.astype(BF16)

    def gate(w_ref, b_ref):
        tiles = [slice(j, j + GATE_TILE) for j in range(0, D_B, GATE_TILE)]
        z = jnp.concatenate([_dot(xcb[:, t], w_ref[t, t]) for t in tiles], axis=-1)
        return _sigmoid(z + b_ref[...])

    r = gate(wa_ref, ba_ref)
    i = gate(wx_ref, bx_ref)
    log_a = -LRU_C * r * _softplus(-lam_ref[...])
    a = jnp.exp(log_a)
    mult = jnp.sqrt(-_expm1(2.0 * log_a))
    return a, mult, i


def _scan_rows(a, u, rowpos, span):
    s = 1
    while s < span:
        keep = rowpos >= s
        a_sh = jnp.where(keep, pltpu.roll(a, s, 0), 1.0)
        u_sh = jnp.where(keep, pltpu.roll(u, s, 0), 0.0)
        u = a * u_sh + u
        a = a * a_sh
        s *= 2
    return a, u


def _row_bcast(ref, h, r, n):
    return jnp.broadcast_to(ref[h, pl.ds(r, 1), :], (n, D_HEAD))


def _pairwise_tile(qs_ref, k_ref, cum_ref, v_ref, out_ref, h, n_rows):
    pos = lax.broadcasted_iota(jnp.int32, (SUBLANES, 1), 0)
    for j in range(n_rows // SUBLANES):
        rows = pl.ds(SUBLANES * j, SUBLANES)
        q_t = qs_ref[h, rows, :]
        c_t = cum_ref[h, rows, :]
        acc = jnp.zeros((SUBLANES, D_HEAD), F32)
        for s in range(SUBLANES):
            src = SUBLANES * j + s
            d = c_t - _row_bcast(cum_ref, h, src, SUBLANES)
            if s > 0:
                d = jnp.where(pos >= s, d, NEG_LARGE)
            p = q_t * _row_bcast(k_ref, h, src, SUBLANES) * jnp.exp(d)
            acc = acc + jnp.sum(p, axis=-1, keepdims=True) * _row_bcast(v_ref, h, src, SUBLANES)
        out_ref[h, rows, :] = acc


def _level_operands(qs_ref, k_ref, cum_ref, h, n_rows, L):
    half = L // 2
    zeros = jnp.zeros((half, D_HEAD), F32)
    qparts, kparts = [], []
    for b in range(n_rows // L):
        r0 = b * L
        first = pl.ds(r0, half)
        second = pl.ds(r0 + half, half)
        mid = _row_bcast(cum_ref, h, r0 + half - 1, half)
        kparts += [k_ref[h, first, :] * jnp.exp(mid - cum_ref[h, first, :]), zeros]
        qparts += [zeros, qs_ref[h, second, :] * jnp.exp(cum_ref[h, second, :] - mid)]
    return jnp.concatenate(qparts, axis=0).astype(BF16), jnp.concatenate(kparts, axis=0).astype(BF16)


def _scan_tiles(a, u, carry):
    n = a.shape[0]
    a3 = a.reshape(n // SUBLANES, SUBLANES, D_B)
    u3 = u.reshape(n // SUBLANES, SUBLANES, D_B)
    pos = lax.broadcasted_iota(jnp.int32, (1, SUBLANES, 1), 1)
    s = 1
    while s < SUBLANES:
        keep = pos >= s
        a_sh = jnp.where(keep, pltpu.roll(a3, s, 1), 1.0)
        u_sh = jnp.where(keep, pltpu.roll(u3, s, 1), 0.0)
        u3 = a3 * u_sh + u3
        a3 = a3 * a_sh
        s *= 2
    tiles = []
    for j in range(n // SUBLANES):
        hj = u3[j] + a3[j] * carry
        tiles.append(hj)
        carry = hj[SUBLANES - 1:SUBLANES, :]
    return jnp.concatenate(tiles, axis=0)


class _SwigluChunks:
    def __init__(self, x, nw_ref, wg_ref, wu_ref, wd_ref):
        self.x = x
        self.xn = _rms(x, nw_ref[...]).astype(BF16)
        self.wg_ref, self.wu_ref, self.wd_ref = wg_ref, wu_ref, wd_ref
        self.hidden = {}
        self.acc = None

    @staticmethod
    def _cols(c):
        return slice(FFN_BOUNDS[c], FFN_BOUNDS[c + 1])

    def up(self, c):
        g = _dot(self.xn, self.wg_ref[:, self._cols(c)])
        u = _dot(self.xn, self.wu_ref[:, self._cols(c)])
        self.hidden[c] = (g * _sigmoid(g) * u).astype(BF16)

    def down(self, c):
        d = _dot(self.hidden.pop(c), self.wd_ref[self._cols(c), :])
        self.acc = d if self.acc is None else self.acc + d

    def result(self):
        assert not self.hidden
        return self.x + 0.5 * self.acc

    def pieces(self):
        n = len(FFN_BOUNDS) - 1
        out = [functools.partial(self.up, 0)]
        for c in range(1, n):
            out += [functools.partial(self.up, c), functools.partial(self.down, c - 1)]
        return out + [functools.partial(self.down, n - 1)]


def _layer_prompt_kernel(x_ref, xs_ref, n1_ref, wg1_ref, wu1_ref, wd1_ref,
                         mn_ref, win_ref, pv_ref, wa_ref, wx_ref, wo_ref,
                         n2_ref, wg2_ref, wu2_ref, wd2_ref, fn_ref,
                         y_ref, ys_ref, shg_ref, slru_ref, scv_ref,
                         x1_ref, x2_ref, proj_ref, st_ref, h_ref, xb_ref, qs_ref, k_ref, cum_ref, v_ref,
                         opw_ref, obuf_ref, *, chunk, chunks_per_seq, n_chunks):
    C = chunk
    s = pl.program_id(0)
    lb_ref, hn_ref, cw_ref, cb_ref, ba_ref, bx_ref, lam_ref = _mixer_vectors(pv_ref)

    @pl.when(s == 0)
    def _():
        x1_ref[...] = jnp.zeros(x1_ref.shape, F32)
        st_ref[...] = jnp.zeros(st_ref.shape, F32)
        h_ref[...] = jnp.zeros(h_ref.shape, F32)
        xb_ref[0:8, :] = jnp.zeros((8, D_B), F32)

    @pl.when(s < LAYER_FILL_STEPS)
    def _():
        x2_ref[...] = xs_ref[...]

    ffn2 = _SwigluChunks(x2_ref[...], n2_ref, wg2_ref, wu2_ref, wd2_ref)
    ffn1 = _SwigluChunks(x_ref[...], n1_ref, wg1_ref, wu1_ref, wd1_ref)

    pos_in_seq = lax.rem(s + (chunks_per_seq - 1), chunks_per_seq)
    first = pos_in_seq == 0
    keep = jnp.where(first, 0.0, 1.0)
    x = x1_ref[...]
    xn = _rms(x, mn_ref[...]).astype(BF16)
    proj_ref[...] = _dot(xn, win_ref[...])
    fill = ffn2.pieces() + ffn1.pieces()
    slots = iter(LAYER_FILL_SLOTS)

    def run_fill():
        for _ in range(next(slots)):
            fill.pop(0)()

    run_fill()

    qs, k, g = _hgrn_gates(proj_ref, lb_ref)
    row = lax.broadcasted_iota(jnp.int32, (C, C), 0)
    col = lax.broadcasted_iota(jnp.int32, (C, C), 1)
    tri = jnp.where(col <= row, 1.0, 0.0).astype(BF16)
    g1, g2, g3 = _split3_bf16(g)
    cum = _dot(tri, g1) + _dot(tri, g2) + _dot(tri, g3)
    for h in range(N_HEADS):
        hs = slice(h * D_HEAD, (h + 1) * D_HEAD)
        qs_ref[h] = qs[:, hs]
        k_ref[h] = k[:, hs]
        cum_ref[h] = cum[:, hs]
        v_ref[h] = proj_ref[:, 2 * D_A + h * D_HEAD:2 * D_A + (h + 1) * D_HEAD]

    levels = []
    L = 2 * SUBLANES
    while L <= C:
        shift = L.bit_length() - 1
        levels.append((L, None if L == C else (row >> shift) == (col >> shift)))
        L *= 2

    gate = proj_ref[:, 3 * D_A:4 * D_A]
    for h in range(N_HEADS):
        hs = slice(h * D_HEAD, (h + 1) * D_HEAD)
        run_fill()
        _pairwise_tile(qs_ref, k_ref, cum_ref, v_ref, opw_ref, h, C)
        scores = jnp.zeros((C, C), F32)
        for L, same in levels:
            qt, kt = _level_operands(qs_ref, k_ref, cum_ref, h, C, L)
            sc = _dot_nt(qt, kt)
            scores = scores + (sc if same is None else jnp.where(same, sc, 0.0))
        cum_h = cum_ref[h]
        cl = _row_bcast(cum_ref, h, C - 1, C)
        qbar = (qs_ref[h] * jnp.exp(cum_h)).astype(BF16)
        khat = (k_ref[h] * jnp.exp(cl - cum_h)).astype(BF16)
        dec = jnp.exp(cum_ref[h, pl.ds(C - 1, 1), :])
        v_h = v_ref[h]
        st = st_ref[h] * keep
        o = _dot(scores.astype(BF16), v_h.astype(BF16)) + _dot_nt(qbar, st.astype(BF16)) + opw_ref[h]
        st_ref[h] = st * dec + _dot(v_h.T.astype(BF16), khat)
        ms = jnp.mean(o * o, axis=-1, keepdims=True)
        obuf_ref[:, hs] = (o * lax.rsqrt(ms + EPS) * hn_ref[:, hs] * _sigmoid(gate[:, hs])).astype(BF16)

    run_fill()
    y_ref[...] = _rms(ffn2.result(), fn_ref[...])

    rowi = lax.broadcasted_iota(jnp.int32, (C, 1), 0)
    xb = proj_ref[:, 4 * D_A:4 * D_A + D_B]
    yb = proj_ref[:, 4 * D_A + D_B:D_IN]
    xb_ref[0:8, :] = xb_ref[0:8, :] * keep
    xb_ref[pl.ds(8, C), :] = xb
    xc = (cb_ref[...] + cw_ref[3:4, :] * xb + cw_ref[2:3, :] * xb_ref[pl.ds(7, C), :]
          + cw_ref[1:2, :] * xb_ref[pl.ds(6, C), :] + cw_ref[0:1, :] * xb_ref[pl.ds(5, C), :])
    xb_ref[0:8, :] = xb[C - 8:C, :]
    a, mult, gate_i = _lru_coeffs(xc, wa_ref, ba_ref, wx_ref, bx_ref, lam_ref)
    run_fill()
    mult = jnp.where((rowi == 0) & first, 1.0, mult)
    hseq = _scan_tiles(a, mult * gate_i * xc, h_ref[...] * keep)
    h_ref[...] = hseq[C - 1:C, :]
    obuf_ref[:, D_A:D_A + D_B] = (hseq * _gelu_tanh(yb)).astype(BF16)

    x2_ref[...] = x + _dot(obuf_ref[...], wo_ref[...])
    run_fill()
    assert not fill and next(slots, None) is None
    x1_ref[...] = ffn1.result()

    @pl.when(s < LAYER_FILL_STEPS)
    def _():
        ys_ref[...] = y_ref[...]

    @pl.when((pos_in_seq == chunks_per_seq - 1) & (s >= 1) & (s <= n_chunks))
    def _():
        for h in range(N_HEADS):
            shg_ref[0, h] = st_ref[h].T
        slru_ref[0] = hseq[C - 1:C, :]
        scv_ref[0] = xb[C - (CONV_W - 1):C, :]


def _layer_prompt(x, xs, ffn1, mix, ffn2, fn):
    b, t, _ = x.shape
    C = MIX_CHUNK
    assert t % C == 0 and C % (2 * SUBLANES) == 0 and C & (C - 1) == 0
    assert FFN_BOUNDS[0] == 0 and FFN_BOUNDS[-1] == D_FF
    assert sum(LAYER_FILL_SLOTS) == 4 * (len(FFN_BOUNDS) - 1) and len(LAYER_FILL_SLOTS) == N_HEADS + 4
    assert xs.shape == (LAYER_FILL_STEPS * C, D_MODEL), "sample rows must exactly fill the pipeline fill steps"
    nt = t // C
    n_chunks = b * nt
    rows_in = pl.BlockSpec((C, D_MODEL), lambda s: (jnp.minimum(s, n_chunks - 1), 0))
    rows_out = pl.BlockSpec((C, D_MODEL), lambda s: (jnp.clip(s - 2, 0, n_chunks - 1), 0))
    fill_index = lambda s: (jnp.minimum(s, LAYER_FILL_STEPS - 1), 0)
    rows_fill = pl.BlockSpec((C, D_MODEL), fill_index)
    rows_fill_in = pl.BlockSpec((C, D_MODEL), fill_index, pipeline_mode=pl.Buffered(1))
    seq_of = lambda s: jnp.clip((s - 1) // nt, 0, b - 1)
    vec = lambda n: _resident((1, n))
    ffn_specs = [vec(D_MODEL), _resident((D_MODEL, D_FF)), _resident((D_MODEL, D_FF)), _resident((D_FF, D_MODEL))]
    mix_specs = [_resident(a.shape) for a in mix]
    out_shape = (jax.ShapeDtypeStruct((b * t, D_MODEL), F32),
                 jax.ShapeDtypeStruct(xs.shape, F32),
                 jax.ShapeDtypeStruct((b, N_HEADS, D_HEAD, D_HEAD), F32),
                 jax.ShapeDtypeStruct((b, 1, D_B), F32),
                 jax.ShapeDtypeStruct((b, CONV_W - 1, D_B), F32))
    out_specs = (rows_out, rows_fill,
                 pl.BlockSpec((1, N_HEADS, D_HEAD, D_HEAD), lambda s: (seq_of(s), 0, 0, 0)),
                 pl.BlockSpec((1, 1, D_B), lambda s: (seq_of(s), 0, 0)),
                 pl.BlockSpec((1, CONV_W - 1, D_B), lambda s: (seq_of(s), 0, 0)))
    per_head = pltpu.VMEM((N_HEADS, C, D_HEAD), F32)
    scratch = [pltpu.VMEM((C, D_MODEL), F32),
               pltpu.VMEM((C, D_MODEL), F32),
               pltpu.VMEM((C, D_IN), F32),
               pltpu.VMEM((N_HEADS, D_HEAD, D_HEAD), F32),
               pltpu.VMEM((1, D_B), F32),
               pltpu.VMEM((C + 8, D_B), F32),
               per_head, per_head, per_head, per_head,
               per_head,
               pltpu.VMEM((C, D_A + D_B), BF16)]
    return pl.pallas_call(
        functools.partial(_layer_prompt_kernel, chunk=C, chunks_per_seq=nt, n_chunks=n_chunks),
        out_shape=out_shape,
        grid=(n_chunks + LAYER_FILL_STEPS,),
        in_specs=[rows_in, rows_fill_in] + ffn_specs + mix_specs + ffn_specs + [vec(D_MODEL)],
        out_specs=out_specs,
        scratch_shapes=scratch,
        compiler_params=pltpu.CompilerParams(dimension_semantics=("arbitrary",),
                                             vmem_limit_bytes=LAYER_VMEM_LIMIT_BYTES),
        name="layer_prompt",
    )(x.reshape(b * t, D_MODEL), xs, *ffn1, *mix, *ffn2, fn)


def _mixer_sample_kernel(x_ref, mn_ref, win_ref, pv_ref, wa_ref, wx_ref, wo_ref, s0_ref, rowstate_ref,
                         y_ref, s1_ref, hseq_ref, xb_out_ref,
                         proj_ref, xb_ref, kpad_ref, cpad_ref, vpad_ref, qbar_ref, khat_ref, ecum_ref,
                         opw_ref, ost_ref, ob_ref, *, steps):
    T = steps
    N = x_ref.shape[0]
    G = s0_ref.shape[0]
    R = G * T
    i = pl.program_id(0)
    tshift = T.bit_length() - 1
    lb_ref, hn_ref, cw_ref, cb_ref, ba_ref, bx_ref, lam_ref = _mixer_vectors(pv_ref)

    @pl.when(i == 0)
    def _():
        pad = kpad_ref.shape[0] - N
        zpad = jnp.zeros((pad, D_A), F32)
        kpad_ref[0:pad, :] = zpad
        cpad_ref[0:pad, :] = zpad
        vpad_ref[0:pad, :] = zpad
        xb_ref[0:8, :] = jnp.zeros((8, D_B), F32)

        xn = _rms(x_ref[...], mn_ref[...]).astype(BF16)
        proj_ref[...] = _dot(xn, win_ref[...])

        qs, k, g = _hgrn_gates(proj_ref, lb_ref)
        v = proj_ref[:, 2 * D_A:3 * D_A]
        tpos = lax.broadcasted_iota(jnp.int32, (N, 1), 0) & (T - 1)
        cum = g
        tail = jnp.zeros_like(g)
        for d in range(1, T):
            cum = cum + jnp.where(tpos >= d, pltpu.roll(g, d, 0), 0.0)
            tail = tail + jnp.where(tpos < T - d, pltpu.roll(g, N - d, 0), 0.0)
        _pairwise(qs, k, cum, v, kpad_ref, cpad_ref, vpad_ref, opw_ref, T, N)
        ecum = jnp.exp(cum)
        ecum_ref[...] = ecum
        qbar_ref[...] = qs * ecum
        khat_ref[...] = k * jnp.exp(tail)

        xb = proj_ref[:, 4 * D_A:4 * D_A + D_B]
        yb = proj_ref[:, 4 * D_A + D_B:D_IN]
        xb_ref[pl.ds(8, N), :] = xb
        xb_out_ref[...] = xb
        xc = cb_ref[...] + cw_ref[CONV_W - 1:CONV_W, :] * xb
        for d in range(1, CONV_W):
            prev = jnp.where(tpos >= d, xb_ref[pl.ds(8 - d, N), :], rowstate_ref[d])
            xc = xc + cw_ref[CONV_W - 1 - d:CONV_W - d, :] * prev
        a, mult, gate_i = _lru_coeffs(xc, wa_ref, ba_ref, wx_ref, bx_ref, lam_ref)
        a_run, h_run = _scan_rows(a, mult * gate_i * xc, tpos, T)
        hseq = h_run + a_run * rowstate_ref[0]
        hseq_ref[...] = hseq
        ob_ref[...] = hseq * _gelu_tanh(yb)

    r0 = pl.multiple_of(i * R, R)
    rows = pl.ds(r0, R)
    row8 = lax.broadcasted_iota(jnp.int32, (8, 1), 0)
    seq_of_row = lax.broadcasted_iota(jnp.int32, (R, 1), 0) >> tshift
    for h in range(N_HEADS):
        hs = slice(h * D_HEAD, (h + 1) * D_HEAD)
        khat_t = khat_ref[rows, hs].T.astype(BF16)
        dec_t = ecum_ref[rows, hs].T
        v_g = proj_ref[rows, 2 * D_A + h * D_HEAD:2 * D_A + (h + 1) * D_HEAD]
        for p in range(R // 8):
            tile = pl.ds(pl.multiple_of(r0 + 8 * p, 8), 8)
            q8 = qbar_ref[tile, hs].astype(BF16)
            o8 = jnp.zeros((8, D_HEAD), F32)
            for j in range(8 // T):
                b = p * (8 // T) + j
                s0 = s0_ref[b, h]
                ob = _dot(q8, s0.astype(BF16))
                o8 = jnp.where((row8 >= j * T) & (row8 < (j + 1) * T), ob, o8)
                v_b = jnp.where(seq_of_row == b, v_g, 0.0).astype(BF16)
                s1_ref[b, h] = s0 * dec_t[:, b * T + T - 1:b * T + T] + _dot(khat_t, v_b)
            ost_ref[tile, hs] = o8

    @pl.when(i == pl.num_programs(0) - 1)
    def _():
        gate = proj_ref[:, 3 * D_A:4 * D_A]
        outs = []
        for h in range(N_HEADS):
            hs = slice(h * D_HEAD, (h + 1) * D_HEAD)
            o = opw_ref[:, hs] + ost_ref[:, hs]
            ms = jnp.mean(o * o, axis=-1, keepdims=True)
            outs.append(o * lax.rsqrt(ms + EPS) * hn_ref[:, hs] * _sigmoid(gate[:, hs]))
        o = jnp.concatenate(outs + [ob_ref[...]], axis=-1).astype(BF16)
        y_ref[...] = x_ref[...] + _dot(o, wo_ref[...])


def _mixer_sample(xr, mix, s0, h0, buf):
    nb, T = s0.shape[0], xr.shape[0] // s0.shape[0]
    assert T == CONV_W and 8 % T == 0, "sample kernel assumes DEC_SEQ == CONV_W == 4"
    G = SAMPLE_GROUP
    N = nb * T
    assert nb % G == 0 and (G * T) % 8 == 0
    rowstate = [jnp.broadcast_to(h0[:, None, :], (nb, T, D_B))]
    rowstate += [jnp.concatenate([buf[:, CONV_W - 1 - d:, :], jnp.zeros((nb, T - d, D_B), F32)], axis=1)
                 for d in range(1, CONV_W)]
    rowstate = jnp.stack(rowstate).reshape(CONV_W, N, D_B)
    whole_out = lambda n: pl.BlockSpec((N, n), lambda i: (0, 0))
    state = pl.BlockSpec((G, N_HEADS, D_HEAD, D_HEAD), lambda i: (i, 0, 0, 0))
    out_shape = (jax.ShapeDtypeStruct((N, D_MODEL), F32),
                 jax.ShapeDtypeStruct((nb, N_HEADS, D_HEAD, D_HEAD), F32),
                 jax.ShapeDtypeStruct((N, D_B), F32),
                 jax.ShapeDtypeStruct((N, D_B), F32))
    half = lambda: pltpu.VMEM((N, D_A), F32)
    scratch = [pltpu.VMEM((N, D_IN), F32),
               pltpu.VMEM((N + 8, D_B), F32),
               pltpu.VMEM((N + 8, D_A), F32), pltpu.VMEM((N + 8, D_A), F32), pltpu.VMEM((N + 8, D_A), F32),
               half(), half(), half(),
               half(), half(),
               pltpu.VMEM((N, D_B), F32)]
    y, s1, hseq, xb = pl.pallas_call(
        functools.partial(_mixer_sample_kernel, steps=T),
        out_shape=out_shape,
        grid=(nb // G,),
        in_specs=[_resident(xr.shape)] + [_resident(a.shape) for a in mix] + [state, _resident(rowstate.shape)],
        out_specs=(whole_out(D_MODEL), state, whole_out(D_B), whole_out(D_B)),
        scratch_shapes=scratch,
        compiler_params=pltpu.CompilerParams(dimension_semantics=("arbitrary",),
                                             vmem_limit_bytes=LAYER_VMEM_LIMIT_BYTES),
        name="mixer_sample",
    )(xr, *mix, s0, rowstate)
    hseq = hseq.reshape(nb, T, D_B)
    xb = xb.reshape(nb, T, D_B)
    return y, s1, hseq[:, T - 1], xb[:, T - (CONV_W - 1):]


def _block_diag(w):
    n, bi, bj = w.shape
    eye = jnp.eye(n, dtype=w.dtype)
    return (w[:, :, None, :] * eye[:, None, :, None]).reshape(n * bi, n * bj)


def kernel(x_prompt, x_sample, state_hgrn, state_lru, state_conv, ffn1_norm, ffn1_wg, ffn1_wu, ffn1_wd,
           mix_norm, w_in, hgrn_lb, hgrn_norm, conv_w, conv_b, lru_wa, lru_ba, lru_wx, lru_bx, lru_lambda,
           w_o, ffn2_norm, ffn2_wg, ffn2_wu, ffn2_wd, final_norm):
    assert ffn1_norm.shape[0] == 1 and hgrn_lb.shape[0] == 2, "single-layer model"
    bp, tp, _ = x_prompt.shape
    bs, ts, _ = x_sample.shape

    ffn2 = (ffn2_norm, ffn2_wg[0].astype(BF16), ffn2_wu[0].astype(BF16), ffn2_wd[0].astype(BF16))
    fn = final_norm.reshape(1, D_MODEL)
    vectors = jnp.concatenate([hgrn_lb, hgrn_norm, conv_w[0], conv_b, lru_ba[0].reshape(1, D_B),
                               lru_bx[0].reshape(1, D_B), lru_lambda], axis=0)
    mix = (mix_norm, w_in[0].astype(BF16), vectors, _block_diag(lru_wa[0]).astype(BF16),
           _block_diag(lru_wx[0]).astype(BF16), w_o[0].astype(BF16))

    xs, *ffn1_bf16 = _ffn_and_round_weights(x_sample.reshape(bs * ts, D_MODEL), ffn1_norm, ffn1_wg[0],
                                            ffn1_wu[0], ffn1_wd[0], name="ffn1_sample")
    ffn1 = (ffn1_norm, *ffn1_bf16)
    xs, hg_s, lru_s, cv_s = _mixer_sample(xs, mix, state_hgrn[0], state_lru[0], state_conv[0])

    yp, ys, hg_p, lru_p, cv_p = _layer_prompt(x_prompt, xs, ffn1, mix, ffn2, fn)

    return (yp.reshape(bp, tp, D_MODEL), ys.reshape(bs, ts, D_MODEL),
            hg_p[None], lru_p.reshape(1, bp, D_B), cv_p[None],
            hg_s[None], lru_s[None], cv_s[None])
```

```python
import functools
import math

import jax
import jax.numpy as jnp
from jax import lax
from jax.experimental import pallas as pl
from jax.experimental.pallas import tpu as pltpu

F32 = jnp.float32
BF16 = jnp.bfloat16

D_MODEL = 1024
D_FF = 2816
D_A = 512
D_B = 512
N_HEADS = 4
D_HEAD = 128
D_IN = 4 * D_A + 2 * D_B
CONV_W = 4
LRU_C = 8.0
EPS = 1e-6

SUBLANES = 8
GATE_TILE = 256
NEG_LARGE = -1e30
MIX_CHUNK = 256
FFN_CHUNK = 256
FFN_BOUNDS = (0, 768, 1536, 2304, 2816)
LAYER_FILL_SLOTS = (1, 2, 2, 2, 2, 2, 2, 3)
LAYER_FILL_STEPS = 2
SAMPLE_GROUP = 16
VMEM_LIMIT_BYTES = 56 * 1024 * 1024
LAYER_VMEM_LIMIT_BYTES = 63 * 1024 * 1024

_NT = (((1,), (1,)), ((), ()))


def _dot(a, b):
    return jnp.dot(a, b, preferred_element_type=F32)


def _dot_nt(a, b):
    return lax.dot_general(a, b, _NT, preferred_element_type=F32)


def _rms(x, w):
    ms = jnp.mean(x * x, axis=-1, keepdims=True)
    return x * lax.rsqrt(ms + EPS) * w


def _sigmoid(x):
    return 1.0 / (1.0 + jnp.exp(-x))


def _log1p(x):
    u = 1.0 + x
    return jnp.where(u == 1.0, x, jnp.log(u) * x / (u - 1.0))


def _expm1(x):
    u = jnp.exp(x)
    return jnp.where(u == 1.0, x, (u - 1.0) * x / jnp.log(u))


def _softplus(x):
    return jnp.maximum(x, 0.0) + _log1p(jnp.exp(-jnp.abs(x)))


def _gelu_tanh(x):
    return 0.5 * x * (1.0 + jnp.tanh(math.sqrt(2.0 / math.pi) * (x + 0.044715 * (x * x * x))))


def _split3_bf16(x):
    x1 = x.astype(BF16)
    r1 = x - x1.astype(F32)
    x2 = r1.astype(BF16)
    r2 = r1 - x2.astype(F32)
    return x1, x2, r2.astype(BF16)


def _ffn_kernel(x_ref, nw_ref, wg_ref, wu_ref, wd_ref, o_ref, wgb_ref, wub_ref, wdb_ref, xn_ref, acc_ref):
    c = pl.program_id(0)

    @pl.when(c == 0)
    def _():
        xn_ref[...] = _rms(x_ref[...], nw_ref[...]).astype(BF16)
        acc_ref[...] = jnp.zeros(acc_ref.shape, F32)

    wg = wg_ref[...].astype(BF16)
    wu = wu_ref[...].astype(BF16)
    wd = wd_ref[...].astype(BF16)
    wgb_ref[...] = wg
    wub_ref[...] = wu
    wdb_ref[...] = wd
    xn = xn_ref[...]
    g = _dot(xn, wg)
    u = _dot(xn, wu)
    acc_ref[...] += _dot((g * _sigmoid(g) * u).astype(BF16), wd)

    @pl.when(c == pl.num_programs(0) - 1)
    def _():
        o_ref[...] = x_ref[...] + 0.5 * acc_ref[...]


def _resident(shape):
    zeros = (0,) * len(shape)
    return pl.BlockSpec(shape, lambda *_: zeros, pipeline_mode=pl.Buffered(1))


def _ffn_and_round_weights(x, norm_w, wg, wu, wd, *, name):
    n = x.shape[0]
    assert D_FF % FFN_CHUNK == 0
    cols = pl.BlockSpec((D_MODEL, FFN_CHUNK), lambda c: (0, c))
    rows = pl.BlockSpec((FFN_CHUNK, D_MODEL), lambda c: (c, 0))
    return pl.pallas_call(
        _ffn_kernel,
        out_shape=(jax.ShapeDtypeStruct((n, D_MODEL), F32), jax.ShapeDtypeStruct(wg.shape, BF16),
                   jax.ShapeDtypeStruct(wu.shape, BF16), jax.ShapeDtypeStruct(wd.shape, BF16)),
        grid=(D_FF // FFN_CHUNK,),
        in_specs=[_resident((n, D_MODEL)), _resident((1, D_MODEL)), cols, cols, rows],
        out_specs=(pl.BlockSpec((n, D_MODEL), lambda c: (0, 0)), cols, cols, rows),
        scratch_shapes=[pltpu.VMEM((n, D_MODEL), BF16), pltpu.VMEM((n, D_MODEL), F32)],
        compiler_params=pltpu.CompilerParams(dimension_semantics=("arbitrary",),
                                             vmem_limit_bytes=VMEM_LIMIT_BYTES),
        name=name,
    )(x, norm_w, wg, wu, wd)


class _Rows:
    def __init__(self, ref, start, n):
        self.ref, self.start, self.n = ref, start, n

    def __getitem__(self, idx):
        rows, cols = (slice(None), slice(None)) if idx is Ellipsis else idx
        lo = self.start + (rows.start or 0)
        hi = self.start + (self.n if rows.stop is None else rows.stop)
        return self.ref[lo:hi, cols]


def _mixer_vectors(pv_ref):
    sizes = (2, 1, CONV_W, 1, 1, 1, 1)
    views, start = [], 0
    for n in sizes:
        views.append(_Rows(pv_ref, start, n))
        start += n
    return views


def _hgrn_gates(proj_ref, lb_ref):
    l0 = lb_ref[0:1, :]
    l1 = lb_ref[1:2, :]
    m = jnp.maximum(l0, l1)
    e0 = jnp.exp(l0 - m)
    e1 = jnp.exp(l1 - m)
    lb = e0 / (e0 + e1)
    q = proj_ref[:, 0:D_A]
    fz = proj_ref[:, D_A:2 * D_A]
    f = lb + (1.0 - lb) * _sigmoid(fz)
    return q * _sigmoid(q), 1.0 - f, jnp.log(f)


def _pairwise(qs, k, cum, v, kpad_ref, cpad_ref, vpad_ref, out_ref, block, n_rows):
    pad = kpad_ref.shape[0] - n_rows
    kpad_ref[pl.ds(pad, n_rows), :] = k
    cpad_ref[pl.ds(pad, n_rows), :] = cum
    vpad_ref[pl.ds(pad, n_rows), :] = v
    rmod = lax.broadcasted_iota(jnp.int32, (n_rows, 1), 0) & (block - 1)
    out = [jnp.zeros((n_rows, D_HEAD), F32) for _ in range(N_HEADS)]
    for d in range(block):
        if d == 0:
            p = qs * k
            vs = v
        else:
            ks = kpad_ref[pl.ds(pad - d, n_rows), :]
            cs = cpad_ref[pl.ds(pad - d, n_rows), :]
            vs = vpad_ref[pl.ds(pad - d, n_rows), :]
            p = qs * ks * jnp.exp(cum - cs)
        valid = rmod >= d
        for h in range(N_HEADS):
            hs = slice(h * D_HEAD, (h + 1) * D_HEAD)
            s = jnp.sum(p[:, hs], axis=-1, keepdims=True)
            s = jnp.where(valid, s, 0.0)
            out[h] = out[h] + s * vs[:, hs]
    for h in range(N_HEADS):
        out_ref[:, h * D_HEAD:(h + 1) * D_HEAD] = out[h]


def _lru_coeffs(xc, wa_ref, ba_ref, wx_ref, bx_ref, lam_ref):
    xcb = xc.astype(BF16)

    def gate(w_ref, b_ref):
        tiles = [slice(j, j + GATE_TILE) for j in range(0, D_B, GATE_TILE)]
        z = jnp.concatenate([_dot(xcb[:, t], w_ref[t, t]) for t in tiles], axis=-1)
        return _sigmoid(z + b_ref[...])

    r = gate(wa_ref, ba_ref)
    i = gate(wx_ref, bx_ref)
    log_a = -LRU_C * r * _softplus(-lam_ref[...])
    a = jnp.exp(log_a)
    mult = jnp.sqrt(-_expm1(2.0 * log_a))
    return a, mult, i


def _scan_rows(a, u, rowpos, span):
    s = 1
    while s < span:
        keep = rowpos >= s
        a_sh = jnp.where(keep, pltpu.roll(a, s, 0), 1.0)
        u_sh = jnp.where(keep, pltpu.roll(u, s, 0), 0.0)
        u = a * u_sh + u
        a = a * a_sh
        s *= 2
    return a, u


def _row_bcast(ref, h, r, n):
    return jnp.broadcast_to(ref[h, pl.ds(r, 1), :], (n, D_HEAD))


def _pairwise_tile(qs_ref, k_ref, cum_ref, v_ref, out_ref, h, n_rows):
    pos = lax.broadcasted_iota(jnp.int32, (SUBLANES, 1), 0)
    for j in range(n_rows // SUBLANES):
        rows = pl.ds(SUBLANES * j, SUBLANES)
        q_t = qs_ref[h, rows, :]
        c_t = cum_ref[h, rows, :]
        acc = jnp.zeros((SUBLANES, D_HEAD), F32)
        for s in range(SUBLANES):
            src = SUBLANES * j + s
            d = c_t - _row_bcast(cum_ref, h, src, SUBLANES)
            if s > 0:
                d = jnp.where(pos >= s, d, NEG_LARGE)
            p = q_t * _row_bcast(k_ref, h, src, SUBLANES) * jnp.exp(d)
            acc = acc + jnp.sum(p, axis=-1, keepdims=True) * _row_bcast(v_ref, h, src, SUBLANES)
        out_ref[h, rows, :] = acc


def _level_operands(qs_ref, k_ref, cum_ref, h, n_rows, L):
    half = L // 2
    zeros = jnp.zeros((half, D_HEAD), F32)
    qparts, kparts = [], []
    for b in range(n_rows // L):
        r0 = b * L
        first = pl.ds(r0, half)
        second = pl.ds(r0 + half, half)
        mid = _row_bcast(cum_ref, h, r0 + half - 1, half)
        kparts += [k_ref[h, first, :] * jnp.exp(mid - cum_ref[h, first, :]), zeros]
        qparts += [zeros, qs_ref[h, second, :] * jnp.exp(cum_ref[h, second, :] - mid)]
    return jnp.concatenate(qparts, axis=0).astype(BF16), jnp.concatenate(kparts, axis=0).astype(BF16)


def _scan_tiles(a, u, carry):
    n = a.shape[0]
    a3 = a.reshape(n // SUBLANES, SUBLANES, D_B)
    u3 = u.reshape(n // SUBLANES, SUBLANES, D_B)
    pos = lax.broadcasted_iota(jnp.int32, (1, SUBLANES, 1), 1)
    s = 1
    while s < SUBLANES:
        keep = pos >= s
        a_sh = jnp.where(keep, pltpu.roll(a3, s, 1), 1.0)
        u_sh = jnp.where(keep, pltpu.roll(u3, s, 1), 0.0)
        u3 = a3 * u_sh + u3
        a3 = a3 * a_sh
        s *= 2
    tiles = []
    for j in range(n // SUBLANES):
        hj = u3[j] + a3[j] * carry
        tiles.append(hj)
        carry = hj[SUBLANES - 1:SUBLANES, :]
    return jnp.concatenate(tiles, axis=0)


class _SwigluChunks:
    def __init__(self, x, nw_ref, wg_ref, wu_ref, wd_ref):
        self.x = x
        self.xn = _rms(x, nw_ref[...]).astype(BF16)
        self.wg_ref, self.wu_ref, self.wd_ref = wg_ref, wu_ref, wd_ref
        self.hidden = {}
        self.acc = None

    @staticmethod
    def _cols(c):
        return slice(FFN_BOUNDS[c], FFN_BOUNDS[c + 1])

    def up(self, c):
        g = _dot(self.xn, self.wg_ref[:, self._cols(c)])
        u = _dot(self.xn, self.wu_ref[:, self._cols(c)])
        self.hidden[c] = (g * _sigmoid(g) * u).astype(BF16)

    def down(self, c):
        d = _dot(self.hidden.pop(c), self.wd_ref[self._cols(c), :])
        self.acc = d if self.acc is None else self.acc + d

    def result(self):
        assert not self.hidden
        return self.x + 0.5 * self.acc

    def pieces(self):
        n = len(FFN_BOUNDS) - 1
        out = [functools.partial(self.up, 0)]
        for c in range(1, n):
            out += [functools.partial(self.up, c), functools.partial(self.down, c - 1)]
        return out + [functools.partial(self.down, n - 1)]


def _layer_prompt_kernel(x_ref, xs_ref, n1_ref, wg1_ref, wu1_ref, wd1_ref,
                         mn_ref, win_ref, pv_ref, wa_ref, wx_ref, wo_ref,
                         n2_ref, wg2_ref, wu2_ref, wd2_ref, fn_ref,
                         y_ref, ys_ref, shg_ref, slru_ref, scv_ref,
                         x1_ref, x2_ref, proj_ref, st_ref, h_ref, xb_ref, qs_ref, k_ref, cum_ref, v_ref,
                         opw_ref, obuf_ref, *, chunk, chunks_per_seq, n_chunks):
    C = chunk
    s = pl.program_id(0)
    lb_ref, hn_ref, cw_ref, cb_ref, ba_ref, bx_ref, lam_ref = _mixer_vectors(pv_ref)

    @pl.when(s == 0)
    def _():
        x1_ref[...] = jnp.zeros(x1_ref.shape, F32)
        st_ref[...] = jnp.zeros(st_ref.shape, F32)
        h_ref[...] = jnp.zeros(h_ref.shape, F32)
        xb_ref[0:8, :] = jnp.zeros((8, D_B), F32)

    @pl.when(s < LAYER_FILL_STEPS)
    def _():
        x2_ref[...] = xs_ref[...]

    ffn2 = _SwigluChunks(x2_ref[...], n2_ref, wg2_ref, wu2_ref, wd2_ref)
    ffn1 = _SwigluChunks(x_ref[...], n1_ref, wg1_ref, wu1_ref, wd1_ref)

    pos_in_seq = lax.rem(s + (chunks_per_seq - 1), chunks_per_seq)
    first = pos_in_seq == 0
    keep = jnp.where(first, 0.0, 1.0)
    x = x1_ref[...]
    xn = _rms(x, mn_ref[...]).astype(BF16)
    proj_ref[...] = _dot(xn, win_ref[...])
    fill = ffn2.pieces() + ffn1.pieces()
    slots = iter(LAYER_FILL_SLOTS)

    def run_fill():
        for _ in range(next(slots)):
            fill.pop(0)()

    run_fill()

    qs, k, g = _hgrn_gates(proj_ref, lb_ref)
    row = lax.broadcasted_iota(jnp.int32, (C, C), 0)
    col = lax.broadcasted_iota(jnp.int32, (C, C), 1)
    tri = jnp.where(col <= row, 1.0, 0.0).astype(BF16)
    g1, g2, g3 = _split3_bf16(g)
    cum = _dot(tri, g1) + _dot(tri, g2) + _dot(tri, g3)
    for h in range(N_HEADS):
        hs = slice(h * D_HEAD, (h + 1) * D_HEAD)
        qs_ref[h] = qs[:, hs]
        k_ref[h] = k[:, hs]
        cum_ref[h] = cum[:, hs]
        v_ref[h] = proj_ref[:, 2 * D_A + h * D_HEAD:2 * D_A + (h + 1) * D_HEAD]

    levels = []
    L = 2 * SUBLANES
    while L <= C:
        shift = L.bit_length() - 1
        levels.append((L, None if L == C else (row >> shift) == (col >> shift)))
        L *= 2

    gate = proj_ref[:, 3 * D_A:4 * D_A]
    for h in range(N_HEADS):
        hs = slice(h * D_HEAD, (h + 1) * D_HEAD)
        run_fill()
        _pairwise_tile(qs_ref, k_ref, cum_ref, v_ref, opw_ref, h, C)
        scores = jnp.zeros((C, C), F32)
        for L, same in levels:
            qt, kt = _level_operands(qs_ref, k_ref, cum_ref, h, C, L)
            sc = _dot_nt(qt, kt)
            scores = scores + (sc if same is None else jnp.where(same, sc, 0.0))
        cum_h = cum_ref[h]
        cl = _row_bcast(cum_ref, h, C - 1, C)
        qbar = (qs_ref[h] * jnp.exp(cum_h)).astype(BF16)
        khat = (k_ref[h] * jnp.exp(cl - cum_h)).astype(BF16)
        dec = jnp.exp(cum_ref[h, pl.ds(C - 1, 1), :])
        v_h = v_ref[h]
        st = st_ref[h] * keep
        o = _dot(scores.astype(BF16), v_h.astype(BF16)) + _dot_nt(qbar, st.astype(BF16)) + opw_ref[h]
        st_ref[h] = st * dec + _dot(v_h.T.astype(BF16), khat)
        ms = jnp.mean(o * o, axis=-1, keepdims=True)
        obuf_ref[:, hs] = (o * lax.rsqrt(ms + EPS) * hn_ref[:, hs] * _sigmoid(gate[:, hs])).astype(BF16)

    run_fill()
    y_ref[...] = _rms(ffn2.result(), fn_ref[...])

    rowi = lax.broadcasted_iota(jnp.int32, (C, 1), 0)
    xb = proj_ref[:, 4 * D_A:4 * D_A + D_B]
    yb = proj_ref[:, 4 * D_A + D_B:D_IN]
    xb_ref[0:8, :] = xb_ref[0:8, :] * keep
    xb_ref[pl.ds(8, C), :] = xb
    xc = (cb_ref[...] + cw_ref[3:4, :] * xb + cw_ref[2:3, :] * xb_ref[pl.ds(7, C), :]
          + cw_ref[1:2, :] * xb_ref[pl.ds(6, C), :] + cw_ref[0:1, :] * xb_ref[pl.ds(5, C), :])
    xb_ref[0:8, :] = xb[C - 8:C, :]
    a, mult, gate_i = _lru_coeffs(xc, wa_ref, ba_ref, wx_ref, bx_ref, lam_ref)
    run_fill()
    mult = jnp.where((rowi == 0) & first, 1.0, mult)
    hseq = _scan_tiles(a, mult * gate_i * xc, h_ref[...] * keep)
    h_ref[...] = hseq[C - 1:C, :]
    obuf_ref[:, D_A:D_A + D_B] = (hseq * _gelu_tanh(yb)).astype(BF16)

    x2_ref[...] = x + _dot(obuf_ref[...], wo_ref[...])
    run_fill()
    assert not fill and next(slots, None) is None
    x1_ref[...] = ffn1.result()

    @pl.when(s < LAYER_FILL_STEPS)
    def _():
        ys_ref[...] = y_ref[...]

    @pl.when((pos_in_seq == chunks_per_seq - 1) & (s >= 1) & (s <= n_chunks))
    def _():
        for h in range(N_HEADS):
            shg_ref[0, h] = st_ref[h].T
        slru_ref[0] = hseq[C - 1:C, :]
        scv_ref[0] = xb[C - (CONV_W - 1):C, :]


def _layer_prompt(x, xs, ffn1, mix, ffn2, fn):
    b, t, _ = x.shape
    C = MIX_CHUNK
    assert t % C == 0 and C % (2 * SUBLANES) == 0 and C & (C - 1) == 0
    assert FFN_BOUNDS[0] == 0 and FFN_BOUNDS[-1] == D_FF
    assert sum(LAYER_FILL_SLOTS) == 4 * (len(FFN_BOUNDS) - 1) and len(LAYER_FILL_SLOTS) == N_HEADS + 4
    assert xs.shape == (LAYER_FILL_STEPS * C, D_MODEL), "sample rows must exactly fill the pipeline fill steps"
    nt = t // C
    n_chunks = b * nt
    rows_in = pl.BlockSpec((C, D_MODEL), lambda s: (jnp.minimum(s, n_chunks - 1), 0))
    rows_out = pl.BlockSpec((C, D_MODEL), lambda s: (jnp.clip(s - 2, 0, n_chunks - 1), 0))
    fill_index = lambda s: (jnp.minimum(s, LAYER_FILL_STEPS - 1), 0)
    rows_fill = pl.BlockSpec((C, D_MODEL), fill_index)
    rows_fill_in = pl.BlockSpec((C, D_MODEL), fill_index, pipeline_mode=pl.Buffered(1))
    seq_of = lambda s: jnp.clip((s - 1) // nt, 0, b - 1)
    vec = lambda n: _resident((1, n))
    ffn_specs = [vec(D_MODEL), _resident((D_MODEL, D_FF)), _resident((D_MODEL, D_FF)), _resident((D_FF, D_MODEL))]
    mix_specs = [_resident(a.shape) for a in mix]
    out_shape = (jax.ShapeDtypeStruct((b * t, D_MODEL), F32),
                 jax.ShapeDtypeStruct(xs.shape, F32),
                 jax.ShapeDtypeStruct((b, N_HEADS, D_HEAD, D_HEAD), F32),
                 jax.ShapeDtypeStruct((b, 1, D_B), F32),
                 jax.ShapeDtypeStruct((b, CONV_W - 1, D_B), F32))
    out_specs = (rows_out, rows_fill,
                 pl.BlockSpec((1, N_HEADS, D_HEAD, D_HEAD), lambda s: (seq_of(s), 0, 0, 0)),
                 pl.BlockSpec((1, 1, D_B), lambda s: (seq_of(s), 0, 0)),
                 pl.BlockSpec((1, CONV_W - 1, D_B), lambda s: (seq_of(s), 0, 0)))
    per_head = pltpu.VMEM((N_HEADS, C, D_HEAD), F32)
    scratch = [pltpu.VMEM((C, D_MODEL), F32),
               pltpu.VMEM((C, D_MODEL), F32),
               pltpu.VMEM((C, D_IN), F32),
               pltpu.VMEM((N_HEADS, D_HEAD, D_HEAD), F32),
               pltpu.VMEM((1, D_B), F32),
               pltpu.VMEM((C + 8, D_B), F32),
               per_head, per_head, per_head, per_head,
               per_head,
               pltpu.VMEM((C, D_A + D_B), BF16)]
    return pl.pallas_call(
        functools.partial(_layer_prompt_kernel, chunk=C, chunks_per_seq=nt, n_chunks=n_chunks),
        out_shape=out_shape,
        grid=(n_chunks + LAYER_FILL_STEPS,),
        in_specs=[rows_in, rows_fill_in] + ffn_specs + mix_specs + ffn_specs + [vec(D_MODEL)],
        out_specs=out_specs,
        scratch_shapes=scratch,
        compiler_params=pltpu.CompilerParams(dimension_semantics=("arbitrary",),
                                             vmem_limit_bytes=LAYER_VMEM_LIMIT_BYTES),
        name="layer_prompt",
    )(x.reshape(b * t, D_MODEL), xs, *ffn1, *mix, *ffn2, fn)


def _mixer_sample_kernel(x_ref, mn_ref, win_ref, pv_ref, wa_ref, wx_ref, wo_ref, s0_ref, rowstate_ref,
                         y_ref, s1_ref, hseq_ref, xb_out_ref,
                         proj_ref, xb_ref, kpad_ref, cpad_ref, vpad_ref, qbar_ref, khat_ref, ecum_ref,
                         opw_ref, ost_ref, ob_ref, *, steps):
    T = steps
    N = x_ref.shape[0]
    G = s0_ref.shape[0]
    R = G * T
    i = pl.program_id(0)
    tshift = T.bit_length() - 1
    lb_ref, hn_ref, cw_ref, cb_ref, ba_ref, bx_ref, lam_ref = _mixer_vectors(pv_ref)

    @pl.when(i == 0)
    def _():
        pad = kpad_ref.shape[0] - N
        zpad = jnp.zeros((pad, D_A), F32)
        kpad_ref[0:pad, :] = zpad
        cpad_ref[0:pad, :] = zpad
        vpad_ref[0:pad, :] = zpad
        xb_ref[0:8, :] = jnp.zeros((8, D_B), F32)

        xn = _rms(x_ref[...], mn_ref[...]).astype(BF16)
        proj_ref[...] = _dot(xn, win_ref[...])

        qs, k, g = _hgrn_gates(proj_ref, lb_ref)
        v = proj_ref[:, 2 * D_A:3 * D_A]
        tpos = lax.broadcasted_iota(jnp.int32, (N, 1), 0) & (T - 1)
        cum = g
        tail = jnp.zeros_like(g)
        for d in range(1, T):
            cum = cum + jnp.where(tpos >= d, pltpu.roll(g, d, 0), 0.0)
            tail = tail + jnp.where(tpos < T - d, pltpu.roll(g, N - d, 0), 0.0)
        _pairwise(qs, k, cum, v, kpad_ref, cpad_ref, vpad_ref, opw_ref, T, N)
        ecum = jnp.exp(cum)
        ecum_ref[...] = ecum
        qbar_ref[...] = qs * ecum
        khat_ref[...] = k * jnp.exp(tail)

        xb = proj_ref[:, 4 * D_A:4 * D_A + D_B]
        yb = proj_ref[:, 4 * D_A + D_B:D_IN]
        xb_ref[pl.ds(8, N), :] = xb
        xb_out_ref[...] = xb
        xc = cb_ref[...] + cw_ref[CONV_W - 1:CONV_W, :] * xb
        for d in range(1, CONV_W):
            prev = jnp.where(tpos >= d, xb_ref[pl.ds(8 - d, N), :], rowstate_ref[d])
            xc = xc + cw_ref[CONV_W - 1 - d:CONV_W - d, :] * prev
        a, mult, gate_i = _lru_coeffs(xc, wa_ref, ba_ref, wx_ref, bx_ref, lam_ref)
        a_run, h_run = _scan_rows(a, mult * gate_i * xc, tpos, T)
        hseq = h_run + a_run * rowstate_ref[0]
        hseq_ref[...] = hseq
        ob_ref[...] = hseq * _gelu_tanh(yb)

    r0 = pl.multiple_of(i * R, R)
    rows = pl.ds(r0, R)
    row8 = lax.broadcasted_iota(jnp.int32, (8, 1), 0)
    seq_of_row = lax.broadcasted_iota(jnp.int32, (R, 1), 0) >> tshift
    for h in range(N_HEADS):
        hs = slice(h * D_HEAD, (h + 1) * D_HEAD)
        khat_t = khat_ref[rows, hs].T.astype(BF16)
        dec_t = ecum_ref[rows, hs].T
        v_g = proj_ref[rows, 2 * D_A + h * D_HEAD:2 * D_A + (h + 1) * D_HEAD]
        for p in range(R // 8):
            tile = pl.ds(pl.multiple_of(r0 + 8 * p, 8), 8)
            q8 = qbar_ref[tile, hs].astype(BF16)
            o8 = jnp.zeros((8, D_HEAD), F32)
            for j in range(8 // T):
                b = p * (8 // T) + j
                s0 = s0_ref[b, h]
                ob = _dot(q8, s0.astype(BF16))
                o8 = jnp.where((row8 >= j * T) & (row8 < (j + 1) * T), ob, o8)
                v_b = jnp.where(seq_of_row == b, v_g, 0.0).astype(BF16)
                s1_ref[b, h] = s0 * dec_t[:, b * T + T - 1:b * T + T] + _dot(khat_t, v_b)
            ost_ref[tile, hs] = o8

    @pl.when(i == pl.num_programs(0) - 1)
    def _():
        gate = proj_ref[:, 3 * D_A:4 * D_A]
        outs = []
        for h in range(N_HEADS):
            hs = slice(h * D_HEAD, (h + 1) * D_HEAD)
            o = opw_ref[:, hs] + ost_ref[:, hs]
            ms = jnp.mean(o * o, axis=-1, keepdims=True)
            outs.append(o * lax.rsqrt(ms + EPS) * hn_ref[:, hs] * _sigmoid(gate[:, hs]))
        o = jnp.concatenate(outs + [ob_ref[...]], axis=-1).astype(BF16)
        y_ref[...] = x_ref[...] + _dot(o, wo_ref[...])


def _mixer_sample(xr, mix, s0, h0, buf):
    nb, T = s0.shape[0], xr.shape[0] // s0.shape[0]
    assert T == CONV_W and 8 % T == 0, "sample kernel assumes DEC_SEQ == CONV_W == 4"
    G = SAMPLE_GROUP
    N = nb * T
    assert nb % G == 0 and (G * T) % 8 == 0
    rowstate = [jnp.broadcast_to(h0[:, None, :], (nb, T, D_B))]
    rowstate += [jnp.concatenate([buf[:, CONV_W - 1 - d:, :], jnp.zeros((nb, T - d, D_B), F32)], axis=1)
                 for d in range(1, CONV_W)]
    rowstate = jnp.stack(rowstate).reshape(CONV_W, N, D_B)
    whole_out = lambda n: pl.BlockSpec((N, n), lambda i: (0, 0))
    state = pl.BlockSpec((G, N_HEADS, D_HEAD, D_HEAD), lambda i: (i, 0, 0, 0))
    out_shape = (jax.ShapeDtypeStruct((N, D_MODEL), F32),
                 jax.ShapeDtypeStruct((nb, N_HEADS, D_HEAD, D_HEAD), F32),
                 jax.ShapeDtypeStruct((N, D_B), F32),
                 jax.ShapeDtypeStruct((N, D_B), F32))
    half = lambda: pltpu.VMEM((N, D_A), F32)
    scratch = [pltpu.VMEM((N, D_IN), F32),
               pltpu.VMEM((N + 8, D_B), F32),
               pltpu.VMEM((N + 8, D_A), F32), pltpu.VMEM((N + 8, D_A), F32), pltpu.VMEM((N + 8, D_A), F32),
               half(), half(), half(),
               half(), half(),
               pltpu.VMEM((N, D_B), F32)]
    y, s1, hseq, xb = pl.pallas_call(
        functools.partial(_mixer_sample_kernel, steps=T),
        out_shape=out_shape,
        grid=(nb // G,),
        in_specs=[_resident(xr.shape)] + [_resident(a.shape) for a in mix] + [state, _resident(rowstate.shape)],
        out_specs=(whole_out(D_MODEL), state, whole_out(D_B), whole_out(D_B)),
        scratch_shapes=scratch,
        compiler_params=pltpu.CompilerParams(dimension_semantics=("arbitrary",),
                                             vmem_limit_bytes=LAYER_VMEM_LIMIT_BYTES),
        name="mixer_sample",
    )(xr, *mix, s0, rowstate)
    hseq = hseq.reshape(nb, T, D_B)
    xb = xb.reshape(nb, T, D_B)
    return y, s1, hseq[:, T - 1], xb[:, T - (CONV_W - 1):]


def _block_diag(w):
    n, bi, bj = w.shape
    eye = jnp.eye(n, dtype=w.dtype)
    return (w[:, :, None, :] * eye[:, None, :, None]).reshape(n * bi, n * bj)


def kernel(x_prompt, x_sample, state_hgrn, state_lru, state_conv, ffn1_norm, ffn1_wg, ffn1_wu, ffn1_wd,
           mix_norm, w_in, hgrn_lb, hgrn_norm, conv_w, conv_b, lru_wa, lru_ba, lru_wx, lru_bx, lru_lambda,
           w_o, ffn2_norm, ffn2_wg, ffn2_wu, ffn2_wd, final_norm):
    assert ffn1_norm.shape[0] == 1 and hgrn_lb.shape[0] == 2, "single-layer model"
    bp, tp, _ = x_prompt.shape
    bs, ts, _ = x_sample.shape

    ffn2 = (ffn2_norm, ffn2_wg[0].astype(BF16), ffn2_wu[0].astype(BF16), ffn2_wd[0].astype(BF16))
    fn = final_norm.reshape(1, D_MODEL)
    vectors = jnp.concatenate([hgrn_lb, hgrn_norm, conv_w[0], conv_b, lru_ba[0].reshape(1, D_B),
                               lru_bx[0].reshape(1, D_B), lru_lambda], axis=0)
    mix = (mix_norm, w_in[0].astype(BF16), vectors, _block_diag(lru_wa[0]).astype(BF16),
           _block_diag(lru_wx[0]).astype(BF16), w_o[0].astype(BF16))

    xs, *ffn1_bf16 = _ffn_and_round_weights(x_sample.reshape(bs * ts, D_MODEL), ffn1_norm, ffn1_wg[0],
                                            ffn1_wu[0], ffn1_wd[0], name="ffn1_sample")
    ffn1 = (ffn1_norm, *ffn1_bf16)
    xs, hg_s, lru_s, cv_s = _mixer_sample(xs, mix, state_hgrn[0], state_lru[0], state_conv[0])

    yp, ys, hg_p, lru_p, cv_p = _layer_prompt(x_prompt, xs, ffn1, mix, ffn2, fn)

    return (yp.reshape(bp, tp, D_MODEL), ys.reshape(bs, ts, D_MODEL),
            hg_p[None], lru_p.reshape(1, bp, D_B), cv_p[None],
            hg_s[None], lru_s[None], cv_s[None])
```

```python
import functools
import math

import jax
import jax.numpy as jnp
from jax import lax
from jax.experimental import pallas as pl
from jax.experimental.pallas import tpu as pltpu

F32 = jnp.float32
BF16 = jnp.bfloat16

D_MODEL = 1024
D_FF = 2816
D_A = 512
D_B = 512
N_HEADS = 4
D_HEAD = 128
D_IN = 4 * D_A + 2 * D_B
CONV_W = 4
LRU_C = 8.0
EPS = 1e-6

SUBLANES = 8
GATE_TILE = 256
NEG_LARGE = -1e30
MIX_CHUNK = 256
FFN_CHUNK = 256
FFN_BOUNDS = (0, 768, 1536, 2304, 2816)
LAYER_FILL_SLOTS = (1, 1, 1, 1, 1, 1, 1, 1, 1, 2, 2, 3)
LAYER_FILL_STEPS = 2
SAMPLE_GROUP = 16
VMEM_LIMIT_BYTES = 56 * 1024 * 1024
LAYER_VMEM_LIMIT_BYTES = 63 * 1024 * 1024

_NT = (((1,), (1,)), ((), ()))


def _dot(a, b):
    return jnp.dot(a, b, preferred_element_type=F32)


def _dot_nt(a, b):
    return lax.dot_general(a, b, _NT, preferred_element_type=F32)


def _rms(x, w):
    ms = jnp.mean(x * x, axis=-1, keepdims=True)
    return x * lax.rsqrt(ms + EPS) * w


def _sigmoid(x):
    return 1.0 / (1.0 + jnp.exp(-x))


def _log1p(x):
    u = 1.0 + x
    return jnp.where(u == 1.0, x, jnp.log(u) * x / (u - 1.0))


def _expm1(x):
    u = jnp.exp(x)
    return jnp.where(u == 1.0, x, (u - 1.0) * x / jnp.log(u))


def _softplus(x):
    return jnp.maximum(x, 0.0) + _log1p(jnp.exp(-jnp.abs(x)))


def _gelu_tanh(x):
    return 0.5 * x * (1.0 + jnp.tanh(math.sqrt(2.0 / math.pi) * (x + 0.044715 * (x * x * x))))


def _split3_bf16(x):
    x1 = x.astype(BF16)
    r1 = x - x1.astype(F32)
    x2 = r1.astype(BF16)
    r2 = r1 - x2.astype(F32)
    return x1, x2, r2.astype(BF16)


def _ffn_kernel(x_ref, nw_ref, wg_ref, wu_ref, wd_ref, o_ref, wgb_ref, wub_ref, wdb_ref, xn_ref, acc_ref):
    c = pl.program_id(0)

    @pl.when(c == 0)
    def _():
        xn_ref[...] = _rms(x_ref[...], nw_ref[...]).astype(BF16)
        acc_ref[...] = jnp.zeros(acc_ref.shape, F32)

    wg = wg_ref[...].astype(BF16)
    wu = wu_ref[...].astype(BF16)
    wd = wd_ref[...].astype(BF16)
    wgb_ref[...] = wg
    wub_ref[...] = wu
    wdb_ref[...] = wd
    xn = xn_ref[...]
    g = _dot(xn, wg)
    u = _dot(xn, wu)
    acc_ref[...] += _dot((g * _sigmoid(g) * u).astype(BF16), wd)

    @pl.when(c == pl.num_programs(0) - 1)
    def _():
        o_ref[...] = x_ref[...] + 0.5 * acc_ref[...]


def _resident(shape):
    zeros = (0,) * len(shape)
    return pl.BlockSpec(shape, lambda *_: zeros, pipeline_mode=pl.Buffered(1))


def _ffn_and_round_weights(x, norm_w, wg, wu, wd, *, name):
    n = x.shape[0]
    assert D_FF % FFN_CHUNK == 0
    cols = pl.BlockSpec((D_MODEL, FFN_CHUNK), lambda c: (0, c))
    rows = pl.BlockSpec((FFN_CHUNK, D_MODEL), lambda c: (c, 0))
    return pl.pallas_call(
        _ffn_kernel,
        out_shape=(jax.ShapeDtypeStruct((n, D_MODEL), F32), jax.ShapeDtypeStruct(wg.shape, BF16),
                   jax.ShapeDtypeStruct(wu.shape, BF16), jax.ShapeDtypeStruct(wd.shape, BF16)),
        grid=(D_FF // FFN_CHUNK,),
        in_specs=[_resident((n, D_MODEL)), _resident((1, D_MODEL)), cols, cols, rows],
        out_specs=(pl.BlockSpec((n, D_MODEL), lambda c: (0, 0)), cols, cols, rows),
        scratch_shapes=[pltpu.VMEM((n, D_MODEL), BF16), pltpu.VMEM((n, D_MODEL), F32)],
        compiler_params=pltpu.CompilerParams(dimension_semantics=("arbitrary",),
                                             vmem_limit_bytes=VMEM_LIMIT_BYTES),
        name=name,
    )(x, norm_w, wg, wu, wd)


class _Rows:
    def __init__(self, ref, start, n):
        self.ref, self.start, self.n = ref, start, n

    def __getitem__(self, idx):
        rows, cols = (slice(None), slice(None)) if idx is Ellipsis else idx
        lo = self.start + (rows.start or 0)
        hi = self.start + (self.n if rows.stop is None else rows.stop)
        return self.ref[lo:hi, cols]


def _mixer_vectors(pv_ref):
    sizes = (2, 1, CONV_W, 1, 1, 1, 1)
    views, start = [], 0
    for n in sizes:
        views.append(_Rows(pv_ref, start, n))
        start += n
    return views


def _hgrn_gates(proj_ref, lb_ref):
    l0 = lb_ref[0:1, :]
    l1 = lb_ref[1:2, :]
    m = jnp.maximum(l0, l1)
    e0 = jnp.exp(l0 - m)
    e1 = jnp.exp(l1 - m)
    lb = e0 / (e0 + e1)
    q = proj_ref[:, 0:D_A]
    fz = proj_ref[:, D_A:2 * D_A]
    f = lb + (1.0 - lb) * _sigmoid(fz)
    return q * _sigmoid(q), 1.0 - f, jnp.log(f)


def _pairwise(qs, k, cum, v, kpad_ref, cpad_ref, vpad_ref, out_ref, block, n_rows):
    pad = kpad_ref.shape[0] - n_rows
    kpad_ref[pl.ds(pad, n_rows), :] = k
    cpad_ref[pl.ds(pad, n_rows), :] = cum
    vpad_ref[pl.ds(pad, n_rows), :] = v
    rmod = lax.broadcasted_iota(jnp.int32, (n_rows, 1), 0) & (block - 1)
    out = [jnp.zeros((n_rows, D_HEAD), F32) for _ in range(N_HEADS)]
    for d in range(block):
        if d == 0:
            p = qs * k
            vs = v
        else:
            ks = kpad_ref[pl.ds(pad - d, n_rows), :]
            cs = cpad_ref[pl.ds(pad - d, n_rows), :]
            vs = vpad_ref[pl.ds(pad - d, n_rows), :]
            p = qs * ks * jnp.exp(cum - cs)
        valid = rmod >= d
        for h in range(N_HEADS):
            hs = slice(h * D_HEAD, (h + 1) * D_HEAD)
            s = jnp.sum(p[:, hs], axis=-1, keepdims=True)
            s = jnp.where(valid, s, 0.0)
            out[h] = out[h] + s * vs[:, hs]
    for h in range(N_HEADS):
        out_ref[:, h * D_HEAD:(h + 1) * D_HEAD] = out[h]


def _lru_coeffs(xc, wa_ref, ba_ref, wx_ref, bx_ref, lam_ref):
    xcb = xc.astype(BF16)

    def gate(w_ref, b_ref):
        tiles = [slice(j, j + GATE_TILE) for j in range(0, D_B, GATE_TILE)]
        z = jnp.concatenate([_dot(xcb[:, t], w_ref[t, t]) for t in tiles], axis=-1)
        return _sigmoid(z + b_ref[...])

    r = gate(wa_ref, ba_ref)
    i = gate(wx_ref, bx_ref)
    log_a = -LRU_C * r * _softplus(-lam_ref[...])
    a = jnp.exp(log_a)
    mult = jnp.sqrt(-_expm1(2.0 * log_a))
    return a, mult, i


def _scan_rows(a, u, rowpos, span):
    s = 1
    while s < span:
        keep = rowpos >= s
        a_sh = jnp.where(keep, pltpu.roll(a, s, 0), 1.0)
        u_sh = jnp.where(keep, pltpu.roll(u, s, 0), 0.0)
        u = a * u_sh + u
        a = a * a_sh
        s *= 2
    return a, u


class _HeadCols:
    def __init__(self, ref, base):
        self.ref, self.base = ref, base

    def __getitem__(self, idx):
        h, rows = (idx[0], idx[1]) if isinstance(idx, tuple) else (idx, slice(None))
        return self.ref[rows, self.base + h * D_HEAD:self.base + (h + 1) * D_HEAD]


def _row_bcast(ref, h, r, n):
    return jnp.broadcast_to(ref[h, pl.ds(r, 1), :], (n, D_HEAD))


def _pairwise_tile(qs_ref, k_ref, cum_ref, v_ref, out_ref, h, n_rows):
    pos = lax.broadcasted_iota(jnp.int32, (SUBLANES, 1), 0)
    for j in range(n_rows // SUBLANES):
        rows = pl.ds(SUBLANES * j, SUBLANES)
        q_t = qs_ref[h, rows, :]
        c_t = cum_ref[h, rows, :]
        acc = jnp.zeros((SUBLANES, D_HEAD), F32)
        for s in range(SUBLANES):
            src = SUBLANES * j + s
            d = c_t - _row_bcast(cum_ref, h, src, SUBLANES)
            if s > 0:
                d = jnp.where(pos >= s, d, NEG_LARGE)
            p = q_t * _row_bcast(k_ref, h, src, SUBLANES) * jnp.exp(d)
            acc = acc + jnp.sum(p, axis=-1, keepdims=True) * _row_bcast(v_ref, h, src, SUBLANES)
        out_ref[rows, :] = acc


def _level_operands(qs_ref, k_ref, cum_ref, h, n_rows, L):
    half = L // 2
    zeros = jnp.zeros((half, D_HEAD), F32)
    qparts, kparts = [], []
    for b in range(n_rows // L):
        r0 = b * L
        first = pl.ds(r0, half)
        second = pl.ds(r0 + half, half)
        mid = _row_bcast(cum_ref, h, r0 + half - 1, half)
        kparts += [k_ref[h, first, :] * jnp.exp(mid - cum_ref[h, first, :]), zeros]
        qparts += [zeros, qs_ref[h, second, :] * jnp.exp(cum_ref[h, second, :] - mid)]
    return jnp.concatenate(qparts, axis=0).astype(BF16), jnp.concatenate(kparts, axis=0).astype(BF16)


def _scan_tiles(a, u, carry):
    n = a.shape[0]
    a3 = a.reshape(n // SUBLANES, SUBLANES, D_B)
    u3 = u.reshape(n // SUBLANES, SUBLANES, D_B)
    pos = lax.broadcasted_iota(jnp.int32, (1, SUBLANES, 1), 1)
    s = 1
    while s < SUBLANES:
        keep = pos >= s
        a_sh = jnp.where(keep, pltpu.roll(a3, s, 1), 1.0)
        u_sh = jnp.where(keep, pltpu.roll(u3, s, 1), 0.0)
        u3 = a3 * u_sh + u3
        a3 = a3 * a_sh
        s *= 2
    tiles = []
    for j in range(n // SUBLANES):
        hj = u3[j] + a3[j] * carry
        tiles.append(hj)
        carry = hj[SUBLANES - 1:SUBLANES, :]
    return jnp.concatenate(tiles, axis=0)


class _SwigluChunks:
    def __init__(self, x, nw_ref, wg_ref, wu_ref, wd_ref):
        self.x = x
        self.xn = _rms(x, nw_ref[...]).astype(BF16)
        self.wg_ref, self.wu_ref, self.wd_ref = wg_ref, wu_ref, wd_ref
        self.hidden = {}
        self.acc = None

    @staticmethod
    def _cols(c):
        return slice(FFN_BOUNDS[c], FFN_BOUNDS[c + 1])

    def up(self, c):
        g = _dot(self.xn, self.wg_ref[:, self._cols(c)])
        u = _dot(self.xn, self.wu_ref[:, self._cols(c)])
        self.hidden[c] = (g * _sigmoid(g) * u).astype(BF16)

    def down(self, c):
        d = _dot(self.hidden.pop(c), self.wd_ref[self._cols(c), :])
        self.acc = d if self.acc is None else self.acc + d

    def result(self):
        assert not self.hidden
        return self.x + 0.5 * self.acc

    def pieces(self):
        n = len(FFN_BOUNDS) - 1
        out = [functools.partial(self.up, 0)]
        for c in range(1, n):
            out += [functools.partial(self.up, c), functools.partial(self.down, c - 1)]
        return out + [functools.partial(self.down, n - 1)]


def _layer_prompt_kernel(x_ref, xs_ref, n1_ref, wg1_ref, wu1_ref, wd1_ref,
                         mn_ref, win_ref, pv_ref, wa_ref, wx_ref, wo_ref,
                         n2_ref, wg2_ref, wu2_ref, wd2_ref, fn_ref,
                         y_ref, ys_ref, shg_ref, slru_ref, scv_ref,
                         x1_ref, x2_ref, proj_ref, st_ref, h_ref, xb_ref, cum_all_ref,
                         opw_ref, obuf_ref, *, chunk, chunks_per_seq, n_chunks):
    C = chunk
    s = pl.program_id(0)
    lb_ref, hn_ref, cw_ref, cb_ref, ba_ref, bx_ref, lam_ref = _mixer_vectors(pv_ref)

    @pl.when(s == 0)
    def _():
        x1_ref[...] = jnp.zeros(x1_ref.shape, F32)
        st_ref[...] = jnp.zeros(st_ref.shape, F32)
        h_ref[...] = jnp.zeros(h_ref.shape, F32)
        xb_ref[0:8, :] = jnp.zeros((8, D_B), F32)

    @pl.when(s < LAYER_FILL_STEPS)
    def _():
        x2_ref[...] = xs_ref[...]

    ffn2 = _SwigluChunks(x2_ref[...], n2_ref, wg2_ref, wu2_ref, wd2_ref)
    ffn1 = _SwigluChunks(x_ref[...], n1_ref, wg1_ref, wu1_ref, wd1_ref)

    pos_in_seq = lax.rem(s + (chunks_per_seq - 1), chunks_per_seq)
    first = pos_in_seq == 0
    keep = jnp.where(first, 0.0, 1.0)
    x = x1_ref[...]
    xn = _rms(x, mn_ref[...]).astype(BF16)
    proj_ref[...] = _dot(xn, win_ref[...])
    fill = ffn2.pieces() + ffn1.pieces()
    slots = iter(LAYER_FILL_SLOTS)

    def run_fill():
        for _ in range(next(slots)):
            fill.pop(0)()

    run_fill()

    qs, k, g = _hgrn_gates(proj_ref, lb_ref)
    row = lax.broadcasted_iota(jnp.int32, (C, C), 0)
    col = lax.broadcasted_iota(jnp.int32, (C, C), 1)
    tri = jnp.where(col <= row, 1.0, 0.0).astype(BF16)
    g1, g2, g3 = _split3_bf16(g)
    cum_all_ref[...] = _dot(tri, g1) + _dot(tri, g2) + _dot(tri, g3)
    proj_ref[:, 0:D_A] = qs
    proj_ref[:, D_A:2 * D_A] = k
    qs_ref, k_ref, v_ref = (_HeadCols(proj_ref, base) for base in (0, D_A, 2 * D_A))
    cum_ref = _HeadCols(cum_all_ref, 0)

    levels = []
    L = 2 * SUBLANES
    while L <= C:
        shift = L.bit_length() - 1
        levels.append((L, None if L == C else (row >> shift) == (col >> shift)))
        L *= 2

    gate = proj_ref[:, 3 * D_A:4 * D_A]
    for h in range(N_HEADS):
        hs = slice(h * D_HEAD, (h + 1) * D_HEAD)
        run_fill()
        _pairwise_tile(qs_ref, k_ref, cum_ref, v_ref, opw_ref, h, C)
        scores = jnp.zeros((C, C), F32)
        for L, same in levels:
            qt, kt = _level_operands(qs_ref, k_ref, cum_ref, h, C, L)
            sc = _dot_nt(qt, kt)
            scores = scores + (sc if same is None else jnp.where(same, sc, 0.0))
        run_fill()
        cum_h = cum_ref[h]
        cl = _row_bcast(cum_ref, h, C - 1, C)
        qbar = (qs_ref[h] * jnp.exp(cum_h)).astype(BF16)
        khat = (k_ref[h] * jnp.exp(cl - cum_h)).astype(BF16)
        dec = jnp.exp(cum_ref[h, pl.ds(C - 1, 1), :])
        v_h = v_ref[h]
        st = st_ref[h] * keep
        o = _dot(scores.astype(BF16), v_h.astype(BF16)) + _dot_nt(qbar, st.astype(BF16)) + opw_ref[...]
        st_ref[h] = st * dec + _dot(v_h.T.astype(BF16), khat)
        ms = jnp.mean(o * o, axis=-1, keepdims=True)
        obuf_ref[:, hs] = (o * lax.rsqrt(ms + EPS) * hn_ref[:, hs] * _sigmoid(gate[:, hs])).astype(BF16)

    run_fill()
    y_ref[...] = _rms(ffn2.result(), fn_ref[...])

    rowi = lax.broadcasted_iota(jnp.int32, (C, 1), 0)
    xb = proj_ref[:, 4 * D_A:4 * D_A + D_B]
    yb = proj_ref[:, 4 * D_A + D_B:D_IN]
    xb_ref[0:8, :] = xb_ref[0:8, :] * keep
    xb_ref[pl.ds(8, C), :] = xb
    xc = (cb_ref[...] + cw_ref[3:4, :] * xb + cw_ref[2:3, :] * xb_ref[pl.ds(7, C), :]
          + cw_ref[1:2, :] * xb_ref[pl.ds(6, C), :] + cw_ref[0:1, :] * xb_ref[pl.ds(5, C), :])
    xb_ref[0:8, :] = xb[C - 8:C, :]
    a, mult, gate_i = _lru_coeffs(xc, wa_ref, ba_ref, wx_ref, bx_ref, lam_ref)
    run_fill()
    mult = jnp.where((rowi == 0) & first, 1.0, mult)
    hseq = _scan_tiles(a, mult * gate_i * xc, h_ref[...] * keep)
    h_ref[...] = hseq[C - 1:C, :]
    obuf_ref[:, D_A:D_A + D_B] = (hseq * _gelu_tanh(yb)).astype(BF16)

    x2_ref[...] = x + _dot(obuf_ref[...], wo_ref[...])
    run_fill()
    assert not fill and next(slots, None) is None
    x1_ref[...] = ffn1.result()

    @pl.when(s < LAYER_FILL_STEPS)
    def _():
        ys_ref[...] = y_ref[...]

    @pl.when((pos_in_seq == chunks_per_seq - 1) & (s >= 1) & (s <= n_chunks))
    def _():
        for h in range(N_HEADS):
            shg_ref[0, h] = st_ref[h].T
        slru_ref[0] = hseq[C - 1:C, :]
        scv_ref[0] = xb[C - (CONV_W - 1):C, :]


def _layer_prompt(x, xs, ffn1, mix, ffn2, fn):
    b, t, _ = x.shape
    C = MIX_CHUNK
    assert t % C == 0 and C % (2 * SUBLANES) == 0 and C & (C - 1) == 0
    assert FFN_BOUNDS[0] == 0 and FFN_BOUNDS[-1] == D_FF
    assert sum(LAYER_FILL_SLOTS) == 4 * (len(FFN_BOUNDS) - 1) and len(LAYER_FILL_SLOTS) == 2 * N_HEADS + 4
    assert xs.shape == (LAYER_FILL_STEPS * C, D_MODEL), "sample rows must exactly fill the pipeline fill steps"
    nt = t // C
    n_chunks = b * nt
    rows_in = pl.BlockSpec((C, D_MODEL), lambda s: (jnp.minimum(s, n_chunks - 1), 0))
    rows_out = pl.BlockSpec((C, D_MODEL), lambda s: (jnp.clip(s - 2, 0, n_chunks - 1), 0))
    fill_index = lambda s: (jnp.minimum(s, LAYER_FILL_STEPS - 1), 0)
    rows_fill = pl.BlockSpec((C, D_MODEL), fill_index, pipeline_mode=pl.Buffered(1))
    seq_of = lambda s: jnp.clip((s - 1) // nt, 0, b - 1)
    vec = lambda n: _resident((1, n))
    ffn_specs = [vec(D_MODEL), _resident((D_MODEL, D_FF)), _resident((D_MODEL, D_FF)), _resident((D_FF, D_MODEL))]
    mix_specs = [_resident(a.shape) for a in mix]
    out_shape = (jax.ShapeDtypeStruct((b * t, D_MODEL), F32),
                 jax.ShapeDtypeStruct(xs.shape, F32),
                 jax.ShapeDtypeStruct((b, N_HEADS, D_HEAD, D_HEAD), F32),
                 jax.ShapeDtypeStruct((b, 1, D_B), F32),
                 jax.ShapeDtypeStruct((b, CONV_W - 1, D_B), F32))
    out_specs = (rows_out, rows_fill,
                 pl.BlockSpec((1, N_HEADS, D_HEAD, D_HEAD), lambda s: (seq_of(s), 0, 0, 0)),
                 pl.BlockSpec((1, 1, D_B), lambda s: (seq_of(s), 0, 0)),
                 pl.BlockSpec((1, CONV_W - 1, D_B), lambda s: (seq_of(s), 0, 0)))
    scratch = [pltpu.VMEM((C, D_MODEL), F32),
               pltpu.VMEM((C, D_MODEL), F32),
               pltpu.VMEM((C, D_IN), F32),
               pltpu.VMEM((N_HEADS, D_HEAD, D_HEAD), F32),
               pltpu.VMEM((1, D_B), F32),
               pltpu.VMEM((C + 8, D_B), F32),
               pltpu.VMEM((C, D_A), F32),
               pltpu.VMEM((C, D_HEAD), F32),
               pltpu.VMEM((C, D_A + D_B), BF16)]
    return pl.pallas_call(
        functools.partial(_layer_prompt_kernel, chunk=C, chunks_per_seq=nt, n_chunks=n_chunks),
        out_shape=out_shape,
        grid=(n_chunks + LAYER_FILL_STEPS,),
        in_specs=[rows_in, rows_fill] + ffn_specs + mix_specs + ffn_specs + [vec(D_MODEL)],
        out_specs=out_specs,
        scratch_shapes=scratch,
        compiler_params=pltpu.CompilerParams(dimension_semantics=("arbitrary",),
                                             vmem_limit_bytes=LAYER_VMEM_LIMIT_BYTES),
        name="layer_prompt",
    )(x.reshape(b * t, D_MODEL), xs, *ffn1, *mix, *ffn2, fn)


def _mixer_sample_kernel(x_ref, mn_ref, win_ref, pv_ref, wa_ref, wx_ref, wo_ref, s0_ref, rowstate_ref,
                         y_ref, s1_ref, hseq_ref, xb_out_ref,
                         proj_ref, xb_ref, kpad_ref, cpad_ref, vpad_ref, qbar_ref, khat_ref, ecum_ref,
                         opw_ref, ost_ref, ob_ref, *, steps):
    T = steps
    N = x_ref.shape[0]
    G = s0_ref.shape[0]
    R = G * T
    i = pl.program_id(0)
    tshift = T.bit_length() - 1
    lb_ref, hn_ref, cw_ref, cb_ref, ba_ref, bx_ref, lam_ref = _mixer_vectors(pv_ref)

    @pl.when(i == 0)
    def _():
        pad = kpad_ref.shape[0] - N
        zpad = jnp.zeros((pad, D_A), F32)
        kpad_ref[0:pad, :] = zpad
        cpad_ref[0:pad, :] = zpad
        vpad_ref[0:pad, :] = zpad
        xb_ref[0:8, :] = jnp.zeros((8, D_B), F32)

        xn = _rms(x_ref[...], mn_ref[...]).astype(BF16)
        proj_ref[...] = _dot(xn, win_ref[...])

        qs, k, g = _hgrn_gates(proj_ref, lb_ref)
        v = proj_ref[:, 2 * D_A:3 * D_A]
        tpos = lax.broadcasted_iota(jnp.int32, (N, 1), 0) & (T - 1)
        cum = g
        tail = jnp.zeros_like(g)
        for d in range(1, T):
            cum = cum + jnp.where(tpos >= d, pltpu.roll(g, d, 0), 0.0)
            tail = tail + jnp.where(tpos < T - d, pltpu.roll(g, N - d, 0), 0.0)
        _pairwise(qs, k, cum, v, kpad_ref, cpad_ref, vpad_ref, opw_ref, T, N)
        ecum = jnp.exp(cum)
        ecum_ref[...] = ecum
        qbar_ref[...] = qs * ecum
        khat_ref[...] = k * jnp.exp(tail)

        xb = proj_ref[:, 4 * D_A:4 * D_A + D_B]
        yb = proj_ref[:, 4 * D_A + D_B:D_IN]
        xb_ref[pl.ds(8, N), :] = xb
        xb_out_ref[...] = xb
        xc = cb_ref[...] + cw_ref[CONV_W - 1:CONV_W, :] * xb
        for d in range(1, CONV_W):
            prev = jnp.where(tpos >= d, xb_ref[pl.ds(8 - d, N), :], rowstate_ref[d])
            xc = xc + cw_ref[CONV_W - 1 - d:CONV_W - d, :] * prev
        a, mult, gate_i = _lru_coeffs(xc, wa_ref, ba_ref, wx_ref, bx_ref, lam_ref)
        a_run, h_run = _scan_rows(a, mult * gate_i * xc, tpos, T)
        hseq = h_run + a_run * rowstate_ref[0]
        hseq_ref[...] = hseq
        ob_ref[...] = hseq * _gelu_tanh(yb)

    r0 = pl.multiple_of(i * R, R)
    rows = pl.ds(r0, R)
    row8 = lax.broadcasted_iota(jnp.int32, (8, 1), 0)
    seq_of_row = lax.broadcasted_iota(jnp.int32, (R, 1), 0) >> tshift
    for h in range(N_HEADS):
        hs = slice(h * D_HEAD, (h + 1) * D_HEAD)
        khat_t = khat_ref[rows, hs].T.astype(BF16)
        dec_t = ecum_ref[rows, hs].T
        v_g = proj_ref[rows, 2 * D_A + h * D_HEAD:2 * D_A + (h + 1) * D_HEAD]
        for p in range(R // 8):
            tile = pl.ds(pl.multiple_of(r0 + 8 * p, 8), 8)
            q8 = qbar_ref[tile, hs].astype(BF16)
            o8 = jnp.zeros((8, D_HEAD), F32)
            for j in range(8 // T):
                b = p * (8 // T) + j
                s0 = s0_ref[b, h]
                ob = _dot(q8, s0.astype(BF16))
                o8 = jnp.where((row8 >= j * T) & (row8 < (j + 1) * T), ob, o8)
                v_b = jnp.where(seq_of_row == b, v_g, 0.0).astype(BF16)
                s1_ref[b, h] = s0 * dec_t[:, b * T + T - 1:b * T + T] + _dot(khat_t, v_b)
            ost_ref[tile, hs] = o8

    @pl.when(i == pl.num_programs(0) - 1)
    def _():
        gate = proj_ref[:, 3 * D_A:4 * D_A]
        outs = []
        for h in range(N_HEADS):
            hs = slice(h * D_HEAD, (h + 1) * D_HEAD)
            o = opw_ref[:, hs] + ost_ref[:, hs]
            ms = jnp.mean(o * o, axis=-1, keepdims=True)
            outs.append(o * lax.rsqrt(ms + EPS) * hn_ref[:, hs] * _sigmoid(gate[:, hs]))
        o = jnp.concatenate(outs + [ob_ref[...]], axis=-1).astype(BF16)
        y_ref[...] = x_ref[...] + _dot(o, wo_ref[...])


def _mixer_sample(xr, mix, s0, h0, buf):
    nb, T = s0.shape[0], xr.shape[0] // s0.shape[0]
    assert T == CONV_W and 8 % T == 0, "sample kernel assumes DEC_SEQ == CONV_W == 4"
    G = SAMPLE_GROUP
    N = nb * T
    assert nb % G == 0 and (G * T) % 8 == 0
    rowstate = [jnp.broadcast_to(h0[:, None, :], (nb, T, D_B))]
    rowstate += [jnp.concatenate([buf[:, CONV_W - 1 - d:, :], jnp.zeros((nb, T - d, D_B), F32)], axis=1)
                 for d in range(1, CONV_W)]
    rowstate = jnp.stack(rowstate).reshape(CONV_W, N, D_B)
    whole_out = lambda n: pl.BlockSpec((N, n), lambda i: (0, 0))
    state = pl.BlockSpec((G, N_HEADS, D_HEAD, D_HEAD), lambda i: (i, 0, 0, 0))
    out_shape = (jax.ShapeDtypeStruct((N, D_MODEL), F32),
                 jax.ShapeDtypeStruct((nb, N_HEADS, D_HEAD, D_HEAD), F32),
                 jax.ShapeDtypeStruct((N, D_B), F32),
                 jax.ShapeDtypeStruct((N, D_B), F32))
    half = lambda: pltpu.VMEM((N, D_A), F32)
    scratch = [pltpu.VMEM((N, D_IN), F32),
               pltpu.VMEM((N + 8, D_B), F32),
               pltpu.VMEM((N + 8, D_A), F32), pltpu.VMEM((N + 8, D_A), F32), pltpu.VMEM((N + 8, D_A), F32),
               half(), half(), half(),
               half(), half(),
               pltpu.VMEM((N, D_B), F32)]
    y, s1, hseq, xb = pl.pallas_call(
        functools.partial(_mixer_sample_kernel, steps=T),
        out_shape=out_shape,
        grid=(nb // G,),
        in_specs=[_resident(xr.shape)] + [_resident(a.shape) for a in mix] + [state, _resident(rowstate.shape)],
        out_specs=(whole_out(D_MODEL), state, whole_out(D_B), whole_out(D_B)),
        scratch_shapes=scratch,
        compiler_params=pltpu.CompilerParams(dimension_semantics=("arbitrary",),
                                             vmem_limit_bytes=LAYER_VMEM_LIMIT_BYTES),
        name="mixer_sample",
    )(xr, *mix, s0, rowstate)
    hseq = hseq.reshape(nb, T, D_B)
    xb = xb.reshape(nb, T, D_B)
    return y, s1, hseq[:, T - 1], xb[:, T - (CONV_W - 1):]


def _block_diag(w):
    n, bi, bj = w.shape
    eye = jnp.eye(n, dtype=w.dtype)
    return (w[:, :, None, :] * eye[:, None, :, None]).reshape(n * bi, n * bj)


def kernel(x_prompt, x_sample, state_hgrn, state_lru, state_conv, ffn1_norm, ffn1_wg, ffn1_wu, ffn1_wd,
           mix_norm, w_in, hgrn_lb, hgrn_norm, conv_w, conv_b, lru_wa, lru_ba, lru_wx, lru_bx, lru_lambda,
           w_o, ffn2_norm, ffn2_wg, ffn2_wu, ffn2_wd, final_norm):
    assert ffn1_norm.shape[0] == 1 and hgrn_lb.shape[0] == 2, "single-layer model"
    bp, tp, _ = x_prompt.shape
    bs, ts, _ = x_sample.shape

    ffn2 = (ffn2_norm, ffn2_wg[0].astype(BF16), ffn2_wu[0].astype(BF16), ffn2_wd[0].astype(BF16))
    fn = final_norm.reshape(1, D_MODEL)
    vectors = jnp.concatenate([hgrn_lb, hgrn_norm, conv_w[0], conv_b, lru_ba[0].reshape(1, D_B),
                               lru_bx[0].reshape(1, D_B), lru_lambda], axis=0)
    mix = (mix_norm, w_in[0].astype(BF16), vectors, _block_diag(lru_wa[0]).astype(BF16),
           _block_diag(lru_wx[0]).astype(BF16), w_o[0].astype(BF16))

    xs, *ffn1_bf16 = _ffn_and_round_weights(x_sample.reshape(bs * ts, D_MODEL), ffn1_norm, ffn1_wg[0],
                                            ffn1_wu[0], ffn1_wd[0], name="ffn1_sample")
    ffn1 = (ffn1_norm, *ffn1_bf16)
    xs, hg_s, lru_s, cv_s = _mixer_sample(xs, mix, state_hgrn[0], state_lru[0], state_conv[0])

    yp, ys, hg_p, lru_p, cv_p = _layer_prompt(x_prompt, xs, ffn1, mix, ffn2, fn)

    return (yp.reshape(bp, tp, D_MODEL), ys.reshape(bs, ts, D_MODEL),
            hg_p[None], lru_p.reshape(1, bp, D_B), cv_p[None],
            hg_s[None], lru_s[None], cv_s[None])
```

```python
import functools
import math

import jax
import jax.numpy as jnp
from jax import lax
from jax.experimental import pallas as pl
from jax.experimental.pallas import tpu as pltpu

F32 = jnp.float32
BF16 = jnp.bfloat16

D_MODEL = 1024
D_FF = 2816
D_A = 512
D_B = 512
N_HEADS = 4
D_HEAD = 128
D_IN = 4 * D_A + 2 * D_B
CONV_W = 4
LRU_C = 8.0
EPS = 1e-6

SUBLANES = 8
GATE_TILE = 256
NEG_LARGE = -1e30
MIX_CHUNK = 256
FFN_CHUNK = 256
FFN_BOUNDS = (0, 768, 1536, 2304, 2816)
LAYER_FILL_SLOTS = (1, 1, 1, 1, 1, 1, 1, 1, 1, 2, 2, 3)
LAYER_FILL_STEPS = 2
SAMPLE_GROUP = 16
VMEM_LIMIT_BYTES = 56 * 1024 * 1024
LAYER_VMEM_LIMIT_BYTES = 63 * 1024 * 1024

_NT = (((1,), (1,)), ((), ()))


def _dot(a, b):
    return jnp.dot(a, b, preferred_element_type=F32)


def _dot_nt(a, b):
    return lax.dot_general(a, b, _NT, preferred_element_type=F32)


def _rms(x, w):
    ms = jnp.mean(x * x, axis=-1, keepdims=True)
    return x * lax.rsqrt(ms + EPS) * w


def _sigmoid(x):
    return 1.0 / (1.0 + jnp.exp(-x))


def _log1p(x):
    u = 1.0 + x
    return jnp.where(u == 1.0, x, jnp.log(u) * x / (u - 1.0))


def _expm1(x):
    u = jnp.exp(x)
    return jnp.where(u == 1.0, x, (u - 1.0) * x / jnp.log(u))


def _softplus(x):
    return jnp.maximum(x, 0.0) + _log1p(jnp.exp(-jnp.abs(x)))


def _gelu_tanh(x):
    return 0.5 * x * (1.0 + jnp.tanh(math.sqrt(2.0 / math.pi) * (x + 0.044715 * (x * x * x))))


def _split3_bf16(x):
    x1 = x.astype(BF16)
    r1 = x - x1.astype(F32)
    x2 = r1.astype(BF16)
    r2 = r1 - x2.astype(F32)
    return x1, x2, r2.astype(BF16)


def _ffn_kernel(x_ref, nw_ref, wg_ref, wu_ref, wd_ref, o_ref, wgb_ref, wub_ref, wdb_ref, xn_ref, acc_ref):
    c = pl.program_id(0)

    @pl.when(c == 0)
    def _():
        xn_ref[...] = _rms(x_ref[...], nw_ref[...]).astype(BF16)
        acc_ref[...] = jnp.zeros(acc_ref.shape, F32)

    wg = wg_ref[...].astype(BF16)
    wu = wu_ref[...].astype(BF16)
    wd = wd_ref[...].astype(BF16)
    wgb_ref[...] = wg
    wub_ref[...] = wu
    wdb_ref[...] = wd
    xn = xn_ref[...]
    g = _dot(xn, wg)
    u = _dot(xn, wu)
    acc_ref[...] += _dot((g * _sigmoid(g) * u).astype(BF16), wd)

    @pl.when(c == pl.num_programs(0) - 1)
    def _():
        o_ref[...] = x_ref[...] + 0.5 * acc_ref[...]


def _resident(shape):
    zeros = (0,) * len(shape)
    return pl.BlockSpec(shape, lambda *_: zeros, pipeline_mode=pl.Buffered(1))


def _ffn_and_round_weights(x, norm_w, wg, wu, wd, *, name):
    n = x.shape[0]
    assert D_FF % FFN_CHUNK == 0
    cols = pl.BlockSpec((D_MODEL, FFN_CHUNK), lambda c: (0, c))
    rows = pl.BlockSpec((FFN_CHUNK, D_MODEL), lambda c: (c, 0))
    return pl.pallas_call(
        _ffn_kernel,
        out_shape=(jax.ShapeDtypeStruct((n, D_MODEL), F32), jax.ShapeDtypeStruct(wg.shape, BF16),
                   jax.ShapeDtypeStruct(wu.shape, BF16), jax.ShapeDtypeStruct(wd.shape, BF16)),
        grid=(D_FF // FFN_CHUNK,),
        in_specs=[_resident((n, D_MODEL)), _resident((1, D_MODEL)), cols, cols, rows],
        out_specs=(pl.BlockSpec((n, D_MODEL), lambda c: (0, 0)), cols, cols, rows),
        scratch_shapes=[pltpu.VMEM((n, D_MODEL), BF16), pltpu.VMEM((n, D_MODEL), F32)],
        compiler_params=pltpu.CompilerParams(dimension_semantics=("arbitrary",),
                                             vmem_limit_bytes=VMEM_LIMIT_BYTES),
        name=name,
    )(x, norm_w, wg, wu, wd)


class _Rows:
    def __init__(self, ref, start, n):
        self.ref, self.start, self.n = ref, start, n

    def __getitem__(self, idx):
        rows, cols = (slice(None), slice(None)) if idx is Ellipsis else idx
        lo = self.start + (rows.start or 0)
        hi = self.start + (self.n if rows.stop is None else rows.stop)
        return self.ref[lo:hi, cols]


def _mixer_vectors(pv_ref):
    sizes = (2, 1, CONV_W, 1, 1, 1, 1)
    views, start = [], 0
    for n in sizes:
        views.append(_Rows(pv_ref, start, n))
        start += n
    return views


def _hgrn_gates(proj_ref, lb_ref):
    l0 = lb_ref[0:1, :]
    l1 = lb_ref[1:2, :]
    m = jnp.maximum(l0, l1)
    e0 = jnp.exp(l0 - m)
    e1 = jnp.exp(l1 - m)
    lb = e0 / (e0 + e1)
    q = proj_ref[:, 0:D_A]
    fz = proj_ref[:, D_A:2 * D_A]
    f = lb + (1.0 - lb) * _sigmoid(fz)
    return q * _sigmoid(q), 1.0 - f, jnp.log(f)


def _pairwise(qs, k, cum, v, kpad_ref, cpad_ref, vpad_ref, out_ref, block, n_rows):
    pad = kpad_ref.shape[0] - n_rows
    kpad_ref[pl.ds(pad, n_rows), :] = k
    cpad_ref[pl.ds(pad, n_rows), :] = cum
    vpad_ref[pl.ds(pad, n_rows), :] = v
    rmod = lax.broadcasted_iota(jnp.int32, (n_rows, 1), 0) & (block - 1)
    out = [jnp.zeros((n_rows, D_HEAD), F32) for _ in range(N_HEADS)]
    for d in range(block):
        if d == 0:
            p = qs * k
            vs = v
        else:
            ks = kpad_ref[pl.ds(pad - d, n_rows), :]
            cs = cpad_ref[pl.ds(pad - d, n_rows), :]
            vs = vpad_ref[pl.ds(pad - d, n_rows), :]
            p = qs * ks * jnp.exp(cum - cs)
        valid = rmod >= d
        for h in range(N_HEADS):
            hs = slice(h * D_HEAD, (h + 1) * D_HEAD)
            s = jnp.sum(p[:, hs], axis=-1, keepdims=True)
            s = jnp.where(valid, s, 0.0)
            out[h] = out[h] + s * vs[:, hs]
    for h in range(N_HEADS):
        out_ref[:, h * D_HEAD:(h + 1) * D_HEAD] = out[h]


def _lru_coeffs(xc, wa_ref, ba_ref, wx_ref, bx_ref, lam_ref):
    xcb = xc.astype(BF16)

    def gate(w_ref, b_ref):
        tiles = [slice(j, j + GATE_TILE) for j in range(0, D_B, GATE_TILE)]
        z = jnp.concatenate([_dot(xcb[:, t], w_ref[t, t]) for t in tiles], axis=-1)
        return _sigmoid(z + b_ref[...])

    r = gate(wa_ref, ba_ref)
    i = gate(wx_ref, bx_ref)
    log_a = -LRU_C * r * _softplus(-lam_ref[...])
    a = jnp.exp(log_a)
    mult = jnp.sqrt(-_expm1(2.0 * log_a))
    return a, mult, i


def _scan_rows(a, u, rowpos, span):
    s = 1
    while s < span:
        keep = rowpos >= s
        a_sh = jnp.where(keep, pltpu.roll(a, s, 0), 1.0)
        u_sh = jnp.where(keep, pltpu.roll(u, s, 0), 0.0)
        u = a * u_sh + u
        a = a * a_sh
        s *= 2
    return a, u


class _HeadCols:
    def __init__(self, ref, base):
        self.ref, self.base = ref, base

    def __getitem__(self, idx):
        h, rows = (idx[0], idx[1]) if isinstance(idx, tuple) else (idx, slice(None))
        return self.ref[rows, self.base + h * D_HEAD:self.base + (h + 1) * D_HEAD]


def _row_bcast(ref, h, r, n):
    return jnp.broadcast_to(ref[h, pl.ds(r, 1), :], (n, D_HEAD))


def _pairwise_tile(qs_ref, k_ref, cum_ref, v_ref, out_ref, h, n_rows):
    pos = lax.broadcasted_iota(jnp.int32, (SUBLANES, 1), 0)
    for j in range(n_rows // SUBLANES):
        rows = pl.ds(SUBLANES * j, SUBLANES)
        q_t = qs_ref[h, rows, :]
        c_t = cum_ref[h, rows, :]
        acc = jnp.zeros((SUBLANES, D_HEAD), F32)
        for s in range(SUBLANES):
            src = SUBLANES * j + s
            d = c_t - _row_bcast(cum_ref, h, src, SUBLANES)
            if s > 0:
                d = jnp.where(pos >= s, d, NEG_LARGE)
            p = q_t * _row_bcast(k_ref, h, src, SUBLANES) * jnp.exp(d)
            acc = acc + jnp.sum(p, axis=-1, keepdims=True) * _row_bcast(v_ref, h, src, SUBLANES)
        out_ref[rows, :] = acc


def _level_operands(qs_ref, k_ref, cum_ref, h, n_rows, L):
    half = L // 2
    zeros = jnp.zeros((half, D_HEAD), F32)
    qparts, kparts = [], []
    for b in range(n_rows // L):
        r0 = b * L
        first = pl.ds(r0, half)
        second = pl.ds(r0 + half, half)
        mid = _row_bcast(cum_ref, h, r0 + half - 1, half)
        kparts += [k_ref[h, first, :] * jnp.exp(mid - cum_ref[h, first, :]), zeros]
        qparts += [zeros, qs_ref[h, second, :] * jnp.exp(cum_ref[h, second, :] - mid)]
    return jnp.concatenate(qparts, axis=0).astype(BF16), jnp.concatenate(kparts, axis=0).astype(BF16)


def _scan_tiles(a, u, carry):
    n = a.shape[0]
    a3 = a.reshape(n // SUBLANES, SUBLANES, D_B)
    u3 = u.reshape(n // SUBLANES, SUBLANES, D_B)
    pos = lax.broadcasted_iota(jnp.int32, (1, SUBLANES, 1), 1)
    s = 1
    while s < SUBLANES:
        keep = pos >= s
        a_sh = jnp.where(keep, pltpu.roll(a3, s, 1), 1.0)
        u_sh = jnp.where(keep, pltpu.roll(u3, s, 1), 0.0)
        u3 = a3 * u_sh + u3
        a3 = a3 * a_sh
        s *= 2
    tiles = []
    for j in range(n // SUBLANES):
        hj = u3[j] + a3[j] * carry
        tiles.append(hj)
        carry = hj[SUBLANES - 1:SUBLANES, :]
    return jnp.concatenate(tiles, axis=0)


class _SwigluChunks:
    def __init__(self, x, nw_ref, wg_ref, wu_ref, wd_ref):
        self.x = x
        self.xn = _rms(x, nw_ref[...]).astype(BF16)
        self.wg_ref, self.wu_ref, self.wd_ref = wg_ref, wu_ref, wd_ref
        self.hidden = {}
        self.acc = None

    @staticmethod
    def _cols(c):
        return slice(FFN_BOUNDS[c], FFN_BOUNDS[c + 1])

    def up(self, c):
        g = _dot(self.xn, self.wg_ref[:, self._cols(c)])
        u = _dot(self.xn, self.wu_ref[:, self._cols(c)])
        self.hidden[c] = (g * _sigmoid(g) * u).astype(BF16)

    def down(self, c):
        d = _dot(self.hidden.pop(c), self.wd_ref[self._cols(c), :])
        self.acc = d if self.acc is None else self.acc + d

    def result(self):
        assert not self.hidden
        return self.x + 0.5 * self.acc

    def pieces(self):
        n = len(FFN_BOUNDS) - 1
        out = [functools.partial(self.up, 0)]
        for c in range(1, n):
            out += [functools.partial(self.up, c), functools.partial(self.down, c - 1)]
        return out + [functools.partial(self.down, n - 1)]


def _layer_prompt_kernel(x_ref, xs_ref, norms_ref, wg1_ref, wu1_ref, wd1_ref,
                         win_ref, pv_ref, wa_ref, wx_ref, wo_ref,
                         wg2_ref, wu2_ref, wd2_ref,
                         y_ref, ys_ref, shg_ref, slru_ref, scv_ref,
                         x1_ref, x2_ref, proj_ref, st_ref, h_ref, xb_ref, cum_all_ref,
                         opw_ref, obuf_ref, *, chunk, chunks_per_seq, n_chunks):
    C = chunk
    s = pl.program_id(0)
    lb_ref, hn_ref, cw_ref, cb_ref, ba_ref, bx_ref, lam_ref = _mixer_vectors(pv_ref)
    n1_ref, mn_ref, n2_ref, fn_ref = (_Rows(norms_ref, r, 1) for r in range(4))

    @pl.when(s == 0)
    def _():
        x1_ref[...] = jnp.zeros(x1_ref.shape, F32)
        st_ref[...] = jnp.zeros(st_ref.shape, F32)
        h_ref[...] = jnp.zeros(h_ref.shape, F32)
        xb_ref[0:8, :] = jnp.zeros((8, D_B), F32)

    @pl.when(s < LAYER_FILL_STEPS)
    def _():
        x2_ref[...] = xs_ref[...]

    ffn2 = _SwigluChunks(x2_ref[...], n2_ref, wg2_ref, wu2_ref, wd2_ref)
    ffn1 = _SwigluChunks(x_ref[...], n1_ref, wg1_ref, wu1_ref, wd1_ref)

    pos_in_seq = lax.rem(s + (chunks_per_seq - 1), chunks_per_seq)
    first = pos_in_seq == 0
    keep = jnp.where(first, 0.0, 1.0)
    x = x1_ref[...]
    xn = _rms(x, mn_ref[...]).astype(BF16)
    proj_ref[...] = _dot(xn, win_ref[...])
    fill = ffn2.pieces() + ffn1.pieces()
    slots = iter(LAYER_FILL_SLOTS)

    def run_fill():
        for _ in range(next(slots)):
            fill.pop(0)()

    run_fill()

    qs, k, g = _hgrn_gates(proj_ref, lb_ref)
    row = lax.broadcasted_iota(jnp.int32, (C, C), 0)
    col = lax.broadcasted_iota(jnp.int32, (C, C), 1)
    tri = jnp.where(col <= row, 1.0, 0.0).astype(BF16)
    g1, g2, g3 = _split3_bf16(g)
    cum_all_ref[...] = _dot(tri, g1) + _dot(tri, g2) + _dot(tri, g3)
    proj_ref[:, 0:D_A] = qs
    proj_ref[:, D_A:2 * D_A] = k
    qs_ref, k_ref, v_ref = (_HeadCols(proj_ref, base) for base in (0, D_A, 2 * D_A))
    cum_ref = _HeadCols(cum_all_ref, 0)

    levels = []
    L = 2 * SUBLANES
    while L <= C:
        shift = L.bit_length() - 1
        levels.append((L, None if L == C else (row >> shift) == (col >> shift)))
        L *= 2

    gate = proj_ref[:, 3 * D_A:4 * D_A]
    for h in range(N_HEADS):
        hs = slice(h * D_HEAD, (h + 1) * D_HEAD)
        run_fill()
        _pairwise_tile(qs_ref, k_ref, cum_ref, v_ref, opw_ref, h, C)
        scores = jnp.zeros((C, C), F32)
        for L, same in levels:
            qt, kt = _level_operands(qs_ref, k_ref, cum_ref, h, C, L)
            sc = _dot_nt(qt, kt)
            scores = scores + (sc if same is None else jnp.where(same, sc, 0.0))
        run_fill()
        cum_h = cum_ref[h]
        cl = _row_bcast(cum_ref, h, C - 1, C)
        qbar = (qs_ref[h] * jnp.exp(cum_h)).astype(BF16)
        khat = (k_ref[h] * jnp.exp(cl - cum_h)).astype(BF16)
        dec = jnp.exp(cum_ref[h, pl.ds(C - 1, 1), :])
        v_h = v_ref[h]
        st = st_ref[h] * keep
        o = _dot(scores.astype(BF16), v_h.astype(BF16)) + _dot_nt(qbar, st.astype(BF16)) + opw_ref[...]
        st_ref[h] = st * dec + _dot(v_h.T.astype(BF16), khat)
        ms = jnp.mean(o * o, axis=-1, keepdims=True)
        obuf_ref[:, hs] = (o * lax.rsqrt(ms + EPS) * hn_ref[:, hs] * _sigmoid(gate[:, hs])).astype(BF16)

    run_fill()
    y_ref[...] = _rms(ffn2.result(), fn_ref[...])

    rowi = lax.broadcasted_iota(jnp.int32, (C, 1), 0)
    xb = proj_ref[:, 4 * D_A:4 * D_A + D_B]
    yb = proj_ref[:, 4 * D_A + D_B:D_IN]
    xb_ref[0:8, :] = xb_ref[0:8, :] * keep
    xb_ref[pl.ds(8, C), :] = xb
    xc = (cb_ref[...] + cw_ref[3:4, :] * xb + cw_ref[2:3, :] * xb_ref[pl.ds(7, C), :]
          + cw_ref[1:2, :] * xb_ref[pl.ds(6, C), :] + cw_ref[0:1, :] * xb_ref[pl.ds(5, C), :])
    xb_ref[0:8, :] = xb[C - 8:C, :]
    a, mult, gate_i = _lru_coeffs(xc, wa_ref, ba_ref, wx_ref, bx_ref, lam_ref)
    run_fill()
    mult = jnp.where((rowi == 0) & first, 1.0, mult)
    hseq = _scan_tiles(a, mult * gate_i * xc, h_ref[0:1, :] * keep)
    h_ref[0:1, :] = hseq[C - 1:C, :]
    obuf_ref[:, D_A:D_A + D_B] = (hseq * _gelu_tanh(yb)).astype(BF16)

    x2_ref[...] = x + _dot(obuf_ref[...], wo_ref[...])
    run_fill()
    assert not fill and next(slots, None) is None
    x1_ref[...] = ffn1.result()

    @pl.when(s < LAYER_FILL_STEPS)
    def _():
        ys_ref[...] = y_ref[...]

    @pl.when((pos_in_seq == chunks_per_seq - 1) & (s >= 1) & (s <= n_chunks))
    def _():
        for h in range(N_HEADS):
            shg_ref[0, h] = st_ref[h].T
        slru_ref[0] = hseq[C - SUBLANES:C, :]
        scv_ref[0] = xb[C - SUBLANES:C, :]


def _layer_prompt(x, xs, norms, ffn1_w, mix_w, ffn2_w):
    b, t, _ = x.shape
    C = MIX_CHUNK
    assert t % C == 0 and C % (2 * SUBLANES) == 0 and C & (C - 1) == 0
    assert FFN_BOUNDS[0] == 0 and FFN_BOUNDS[-1] == D_FF
    assert sum(LAYER_FILL_SLOTS) == 4 * (len(FFN_BOUNDS) - 1) and len(LAYER_FILL_SLOTS) == 2 * N_HEADS + 4
    assert xs.shape == (LAYER_FILL_STEPS * C, D_MODEL), "sample rows must exactly fill the pipeline fill steps"
    nt = t // C
    n_chunks = b * nt
    rows_in = pl.BlockSpec((C, D_MODEL), lambda s: (jnp.minimum(s, n_chunks - 1), 0))
    rows_out = pl.BlockSpec((C, D_MODEL), lambda s: (jnp.clip(s - 2, 0, n_chunks - 1), 0))
    fill_index = lambda s: (jnp.minimum(s, LAYER_FILL_STEPS - 1), 0)
    rows_fill = pl.BlockSpec((C, D_MODEL), fill_index, pipeline_mode=pl.Buffered(1))
    seq_of = lambda s: jnp.clip((s - 1) // nt, 0, b - 1)
    weights = (norms, *ffn1_w, *mix_w, *ffn2_w)
    out_shape = (jax.ShapeDtypeStruct((b * t, D_MODEL), F32),
                 jax.ShapeDtypeStruct(xs.shape, F32),
                 jax.ShapeDtypeStruct((b, N_HEADS, D_HEAD, D_HEAD), F32),
                 jax.ShapeDtypeStruct((b, SUBLANES, D_B), F32),
                 jax.ShapeDtypeStruct((b, SUBLANES, D_B), F32))
    out_specs = (rows_out, rows_fill,
                 pl.BlockSpec((1, N_HEADS, D_HEAD, D_HEAD), lambda s: (seq_of(s), 0, 0, 0)),
                 pl.BlockSpec((1, SUBLANES, D_B), lambda s: (seq_of(s), 0, 0)),
                 pl.BlockSpec((1, SUBLANES, D_B), lambda s: (seq_of(s), 0, 0)))
    scratch = [pltpu.VMEM((C, D_MODEL), F32),
               pltpu.VMEM((C, D_MODEL), F32),
               pltpu.VMEM((C, D_IN), F32),
               pltpu.VMEM((N_HEADS, D_HEAD, D_HEAD), F32),
               pltpu.VMEM((SUBLANES, D_B), F32),
               pltpu.VMEM((C + 8, D_B), F32),
               pltpu.VMEM((C, D_A), F32),
               pltpu.VMEM((C, D_HEAD), F32),
               pltpu.VMEM((C, D_A + D_B), BF16)]
    return pl.pallas_call(
        functools.partial(_layer_prompt_kernel, chunk=C, chunks_per_seq=nt, n_chunks=n_chunks),
        out_shape=out_shape,
        grid=(n_chunks + LAYER_FILL_STEPS,),
        in_specs=[rows_in, rows_fill] + [_resident(w.shape) for w in weights],
        out_specs=out_specs,
        scratch_shapes=scratch,
        compiler_params=pltpu.CompilerParams(dimension_semantics=("arbitrary",),
                                             vmem_limit_bytes=LAYER_VMEM_LIMIT_BYTES),
        name="layer_prompt",
    )(x.reshape(b * t, D_MODEL), xs, *weights)


def _mixer_sample_kernel(x_ref, mn_ref, win_ref, pv_ref, wa_ref, wx_ref, wo_ref, s0_ref, rowstate_ref,
                         y_ref, s1_ref, hseq_ref, xb_out_ref,
                         proj_ref, xb_ref, kpad_ref, cpad_ref, vpad_ref, qbar_ref, khat_ref, ecum_ref,
                         opw_ref, ost_ref, ob_ref, *, steps):
    T = steps
    N = x_ref.shape[0]
    G = s0_ref.shape[0]
    R = G * T
    i = pl.program_id(0)
    tshift = T.bit_length() - 1
    lb_ref, hn_ref, cw_ref, cb_ref, ba_ref, bx_ref, lam_ref = _mixer_vectors(pv_ref)

    @pl.when(i == 0)
    def _():
        pad = kpad_ref.shape[0] - N
        zpad = jnp.zeros((pad, D_A), F32)
        kpad_ref[0:pad, :] = zpad
        cpad_ref[0:pad, :] = zpad
        vpad_ref[0:pad, :] = zpad
        xb_ref[0:8, :] = jnp.zeros((8, D_B), F32)

        xn = _rms(x_ref[...], mn_ref[...]).astype(BF16)
        proj_ref[...] = _dot(xn, win_ref[...])

        qs, k, g = _hgrn_gates(proj_ref, lb_ref)
        v = proj_ref[:, 2 * D_A:3 * D_A]
        tpos = lax.broadcasted_iota(jnp.int32, (N, 1), 0) & (T - 1)
        cum = g
        tail = jnp.zeros_like(g)
        for d in range(1, T):
            cum = cum + jnp.where(tpos >= d, pltpu.roll(g, d, 0), 0.0)
            tail = tail + jnp.where(tpos < T - d, pltpu.roll(g, N - d, 0), 0.0)
        _pairwise(qs, k, cum, v, kpad_ref, cpad_ref, vpad_ref, opw_ref, T, N)
        ecum = jnp.exp(cum)
        ecum_ref[...] = ecum
        qbar_ref[...] = qs * ecum
        khat_ref[...] = k * jnp.exp(tail)

        xb = proj_ref[:, 4 * D_A:4 * D_A + D_B]
        yb = proj_ref[:, 4 * D_A + D_B:D_IN]
        xb_ref[pl.ds(8, N), :] = xb
        xb_out_ref[...] = xb
        xc = cb_ref[...] + cw_ref[CONV_W - 1:CONV_W, :] * xb
        for d in range(1, CONV_W):
            prev = jnp.where(tpos >= d, xb_ref[pl.ds(8 - d, N), :], rowstate_ref[d])
            xc = xc + cw_ref[CONV_W - 1 - d:CONV_W - d, :] * prev
        a, mult, gate_i = _lru_coeffs(xc, wa_ref, ba_ref, wx_ref, bx_ref, lam_ref)
        a_run, h_run = _scan_rows(a, mult * gate_i * xc, tpos, T)
        hseq = h_run + a_run * rowstate_ref[0]
        hseq_ref[...] = hseq
        ob_ref[...] = hseq * _gelu_tanh(yb)

    r0 = pl.multiple_of(i * R, R)
    rows = pl.ds(r0, R)
    row8 = lax.broadcasted_iota(jnp.int32, (8, 1), 0)
    seq_of_row = lax.broadcasted_iota(jnp.int32, (R, 1), 0) >> tshift
    for h in range(N_HEADS):
        hs = slice(h * D_HEAD, (h + 1) * D_HEAD)
        khat_t = khat_ref[rows, hs].T.astype(BF16)
        dec_t = ecum_ref[rows, hs].T
        v_g = proj_ref[rows, 2 * D_A + h * D_HEAD:2 * D_A + (h + 1) * D_HEAD]
        for p in range(R // 8):
            tile = pl.ds(pl.multiple_of(r0 + 8 * p, 8), 8)
            q8 = qbar_ref[tile, hs].astype(BF16)
            o8 = jnp.zeros((8, D_HEAD), F32)
            for j in range(8 // T):
                b = p * (8 // T) + j
                s0 = s0_ref[b, h]
                ob = _dot(q8, s0.astype(BF16))
                o8 = jnp.where((row8 >= j * T) & (row8 < (j + 1) * T), ob, o8)
                v_b = jnp.where(seq_of_row == b, v_g, 0.0).astype(BF16)
                s1_ref[b, h] = s0 * dec_t[:, b * T + T - 1:b * T + T] + _dot(khat_t, v_b)
            ost_ref[tile, hs] = o8

    @pl.when(i == pl.num_programs(0) - 1)
    def _():
        gate = proj_ref[:, 3 * D_A:4 * D_A]
        outs = []
        for h in range(N_HEADS):
            hs = slice(h * D_HEAD, (h + 1) * D_HEAD)
            o = opw_ref[:, hs] + ost_ref[:, hs]
            ms = jnp.mean(o * o, axis=-1, keepdims=True)
            outs.append(o * lax.rsqrt(ms + EPS) * hn_ref[:, hs] * _sigmoid(gate[:, hs]))
        o = jnp.concatenate(outs + [ob_ref[...]], axis=-1).astype(BF16)
        y_ref[...] = x_ref[...] + _dot(o, wo_ref[...])


def _mixer_sample(xr, mix, s0, h0, buf):
    nb, T = s0.shape[0], xr.shape[0] // s0.shape[0]
    assert T == CONV_W and 8 % T == 0, "sample kernel assumes DEC_SEQ == CONV_W == 4"
    G = SAMPLE_GROUP
    N = nb * T
    assert nb % G == 0 and (G * T) % 8 == 0
    rowstate = [jnp.broadcast_to(h0[:, None, :], (nb, T, D_B))]
    rowstate += [jnp.concatenate([buf[:, CONV_W - 1 - d:, :], jnp.zeros((nb, T - d, D_B), F32)], axis=1)
                 for d in range(1, CONV_W)]
    rowstate = jnp.stack(rowstate).reshape(CONV_W, N, D_B)
    whole_out = lambda n: pl.BlockSpec((N, n), lambda i: (0, 0))
    state = pl.BlockSpec((G, N_HEADS, D_HEAD, D_HEAD), lambda i: (i, 0, 0, 0))
    out_shape = (jax.ShapeDtypeStruct((N, D_MODEL), F32),
                 jax.ShapeDtypeStruct((nb, N_HEADS, D_HEAD, D_HEAD), F32),
                 jax.ShapeDtypeStruct((N, D_B), F32),
                 jax.ShapeDtypeStruct((N, D_B), F32))
    half = lambda: pltpu.VMEM((N, D_A), F32)
    scratch = [pltpu.VMEM((N, D_IN), F32),
               pltpu.VMEM((N + 8, D_B), F32),
               pltpu.VMEM((N + 8, D_A), F32), pltpu.VMEM((N + 8, D_A), F32), pltpu.VMEM((N + 8, D_A), F32),
               half(), half(), half(),
               half(), half(),
               pltpu.VMEM((N, D_B), F32)]
    y, s1, hseq, xb = pl.pallas_call(
        functools.partial(_mixer_sample_kernel, steps=T),
        out_shape=out_shape,
        grid=(nb // G,),
        in_specs=[_resident(xr.shape)] + [_resident(a.shape) for a in mix] + [state, _resident(rowstate.shape)],
        out_specs=(whole_out(D_MODEL), state, whole_out(D_B), whole_out(D_B)),
        scratch_shapes=scratch,
        compiler_params=pltpu.CompilerParams(dimension_semantics=("arbitrary",),
                                             vmem_limit_bytes=LAYER_VMEM_LIMIT_BYTES),
        name="mixer_sample",
    )(xr, *mix, s0, rowstate)
    hseq = hseq.reshape(nb, T, D_B)
    xb = xb.reshape(nb, T, D_B)
    return y, s1, hseq[:, T - 1], xb[:, T - (CONV_W - 1):]


def _block_diag(w):
    n, bi, bj = w.shape
    eye = jnp.eye(n, dtype=w.dtype)
    return (w[:, :, None, :] * eye[:, None, :, None]).reshape(n * bi, n * bj)


def kernel(x_prompt, x_sample, state_hgrn, state_lru, state_conv, ffn1_norm, ffn1_wg, ffn1_wu, ffn1_wd,
           mix_norm, w_in, hgrn_lb, hgrn_norm, conv_w, conv_b, lru_wa, lru_ba, lru_wx, lru_bx, lru_lambda,
           w_o, ffn2_norm, ffn2_wg, ffn2_wu, ffn2_wd, final_norm):
    assert ffn1_norm.shape[0] == 1 and hgrn_lb.shape[0] == 2, "single-layer model"
    bp, tp, _ = x_prompt.shape
    bs, ts, _ = x_sample.shape

    ffn2_w = (ffn2_wg[0].astype(BF16), ffn2_wu[0].astype(BF16), ffn2_wd[0].astype(BF16))
    norms = jnp.concatenate([ffn1_norm, mix_norm, ffn2_norm, final_norm.reshape(1, D_MODEL)], axis=0)
    vectors = jnp.concatenate([hgrn_lb, hgrn_norm, conv_w[0], conv_b, lru_ba[0].reshape(1, D_B),
                               lru_bx[0].reshape(1, D_B), lru_lambda], axis=0)
    mix = (mix_norm, w_in[0].astype(BF16), vectors, _block_diag(lru_wa[0]).astype(BF16),
           _block_diag(lru_wx[0]).astype(BF16), w_o[0].astype(BF16))

    xs, *ffn1_w = _ffn_and_round_weights(x_sample.reshape(bs * ts, D_MODEL), ffn1_norm, ffn1_wg[0],
                                         ffn1_wu[0], ffn1_wd[0], name="ffn1_sample")
    xs, hg_s, lru_s, cv_s = _mixer_sample(xs, mix, state_hgrn[0], state_lru[0], state_conv[0])

    yp, ys, hg_p, lru_tail, xb_tail = _layer_prompt(x_prompt, xs, norms, ffn1_w, mix[1:], ffn2_w)

    return (yp.reshape(bp, tp, D_MODEL), ys.reshape(bs, ts, D_MODEL),
            hg_p[None], lru_tail[None, :, SUBLANES - 1], xb_tail[None, :, SUBLANES - (CONV_W - 1):],
            hg_s[None], lru_s[None], cv_s[None])
```
